```python
import math
import jax
import jax.numpy as jnp
from jax import lax
import numpy as np

D_MODEL = 1024
BATCH = 16
SEQ = 4096
DEPTH = 4

GRID_W = 64
CTX_LEN = 256
MIX_WIDTH = D_MODEL
GROUP_WIDTH = MIX_WIDTH // 4
Q_BLOCK = 128
ROPE_BASE = 10000.0
EPS = 1e-6

MLA_HEADS = 4
MLA_NOPE = 64
MLA_ROPE = 32
MLA_V = GROUP_WIDTH // MLA_HEADS
MLA_Q_RANK = 192
MLA_KV_RANK = 128
GQA_HEADS = 4
GQA_KV_HEADS = 2
GQA_HEAD_DIM = GROUP_WIDTH // GQA_HEADS
SSM_GROUP = 16
SSM_GROUPS = GROUP_WIDTH // SSM_GROUP
SSM_STATE = 64
HY_WIDTH = GROUP_WIDTH
HY_ORDER = 2
HY_EMB = 33
HY_BANDS = (HY_EMB - 1) // 2
HY_HIDDEN = 64
HY_CONV = 3
HY_FAST_DECAY = 0.3
HY_SLOW_DECAY = 1.5
HY_DECAY_TARGET = 1e-2

MLA_COLS = MLA_Q_RANK + MLA_KV_RANK + MLA_ROPE + GROUP_WIDTH
GQA_COLS = (GQA_HEADS + 2 * GQA_KV_HEADS) * GQA_HEAD_DIM + GROUP_WIDTH
SSM_COLS = 2 * GROUP_WIDTH
HY_COLS = (HY_ORDER + 1) * HY_WIDTH + GROUP_WIDTH
IN_COLS = MLA_COLS + GQA_COLS + SSM_COLS + HY_COLS

kernel_name = 'hybrid_parallel_group_dit_block'


def rms_norm(x, g):
    xf = x.astype(jnp.float32)
    y = xf * lax.rsqrt(jnp.mean(xf * xf, axis=-1, keepdims=True) + EPS)
    return (y * g.astype(jnp.float32)).astype(x.dtype)


def grid_positions(n_rows):
    row = jnp.repeat(jnp.arange(n_rows, dtype=jnp.float32), GRID_W)
    col = jnp.tile(jnp.arange(GRID_W, dtype=jnp.float32), n_rows)
    return row, col


def _rotate(x, pos):
    h = x.shape[-1]
    inv = ROPE_BASE ** (-jnp.arange(0, h, 2, dtype=jnp.float32) / h)
    ang = pos[:, None] * inv[None, :]
    cos = jnp.cos(ang)[None, :, None, :].astype(x.dtype)
    sin = jnp.sin(ang)[None, :, None, :].astype(x.dtype)
    x1, x2 = x[..., :h // 2], x[..., h // 2:]
    return jnp.concatenate([x1 * cos - x2 * sin, x1 * sin + x2 * cos], axis=-1)


def axial_rope(x, row, col):
    half = x.shape[-1] // 2
    return jnp.concatenate([_rotate(x[..., :half], row), _rotate(x[..., half:], col)], axis=-1)


def block_attention(q, k, v, scale):
    b, n, h, dk = q.shape
    hk = k.shape[2]
    dv = v.shape[-1]
    nb = n // Q_BLOCK
    qb = q.reshape(b, nb, Q_BLOCK, hk, h // hk, dk).transpose(1, 0, 2, 3, 4, 5)

    def attend(qblk):
        s = jnp.einsum('bqhgd,bkhd->bhgqk', qblk, k, preferred_element_type=jnp.float32) * scale
        p = jax.nn.softmax(s, axis=-1).astype(v.dtype)
        return jnp.einsum('bhgqk,bkhd->bqhgd', p, v)

    o = lax.map(attend, qb)
    return o.transpose(1, 0, 2, 3, 4, 5).reshape(b, n, h, dv)


def mla_mixer(p_lat, p_ctx, g_cq, w_uq, g_ckv, w_ukv, row, col, need_ctx):
    r1 = MLA_Q_RANK
    r2 = r1 + MLA_KV_RANK
    r3 = r2 + MLA_ROPE
    scale = (MLA_NOPE + MLA_ROPE) ** -0.5

    def keys(p, rotary):
        b, n, _ = p.shape
        kv = (rms_norm(p[..., r1:r2], g_ckv) @ w_ukv).reshape(b, n, MLA_HEADS, MLA_NOPE + MLA_V)
        k_rope = p[..., r2:r3][:, :, None, :]
        if rotary:
            k_rope = axial_rope(k_rope, row, col)
        k = jnp.concatenate([kv[..., :MLA_NOPE], jnp.broadcast_to(k_rope, (b, n, MLA_HEADS, MLA_ROPE))], axis=-1)
        return k, kv[..., MLA_NOPE:]

    def attend(p, k, v, rotary):
        b, n, _ = p.shape
        q = (rms_norm(p[..., :r1], g_cq) @ w_uq).reshape(b, n, MLA_HEADS, MLA_NOPE + MLA_ROPE)
        q_rope = q[..., MLA_NOPE:]
        if rotary:
            q_rope = axial_rope(q_rope, row, col)
        q = jnp.concatenate([q[..., :MLA_NOPE], q_rope], axis=-1)
        o = block_attention(q, k, v, scale).reshape(b, n, GROUP_WIDTH)
        return o * jax.nn.silu(p[..., r3:])

    k_c, v_c = keys(p_ctx, False)
    k_l, v_l = keys(p_lat, True)
    o_l = attend(p_lat, jnp.concatenate([k_l, k_c], axis=1), jnp.concatenate([v_l, v_c], axis=1), True)
    o_c = attend(p_ctx, k_c, v_c, False) if need_ctx else None
    return o_l, o_c


def gqa_mixer(p_lat, p_ctx, g_q, g_k, row, col, need_ctx):
    qd = GQA_HEADS * GQA_HEAD_DIM
    kd = GQA_KV_HEADS * GQA_HEAD_DIM
    scale = GQA_HEAD_DIM ** -0.5

    def keys(p, rotary):
        b, n, _ = p.shape
        k = rms_norm(p[..., qd:qd + kd].reshape(b, n, GQA_KV_HEADS, GQA_HEAD_DIM), g_k)
        v = p[..., qd + kd:qd + 2 * kd].reshape(b, n, GQA_KV_HEADS, GQA_HEAD_DIM)
        if rotary:
            k = axial_rope(k, row, col)
        return k, v

    def attend(p, k, v, rotary):
        b, n, _ = p.shape
        q = rms_norm(p[..., :qd].reshape(b, n, GQA_HEADS, GQA_HEAD_DIM), g_q)
        if rotary:
            q = axial_rope(q, row, col)
        o = block_attention(q, k, v, scale).reshape(b, n, GROUP_WIDTH)
        return o * jax.nn.silu(p[..., qd + 2 * kd:])

    k_c, v_c = keys(p_ctx, False)
    k_l, v_l = keys(p_lat, True)
    o_l = attend(p_lat, jnp.concatenate([k_l, k_c], axis=1), jnp.concatenate([v_l, v_c], axis=1), True)
    o_c = attend(p_ctx, k_c, v_c, False) if need_ctx else None
    return o_l, o_c


def ssm_discretise(lam_re, lam_im, log_step, b_re, b_im):
    lam = lax.complex(lam_re.astype(jnp.float32), lam_im.astype(jnp.float32))
    step = jnp.exp(log_step.astype(jnp.float32))[:, None]
    a_bar = jnp.exp(lam * step)
    b_mat = lax.complex(b_re.astype(jnp.float32), b_im.astype(jnp.float32))
    b_bar = ((a_bar - 1.0) / lam)[..., None] * b_mat
    return a_bar, b_bar


def linear_scan(a_bar, bu, s0, reverse):
    a = jnp.broadcast_to(a_bar, bu.shape)

    def combine(e1, e2):
        return e1[0] * e2[0], e2[0] * e1[1] + e2[1]

    a_cum, h = lax.associative_scan(combine, (a, bu), axis=1, reverse=reverse)
    if s0 is not None:
        h = h + a_cum * s0[:, None]
    return h


def ssm_mixer(p_lat, p_ctx, lam_re, lam_im, log_step, b_re, b_im, c_re, c_im, d_skip, glu_w, glu_b, need_ctx):
    w = GROUP_WIDTH

    def drive(p, b_bar):
        b, n, _ = p.shape
        u = p[..., :w].astype(jnp.float32).reshape(b, n, SSM_GROUPS, SSM_GROUP)
        return jnp.einsum('bngh,gph->bngp', u.astype(jnp.complex64), b_bar)

    def readout(states, c_mat):
        return jnp.einsum('bngp,ghp->bngh', states, c_mat).real

    def finish(p, y):
        b, n, _ = p.shape
        u = p[..., :w].astype(jnp.float32)
        y = y.reshape(b, n, w) + d_skip.astype(jnp.float32) * u
        z = jax.nn.gelu(y)
        z = z * jax.nn.sigmoid(z @ glu_w.astype(jnp.float32) + glu_b.astype(jnp.float32))
        return z.astype(p.dtype) * jax.nn.silu(p[..., w:])

    y_lat = []
    y_ctx = []
    for direction, reverse in ((0, False), (1, True)):
        a_bar, b_bar = ssm_discretise(lam_re[direction], lam_im[direction], log_step[direction],
                                      b_re[direction], b_im[direction])
        c_mat = lax.complex(c_re[direction].astype(jnp.float32), c_im[direction].astype(jnp.float32))
        s_ctx = linear_scan(a_bar, drive(p_ctx, b_bar), None, reverse)
        s_end = s_ctx[:, 0] if reverse else s_ctx[:, -1]
        s_lat = linear_scan(a_bar, drive(p_lat, b_bar), s_end, reverse)
        y_lat.append(readout(s_lat, c_mat))
        if need_ctx:
            y_ctx.append(readout(s_ctx, c_mat))
    o_l = finish(p_lat, y_lat[0] + y_lat[1])
    o_c = finish(p_ctx, y_ctx[0] + y_ctx[1]) if need_ctx else None
    return o_l, o_c


def short_conv(x, w, b):
    n = x.shape[1]
    xp = jnp.pad(x, ((0, 0), (1, 1), (0, 0)))
    return xp[:, :n] * w[0] + xp[:, 1:n + 1] * w[1] + xp[:, 2:] * w[2] + b


def hyena_kernel_freq(n, w1, b1, fr1, w2, b2, fr2, w3):
    f32 = jnp.float32
    t = jnp.linspace(0.0, 1.0, n, dtype=f32)[:, None]
    omega = 2.0 * math.pi * jnp.arange(n, dtype=f32)[:, None] / n
    bands = jnp.linspace(1e-4, HY_BANDS - 1, HY_BANDS, dtype=f32)[None, :]
    z = jnp.concatenate([t, jnp.cos(bands * omega), -jnp.sin(bands * omega)], axis=-1)
    h = jnp.sin(fr1.astype(f32) * (z @ w1.astype(f32) + b1.astype(f32)))
    h = jnp.sin(fr2.astype(f32) * (h @ w2.astype(f32) + b2.astype(f32)))
    h = (h @ w3.astype(f32)).reshape(n, HY_ORDER, 2, HY_WIDTH)
    max_decay = math.log(HY_DECAY_TARGET) / HY_FAST_DECAY
    min_decay = math.log(HY_DECAY_TARGET) / HY_SLOW_DECAY
    deltas = jnp.linspace(min_decay, max_decay, HY_WIDTH, dtype=f32)
    h = h * jnp.exp(-t[:, :, None, None] * jnp.abs(deltas))
    h = h * lax.rsqrt(jnp.sum(h * h, axis=(0, 2), keepdims=True) + EPS)
    fwd, bwd = h[:, :, 0], h[:, :, 1]
    zero = jnp.zeros((1, HY_ORDER, HY_WIDTH), f32)
    k_circ = jnp.concatenate([fwd, zero, bwd[1:][::-1]], axis=0)
    return jnp.fft.rfft(k_circ, axis=0)


def long_conv(u, k_f, bias):
    n = u.shape[1]
    uf = jnp.fft.rfft(u, n=2 * n, axis=1)
    y = jnp.fft.irfft(uf * k_f[None], n=2 * n, axis=1)[:, :n]
    return y + u * bias.astype(jnp.float32)


def hyena_mixer(p_lat, p_ctx, conv_w, conv_b, w1, b1, fr1, w2, b2, fr2, w3, bias, need_ctx):
    w = HY_WIDTH

    def run(p):
        n = p.shape[1]
        proj = short_conv(p[..., :(HY_ORDER + 1) * w], conv_w, conv_b)
        v = proj[..., :w].astype(jnp.float32)
        x1 = proj[..., w:2 * w].astype(jnp.float32)
        x2 = proj[..., 2 * w:3 * w].astype(jnp.float32)
        k_f = hyena_kernel_freq(n, w1, b1, fr1, w2, b2, fr2, w3)
        z = x1 * long_conv(v, k_f[:, 0], bias[0])
        z = x2 * long_conv(z, k_f[:, 1], bias[1])
        return z.astype(p.dtype) * jax.nn.silu(p[..., (HY_ORDER + 1) * w:])

    o_l = run(p_lat)
    o_c = run(p_ctx) if need_ctx else None
    return o_l, o_c


def modulation(cvec, w_mod, b_mod):
    m = jax.nn.silu(cvec) @ w_mod + b_mod
    return jnp.split(m, 3, axis=-1)


def setup_inputs(seed: int = 0) -> dict:
    key = jax.random.key(seed)
    keys = iter(jax.random.split(key, 48))
    f32 = jnp.float32

    def normal(shape, scale):
        return jax.random.normal(next(keys), shape, f32) * scale

    def gain(shape):
        return 1.0 + normal(shape, 0.02)

    w = GROUP_WIDTH
    ssm_shape = (DEPTH, 2, SSM_GROUPS, SSM_STATE)
    x = normal((BATCH, SEQ, D_MODEL), 1.0)
    c = normal((BATCH, D_MODEL), 1.0)
    ctx = normal((BATCH, CTX_LEN, D_MODEL), 1.0)
    c_ctx = normal((D_MODEL,), 1.0)
    w_mod = normal((DEPTH, D_MODEL, 3 * D_MODEL), D_MODEL ** -0.5)
    b_mod = normal((DEPTH, 3 * D_MODEL), 0.02)
    g_pre = gain((DEPTH, D_MODEL))
    g_post = gain((DEPTH, D_MODEL))
    w_in = normal((DEPTH, D_MODEL, IN_COLS), D_MODEL ** -0.5)
    w_out = normal((DEPTH, MIX_WIDTH, D_MODEL), MIX_WIDTH ** -0.5)
    mla_g_cq = gain((DEPTH, MLA_Q_RANK))
    mla_w_uq = normal((DEPTH, MLA_Q_RANK, MLA_HEADS * (MLA_NOPE + MLA_ROPE)), MLA_Q_RANK ** -0.5)
    mla_g_ckv = gain((DEPTH, MLA_KV_RANK))
    mla_w_ukv = normal((DEPTH, MLA_KV_RANK, MLA_HEADS * (MLA_NOPE + MLA_V)), MLA_KV_RANK ** -0.5)
    gqa_g_q = gain((DEPTH, GQA_HEAD_DIM))
    gqa_g_k = gain((DEPTH, GQA_HEAD_DIM))
    ssm_lambda_re = -0.5 * jnp.exp(normal(ssm_shape, 0.05))
    ssm_lambda_im = math.pi * jnp.arange(SSM_STATE, dtype=f32) + normal(ssm_shape, 0.01)
    ssm_log_step = jax.random.uniform(next(keys), (DEPTH, 2, SSM_GROUPS), f32, math.log(1e-3), math.log(1e-1))
    ssm_b_re = normal((DEPTH, 2, SSM_GROUPS, SSM_STATE, SSM_GROUP), (2 * SSM_GROUP) ** -0.5)
    ssm_b_im = normal((DEPTH, 2, SSM_GROUPS, SSM_STATE, SSM_GROUP), (2 * SSM_GROUP) ** -0.5)
    ssm_c_re = normal((DEPTH, 2, SSM_GROUPS, SSM_GROUP, SSM_STATE), SSM_STATE ** -0.5)
    ssm_c_im = normal((DEPTH, 2, SSM_GROUPS, SSM_GROUP, SSM_STATE), SSM_STATE ** -0.5)
    ssm_d = normal((DEPTH, w), 1.0)
    ssm_glu_w = normal((DEPTH, w, w), w ** -0.5)
    ssm_glu_b = normal((DEPTH, w), 0.02)
    hy_conv_w = normal((DEPTH, HY_CONV, (HY_ORDER + 1) * HY_WIDTH), HY_CONV ** -0.5)
    hy_conv_b = normal((DEPTH, (HY_ORDER + 1) * HY_WIDTH), 0.02)
    hy_f_w1 = normal((DEPTH, HY_EMB, HY_HIDDEN), HY_EMB ** -0.5)
    hy_f_b1 = normal((DEPTH, HY_HIDDEN), 0.02)
    hy_f_freq1 = gain((DEPTH, HY_HIDDEN))
    hy_f_w2 = normal((DEPTH, HY_HIDDEN, HY_HIDDEN), HY_HIDDEN ** -0.5)
    hy_f_b2 = normal((DEPTH, HY_HIDDEN), 0.02)
    hy_f_freq2 = gain((DEPTH, HY_HIDDEN))
    hy_f_w3 = normal((DEPTH, HY_HIDDEN, HY_ORDER * 2 * HY_WIDTH), HY_HIDDEN ** -0.5)
    hy_bias = normal((DEPTH, HY_ORDER, HY_WIDTH), 0.5)
    return {'x': x, 'c': c, 'ctx': ctx, 'c_ctx': c_ctx, 'w_mod': w_mod, 'b_mod': b_mod,
            'g_pre': g_pre, 'g_post': g_post, 'w_in': w_in, 'w_out': w_out,
            'mla_g_cq': mla_g_cq, 'mla_w_uq': mla_w_uq, 'mla_g_ckv': mla_g_ckv, 'mla_w_ukv': mla_w_ukv,
            'gqa_g_q': gqa_g_q, 'gqa_g_k': gqa_g_k,
            'ssm_lambda_re': ssm_lambda_re, 'ssm_lambda_im': ssm_lambda_im, 'ssm_log_step': ssm_log_step,
            'ssm_b_re': ssm_b_re, 'ssm_b_im': ssm_b_im, 'ssm_c_re': ssm_c_re, 'ssm_c_im': ssm_c_im,
            'ssm_d': ssm_d, 'ssm_glu_w': ssm_glu_w, 'ssm_glu_b': ssm_glu_b,
            'hy_conv_w': hy_conv_w, 'hy_conv_b': hy_conv_b, 'hy_f_w1': hy_f_w1, 'hy_f_b1': hy_f_b1,
            'hy_f_freq1': hy_f_freq1, 'hy_f_w2': hy_f_w2, 'hy_f_b2': hy_f_b2, 'hy_f_freq2': hy_f_freq2,
            'hy_f_w3': hy_f_w3, 'hy_bias': hy_bias}


def reference(x, c, ctx, c_ctx, w_mod, b_mod, g_pre, g_post, w_in, w_out,
              mla_g_cq, mla_w_uq, mla_g_ckv, mla_w_ukv, gqa_g_q, gqa_g_k,
              ssm_lambda_re, ssm_lambda_im, ssm_log_step, ssm_b_re, ssm_b_im, ssm_c_re, ssm_c_im,
              ssm_d, ssm_glu_w, ssm_glu_b,
              hy_conv_w, hy_conv_b, hy_f_w1, hy_f_b1, hy_f_freq1, hy_f_w2, hy_f_b2, hy_f_freq2,
              hy_f_w3, hy_bias):
    n_rows = x.shape[1] // GRID_W
    row, col = grid_positions(n_rows)
    o1 = MLA_COLS
    o2 = o1 + GQA_COLS
    o3 = o2 + SSM_COLS
    for l in range(DEPTH):
        need_ctx = l < DEPTH - 1
        sh_l, sc_l, gt_l = [m[:, None, :] for m in modulation(c, w_mod[l], b_mod[l])]
        sh_c, sc_c, gt_c = modulation(c_ctx, w_mod[l], b_mod[l])
        h_l = rms_norm(x, g_pre[l]) * (1 + sc_l) + sh_l
        h_c = rms_norm(ctx, g_pre[l]) * (1 + sc_c) + sh_c
        p_l = h_l @ w_in[l]
        p_c = h_c @ w_in[l]
        a_l, a_c = mla_mixer(p_l[..., :o1], p_c[..., :o1], mla_g_cq[l], mla_w_uq[l],
                             mla_g_ckv[l], mla_w_ukv[l], row, col, need_ctx)
        g_l, g_c = gqa_mixer(p_l[..., o1:o2], p_c[..., o1:o2], gqa_g_q[l], gqa_g_k[l], row, col, need_ctx)
        s_l, s_c = ssm_mixer(p_l[..., o2:o3], p_c[..., o2:o3], ssm_lambda_re[l], ssm_lambda_im[l],
                             ssm_log_step[l], ssm_b_re[l], ssm_b_im[l], ssm_c_re[l], ssm_c_im[l],
                             ssm_d[l], ssm_glu_w[l], ssm_glu_b[l], need_ctx)
        y_l, y_c = hyena_mixer(p_l[..., o3:], p_c[..., o3:], hy_conv_w[l], hy_conv_b[l],
                               hy_f_w1[l], hy_f_b1[l], hy_f_freq1[l], hy_f_w2[l], hy_f_b2[l],
                               hy_f_freq2[l], hy_f_w3[l], hy_bias[l], need_ctx)
        out_l = jnp.concatenate([a_l, g_l, s_l, y_l], axis=-1) @ w_out[l]
        x = x + gt_l * rms_norm(out_l, g_post[l])
        if need_ctx:
            out_c = jnp.concatenate([a_c, g_c, s_c, y_c], axis=-1) @ w_out[l]
            ctx = ctx + gt_c * rms_norm(out_c, g_post[l])
    return x
```

```python
import functools
import math

import numpy as np
import jax
import jax.numpy as jnp
from jax import lax
from jax.experimental import pallas as pl
from jax.experimental.pallas import tpu as pltpu

F32 = jnp.float32
BF16 = jnp.bfloat16

GRID_W = 64
ROPE_BASE = 10000.0
EPS = 1e-6
GROUP_W = 256
MLA_HEADS, MLA_NOPE, MLA_ROPE, MLA_V = 4, 64, 32, 64
MLA_Q_RANK, MLA_KV_RANK = 192, 128
GQA_HEADS, GQA_KV_HEADS, GQA_DIM = 4, 2, 64
SSM_GROUPS, SSM_GROUP, SSM_STATE = 16, 16, 64
HY_W, HY_ORDER, HY_EMB, HY_HIDDEN = 256, 2, 33, 64
HY_BANDS = (HY_EMB - 1) // 2
HY_FAST_DECAY, HY_SLOW_DECAY, HY_DECAY_TARGET = 0.3, 1.5, 1e-2

MLA_PACK = 768
GQA_PACK = 768
SSM_PACK = 512
HY_PACK = 1024
IN_PACK = MLA_PACK + GQA_PACK + SSM_PACK + HY_PACK
MLA_DK = 128

LANES = 128
ROW_TILE = 256
ATT_TQ = 256
ATT_TK = 256
SSM_T = 64
SSM_LANE_SPLIT = 512
FFT_R = 128
VMEM_LIMIT = 52 * 1024 * 1024


def _cparams(*sem):
    return pltpu.CompilerParams(dimension_semantics=sem, vmem_limit_bytes=VMEM_LIMIT)


def _silu(x):
    return x * jax.nn.sigmoid(x)


def _mod_kernel(c_ref, w_ref, b_ref, o_ref):
    c = c_ref[...]
    s = _silu(c).astype(BF16)
    o_ref[0] = jnp.dot(s, w_ref[0].astype(BF16), preferred_element_type=F32) + b_ref[0]


def _modulation(cond, w_mod, b_mod):
    depth, d, n3 = w_mod.shape
    r = cond.shape[0]
    tn = 512
    return pl.pallas_call(
        _mod_kernel,
        grid=(depth, n3 // tn),
        in_specs=[pl.BlockSpec((r, d), lambda l, j: (0, 0)),
                  pl.BlockSpec((1, d, tn), lambda l, j: (l, 0, j)),
                  pl.BlockSpec((1, 1, tn), lambda l, j: (l, 0, j))],
        out_specs=pl.BlockSpec((1, r, tn), lambda l, j: (l, 0, j)),
        out_shape=jax.ShapeDtypeStruct((depth, r, n3), F32),
        compiler_params=_cparams("parallel", "parallel"),
        name="modulation",
    )(cond, w_mod, b_mod.reshape(depth, 1, n3))


def _inproj_kernel(x_ref, sc_ref, sh_ref, w_ref, o_mla, o_gqa, o_ssm, o_hy):
    x = x_ref[0]
    ms = jnp.mean(x * x, axis=-1, keepdims=True)
    h = x * lax.rsqrt(ms + EPS) * sc_ref[0] + sh_ref[0]
    hb = h.astype(BF16)
    c0 = 0
    for o_ref, width in ((o_mla, MLA_PACK), (o_gqa, GQA_PACK), (o_ssm, SSM_PACK), (o_hy, HY_PACK)):
        o_ref[0] = jnp.dot(hb, w_ref[:, c0:c0 + width], preferred_element_type=F32).astype(BF16)
        c0 += width


def _inproj(xa, scale, shift, w, nct):
    b, lt, d = xa.shape
    tm = ROW_TILE
    widths = (MLA_PACK, GQA_PACK, SSM_PACK, HY_PACK)

    def mod_idx(bi, i):
        return (jnp.where(i < nct, b, bi), 0, 0)

    return pl.pallas_call(
        _inproj_kernel,
        grid=(b, lt // tm),
        in_specs=[pl.BlockSpec((1, tm, d), lambda bi, i: (bi, i, 0)),
                  pl.BlockSpec((1, 1, d), mod_idx),
                  pl.BlockSpec((1, 1, d), mod_idx),
                  pl.BlockSpec((d, IN_PACK), lambda bi, i: (0, 0))],
        out_specs=[pl.BlockSpec((1, tm, wd), lambda bi, i: (bi, i, 0)) for wd in widths],
        out_shape=[jax.ShapeDtypeStruct((b, lt, wd), BF16) for wd in widths],
        compiler_params=_cparams("parallel", "parallel"),
        name="inproj",
    )(xa, scale, shift, w)


def _outproj_kernel(a_ref, g_ref, s_ref, y_ref, x_ref, gt_ref, gp_ref, w_ref, o_ref):
    acc = jnp.dot(a_ref[0], w_ref[0:GROUP_W], preferred_element_type=F32)
    acc += jnp.dot(g_ref[0], w_ref[GROUP_W:2 * GROUP_W], preferred_element_type=F32)
    acc += jnp.dot(s_ref[0], w_ref[2 * GROUP_W:3 * GROUP_W], preferred_element_type=F32)
    acc += jnp.dot(y_ref[0], w_ref[3 * GROUP_W:4 * GROUP_W], preferred_element_type=F32)
    ms = jnp.mean(acc * acc, axis=-1, keepdims=True)
    o_ref[0] = x_ref[0] + gt_ref[0] * (acc * lax.rsqrt(ms + EPS) * gp_ref[...])


def _outproj(a, g, s, y, xa, gate, g_post, w, nct):
    b, lt, d = xa.shape
    tm = ROW_TILE

    def mod_idx(bi, i):
        return (jnp.where(i < nct, b, bi), 0, 0)

    row = lambda bi, i: (bi, i, 0)
    return pl.pallas_call(
        _outproj_kernel,
        grid=(b, lt // tm),
        in_specs=[pl.BlockSpec((1, tm, GROUP_W), row)] * 4 + [
            pl.BlockSpec((1, tm, d), row),
            pl.BlockSpec((1, 1, d), mod_idx),
            pl.BlockSpec((1, d), lambda bi, i: (0, 0)),
            pl.BlockSpec((4 * GROUP_W, d), lambda bi, i: (0, 0))],
        out_specs=pl.BlockSpec((1, tm, d), row),
        out_shape=jax.ShapeDtypeStruct((b, lt, d), F32),
        compiler_params=_cparams("parallel", "parallel"),
        name="outproj",
    )(a, g, s, y, xa, gate, g_post, w)


def _rope(x, cos, sin, shift):
    w = x.shape[-1]
    lane = lax.broadcasted_iota(jnp.int32, x.shape, 1)
    first = (lane & shift) == 0
    swapped = jnp.where(first, -pltpu.roll(x, w - shift, 1), pltpu.roll(x, shift, 1))
    return x * cos + swapped * sin


def _rope_tables(n_ctx, n_lat):
    t = np.arange(n_lat)
    row = (t // GRID_W).astype(np.float64)
    col = (t % GRID_W).astype(np.float64)

    def block(pos, h):
        inv = ROPE_BASE ** (-np.arange(0, h, 2, dtype=np.float64) / h)
        ang = (pos[:, None].astype(np.float32) * inv[None, :].astype(np.float32)).astype(np.float32)
        c, s = np.cos(ang), np.sin(ang)
        return np.concatenate([c, c], -1), np.concatenate([s, s], -1)

    def full(h, lead, width):
        cr, sr = block(row, h)
        cc, sc = block(col, h)
        cos = np.ones((n_ctx + n_lat, width), np.float32)
        sin = np.zeros((n_ctx + n_lat, width), np.float32)
        cos[n_ctx:, lead:lead + 2 * h] = np.concatenate([cr, cc], -1)
        sin[n_ctx:, lead:lead + 2 * h] = np.concatenate([sr, sc], -1)
        return cos, sin

    mc, ms = full(MLA_ROPE // 2, MLA_NOPE, MLA_DK)
    gc, gs = full(GQA_DIM // 2, 0, GQA_DIM)
    reps = GQA_HEADS
    return (jnp.asarray(mc), jnp.asarray(ms),
            jnp.asarray(np.tile(gc, (1, reps))), jnp.asarray(np.tile(gs, (1, reps))))


def _mla_prep_kernel(p_ref, cos_ref, sin_ref, gq_ref, gkv_ref, wq_ref, wk_ref, wv_ref,
                     q_ref, k_ref, v_ref):
    p = p_ref[0].astype(F32)
    cos, sin = cos_ref[...], sin_ref[...]
    ql = p[:, 0:256]
    rq = lax.rsqrt(jnp.sum(ql * ql, axis=-1, keepdims=True) * (1.0 / MLA_Q_RANK) + EPS)
    qn = (ql * rq * gq_ref[...]).astype(BF16)
    kvl = p[:, 256:384]
    rk = lax.rsqrt(jnp.mean(kvl * kvl, axis=-1, keepdims=True) + EPS)
    kvn = (kvl * rk * gkv_ref[...]).astype(BF16)
    k_rope = _rope(p[:, 384:512], cos, sin, MLA_ROPE // 4)
    scale = (MLA_NOPE + MLA_ROPE) ** -0.5
    for h in range(MLA_HEADS):
        q = jnp.dot(qn, wq_ref[h], preferred_element_type=F32)
        q_ref[0, h] = (_rope(q, cos, sin, MLA_ROPE // 4) * scale).astype(BF16)
        k = jnp.dot(kvn, wk_ref[h], preferred_element_type=F32) + k_rope
        k_ref[0, h] = k.astype(BF16)
        v_ref[0, h] = jnp.dot(kvn, wv_ref[h], preferred_element_type=F32).astype(BF16)


def _mla_prep(p_mla, cos, sin, g_cq, g_ckv, wq, wk, wv):
    b, lt, _ = p_mla.shape
    tm = ROW_TILE
    const2 = lambda bi, i: (0, 0)
    const3 = lambda bi, i: (0, 0, 0)
    hm = lambda bi, i: (bi, 0, i, 0)
    return pl.pallas_call(
        _mla_prep_kernel,
        grid=(b, lt // tm),
        in_specs=[pl.BlockSpec((1, tm, 512), lambda bi, i: (bi, i, 0)),
                  pl.BlockSpec((tm, MLA_DK), lambda bi, i: (i, 0)),
                  pl.BlockSpec((tm, MLA_DK), lambda bi, i: (i, 0)),
                  pl.BlockSpec((1, 256), const2),
                  pl.BlockSpec((1, MLA_KV_RANK), const2),
                  pl.BlockSpec((MLA_HEADS, 256, MLA_DK), const3),
                  pl.BlockSpec((MLA_HEADS, MLA_KV_RANK, MLA_DK), const3),
                  pl.BlockSpec((MLA_HEADS, MLA_KV_RANK, MLA_V), const3)],
        out_specs=[pl.BlockSpec((1, MLA_HEADS, tm, MLA_DK), hm),
                   pl.BlockSpec((1, MLA_HEADS, tm, MLA_DK), hm),
                   pl.BlockSpec((1, MLA_HEADS, tm, MLA_V), hm)],
        out_shape=[jax.ShapeDtypeStruct((b, MLA_HEADS, lt, MLA_DK), BF16),
                   jax.ShapeDtypeStruct((b, MLA_HEADS, lt, MLA_DK), BF16),
                   jax.ShapeDtypeStruct((b, MLA_HEADS, lt, MLA_V), BF16)],
        compiler_params=_cparams("parallel", "parallel"),
        name="mla_prep",
    )(p_mla, cos, sin, g_cq, g_ckv, wq, wk, wv)


def _head_mean_sq(x, ones_bd):
    sq = x * x
    hi = sq.astype(BF16)
    lo = (sq - hi.astype(F32)).astype(BF16)
    s = jnp.dot(hi, ones_bd, preferred_element_type=F32) + jnp.dot(lo, ones_bd, preferred_element_type=F32)
    return s * (1.0 / GQA_DIM)


def _gqa_prep_kernel(p_ref, cos_ref, sin_ref, gq_ref, gk_ref, ones_ref, q_ref, k_ref, v_ref):
    p = p_ref[0].astype(F32)
    cos, sin = cos_ref[...], sin_ref[...]
    nq = GQA_HEADS * GQA_DIM
    nk = GQA_KV_HEADS * GQA_DIM
    q = p[:, 0:nq]
    qn = q * lax.rsqrt(_head_mean_sq(q, ones_ref[...]) + EPS) * gq_ref[...]
    qr = _rope(qn, cos, sin, GQA_DIM // 4) * (GQA_DIM ** -0.5)
    k = p[:, nq:nq + nk]
    kn = k * lax.rsqrt(_head_mean_sq(k, ones_ref[0:nk, 0:nk]) + EPS) * gk_ref[...]
    kr = _rope(kn, cos[:, 0:nk], sin[:, 0:nk], GQA_DIM // 4)
    v = p[:, nq + nk:nq + 2 * nk]
    for h in range(GQA_HEADS):
        q_ref[0, h] = qr[:, h * GQA_DIM:(h + 1) * GQA_DIM].astype(BF16)
    for h in range(GQA_KV_HEADS):
        k_ref[0, h] = kr[:, h * GQA_DIM:(h + 1) * GQA_DIM].astype(BF16)
        v_ref[0, h] = v[:, h * GQA_DIM:(h + 1) * GQA_DIM].astype(BF16)


def _gqa_prep(p_gqa, cos, sin, g_q, g_k, ones_bd):
    b, lt, _ = p_gqa.shape
    tm = ROW_TILE
    nq = GQA_HEADS * GQA_DIM
    nk = GQA_KV_HEADS * GQA_DIM
    const2 = lambda bi, i: (0, 0)
    hm = lambda bi, i: (bi, 0, i, 0)
    return pl.pallas_call(
        _gqa_prep_kernel,
        grid=(b, lt // tm),
        in_specs=[pl.BlockSpec((1, tm, 512), lambda bi, i: (bi, i, 0)),
                  pl.BlockSpec((tm, nq), lambda bi, i: (i, 0)),
                  pl.BlockSpec((tm, nq), lambda bi, i: (i, 0)),
                  pl.BlockSpec((1, nq), const2),
                  pl.BlockSpec((1, nk), const2),
                  pl.BlockSpec((nq, nq), const2)],
        out_specs=[pl.BlockSpec((1, GQA_HEADS, tm, GQA_DIM), hm),
                   pl.BlockSpec((1, GQA_KV_HEADS, tm, GQA_DIM), hm),
                   pl.BlockSpec((1, GQA_KV_HEADS, tm, GQA_DIM), hm)],
        out_shape=[jax.ShapeDtypeStruct((b, GQA_HEADS, lt, GQA_DIM), BF16),
                   jax.ShapeDtypeStruct((b, GQA_KV_HEADS, lt, GQA_DIM), BF16),
                   jax.ShapeDtypeStruct((b, GQA_KV_HEADS, lt, GQA_DIM), BF16)],
        compiler_params=_cparams("parallel", "parallel"),
        name="gqa_prep",
    )(p_gqa, cos, sin, g_q, g_k, ones_bd)


def _attn_kernel(q_ref, k_ref, v_ref, gate_ref, place_ref, o_ref, *, heads, group, nct_q,
                 ctx_chunks, all_chunks):
    i = pl.program_id(1)
    n_chunks = jnp.where(i < nct_q, ctx_chunks, all_chunks)
    tq = q_ref.shape[2]
    dv = v_ref.shape[3]
    out = jnp.zeros((tq, GROUP_W), F32)
    for h in range(heads):
        hk = h // group
        q = q_ref[0, h]

        def body(j, carry, hk=hk, q=q):
            m, l, acc = carry
            rows = pl.ds(pl.multiple_of(j * ATT_TK, ATT_TK), ATT_TK)
            ks = k_ref[0, hk, rows, :]
            vs = v_ref[0, hk, rows, :]
            s = lax.dot_general(q, ks, (((1,), (1,)), ((), ())), preferred_element_type=F32)
            m_new = jnp.maximum(m, jnp.max(s, axis=1, keepdims=True))
            alpha = jnp.exp(m - m_new)
            p = jnp.exp(s - m_new)
            l = alpha * l + jnp.sum(p, axis=1, keepdims=True)
            acc = alpha * acc + jnp.dot(p.astype(BF16), vs, preferred_element_type=F32)
            return m_new, l, acc

        init = (jnp.full((tq, 1), -jnp.inf, F32), jnp.zeros((tq, 1), F32), jnp.zeros((tq, dv), F32))
        _, l, acc = lax.fori_loop(0, n_chunks, body, init)
        o = (acc / l).astype(BF16)
        out = out + jnp.dot(o, place_ref[h], preferred_element_type=F32)
    g = gate_ref[0].astype(F32)
    o_ref[0] = (out * _silu(g)).astype(BF16)


def _attention(q, k, v, gate_src, gate_block, n_ctx):
    b, heads, lt, dk = q.shape
    hk, dv = k.shape[1], v.shape[3]
    place = np.zeros((heads, dv, GROUP_W), np.float32)
    for h in range(heads):
        place[h, np.arange(dv), h * dv + np.arange(dv)] = 1.0
    kern = functools.partial(_attn_kernel, heads=heads, group=heads // hk, nct_q=n_ctx // ATT_TQ,
                             ctx_chunks=n_ctx // ATT_TK, all_chunks=lt // ATT_TK)
    return pl.pallas_call(
        kern,
        grid=(b, lt // ATT_TQ),
        in_specs=[pl.BlockSpec((1, heads, ATT_TQ, dk), lambda bi, i: (bi, 0, i, 0)),
                  pl.BlockSpec((1, hk, lt, dk), lambda bi, i: (bi, 0, 0, 0)),
                  pl.BlockSpec((1, hk, lt, dv), lambda bi, i: (bi, 0, 0, 0)),
                  pl.BlockSpec((1, ATT_TQ, GROUP_W), lambda bi, i: (bi, i, gate_block)),
                  pl.BlockSpec((heads, dv, GROUP_W), lambda bi, i: (0, 0, 0))],
        out_specs=pl.BlockSpec((1, ATT_TQ, GROUP_W), lambda bi, i: (bi, i, 0)),
        out_shape=jax.ShapeDtypeStruct((b, lt, GROUP_W), BF16),
        compiler_params=_cparams("parallel", "arbitrary"),
        name="attention",
    )(q, k, v, gate_src, jnp.asarray(place, BF16))


def _ssm_kernel(*refs, steps, nb, reverse, finish):
    if finish:
        (u_ref, bre_ref, bim_ref, are_ref, aim_ref, cre_ref, cim_ref, yf_ref, d_ref, gw_ref, gb_ref,
         o_ref, sre, sim, car_re, car_im) = refs
    else:
        (u_ref, bre_ref, bim_ref, are_ref, aim_ref, cre_ref, cim_ref,
         o_ref, sre, sim, car_re, car_im) = refs

    @pl.when(pl.program_id(0) == 0)
    def _():
        car_re[...] = jnp.zeros_like(car_re)
        car_im[...] = jnp.zeros_like(car_im)

    u = u_ref[:, 0:GROUP_W]
    sre[...] = jnp.dot(u, bre_ref[...], preferred_element_type=F32)
    sim[...] = jnp.dot(u, bim_ref[...], preferred_element_type=F32)
    n_state = sre.shape[1]
    lw = min(SSM_LANE_SPLIT, n_state)
    for c0 in range(0, n_state, lw):
        ls = slice(c0, c0 + lw)
        ar = jnp.broadcast_to(are_ref[:, ls], (nb, lw))
        ai = jnp.broadcast_to(aim_ref[:, ls], (nb, lw))

        def body(tt, carry, ls=ls, ar=ar, ai=ai):
            sr, si = carry
            t = (steps - 1 - tt) if reverse else tt
            rows = pl.ds(pl.multiple_of(t * nb, nb), nb)
            nr = ar * sr - ai * si + sre[rows, ls]
            ni = ar * si + ai * sr + sim[rows, ls]
            sre[rows, ls] = nr
            sim[rows, ls] = ni
            return nr, ni

        sr, si = lax.fori_loop(0, steps, body, (car_re[:, ls], car_im[:, ls]), unroll=2)
        car_re[:, ls] = sr
        car_im[:, ls] = si
    y = jnp.dot(sre[...].astype(BF16), cre_ref[...], preferred_element_type=F32)
    y += jnp.dot(sim[...].astype(BF16), cim_ref[...], preferred_element_type=F32)
    if not finish:
        o_ref[...] = y
    else:
        uf = u.astype(F32)
        y = y + yf_ref[...] + d_ref[...] * uf
        z = jax.nn.gelu(y, approximate=True)
        gl = jnp.dot(z.astype(BF16), gw_ref[...], preferred_element_type=F32) + gb_ref[...]
        z = z * jax.nn.sigmoid(gl)
        gate = u_ref[:, GROUP_W:2 * GROUP_W].astype(F32)
        o_ref[...] = (z * _silu(gate)).astype(BF16)


def _ssm_direction(u_tm, mats, nb, n_ctx_chunks, reverse, finish_args=None):
    rows, _ = u_tm.shape
    steps = SSM_T
    blk = steps * nb
    n_chunks = rows // blk
    n_state = mats[0].shape[1]
    if reverse:
        cidx = lambda i: (jnp.where(i < n_ctx_chunks, n_ctx_chunks - 1 - i, n_chunks - 1 - (i - n_ctx_chunks)), 0)
    else:
        cidx = lambda i: (i, 0)
    const = lambda i: (0, 0)
    in_specs = [pl.BlockSpec((blk, SSM_PACK), cidx)] + [pl.BlockSpec(m.shape, const) for m in mats]
    args = [u_tm, *mats]
    finish = finish_args is not None
    if finish:
        yf, d_skip, glu_w, glu_b = finish_args
        in_specs += [pl.BlockSpec((blk, GROUP_W), cidx), pl.BlockSpec(d_skip.shape, const),
                     pl.BlockSpec(glu_w.shape, const), pl.BlockSpec(glu_b.shape, const)]
        args += [yf, d_skip, glu_w, glu_b]
    return pl.pallas_call(
        functools.partial(_ssm_kernel, steps=steps, nb=nb, reverse=reverse, finish=finish),
        grid=(n_chunks,),
        in_specs=in_specs,
        out_specs=pl.BlockSpec((blk, GROUP_W), cidx),
        out_shape=jax.ShapeDtypeStruct((rows, GROUP_W), BF16 if finish else F32),
        scratch_shapes=[pltpu.VMEM((blk, n_state), F32), pltpu.VMEM((blk, n_state), F32),
                        pltpu.VMEM((nb, n_state), F32), pltpu.VMEM((nb, n_state), F32)],
        compiler_params=_cparams("arbitrary"),
        name="ssm_rev" if reverse else "ssm_fwd",
    )(*args)


def _ssm_matrices(lam_re, lam_im, log_step, b_re, b_im, c_re, c_im):
    lam = lax.complex(lam_re.astype(F32), lam_im.astype(F32))
    step = jnp.exp(log_step.astype(F32))[:, None]
    a_bar = jnp.exp(lam * step)
    b_bar = ((a_bar - 1.0) / lam)[..., None] * lax.complex(b_re.astype(F32), b_im.astype(F32))
    eye = jnp.eye(SSM_GROUPS, dtype=F32)
    n_in = SSM_GROUPS * SSM_GROUP
    n_state = SSM_GROUPS * SSM_STATE

    def drive(m):
        return jnp.einsum('gph,gk->ghkp', m, eye).reshape(n_in, n_state).astype(BF16)

    def readout(m):
        return jnp.einsum('ghp,gk->gpkh', m, eye).reshape(n_state, n_in).astype(BF16)

    return (drive(jnp.real(b_bar)), drive(jnp.imag(b_bar)),
            jnp.real(a_bar).reshape(1, n_state), jnp.imag(a_bar).reshape(1, n_state),
            readout(c_re.astype(F32)), readout(-c_im.astype(F32)))


def _hy_pre_kernel(x_ref, prev_ref, next_ref, w_ref, b_ref, v_ref, x1_ref, x2_ref, sg_ref, *, n_tiles):
    i = pl.program_id(1)
    nconv = 3 * HY_W
    x = x_ref[0, :, 0:nconv].astype(F32)
    tm = x.shape[0]
    prev_row = jnp.where(i > 0, prev_ref[0, 7:8, 0:nconv].astype(F32), 0.0)
    next_row = jnp.where(i < n_tiles - 1, next_ref[0, 0:1, 0:nconv].astype(F32), 0.0)
    rid = lax.broadcasted_iota(jnp.int32, x.shape, 0)
    xm = jnp.where(rid == 0, prev_row, pltpu.roll(x, 1, 0))
    xp = jnp.where(rid == tm - 1, next_row, pltpu.roll(x, tm - 1, 0))
    proj = xm * w_ref[0:1] + x * w_ref[1:2] + xp * w_ref[2:3] + b_ref[...]
    gate = x_ref[0, :, nconv:nconv + HY_W].astype(F32)
    for o_ref, val in ((v_ref, proj[:, 0:HY_W]), (x1_ref, proj[:, HY_W:2 * HY_W]),
                       (x2_ref, proj[:, 2 * HY_W:3 * HY_W]), (sg_ref, _silu(gate))):
        for hf in range(HY_W // LANES):
            o_ref[0, hf] = val[:, hf * LANES:(hf + 1) * LANES].astype(BF16)


def _hy_pre(p_hy, conv_w, conv_b, tile0, n_tiles):
    b = p_hy.shape[0]
    tm = ROW_TILE
    sub = tm // 8
    n = n_tiles * tm
    last_sub = p_hy.shape[1] // 8 - 1
    halves = HY_W // LANES
    const2 = lambda bi, i: (0, 0)
    out_spec = pl.BlockSpec((1, halves, tm, LANES), lambda bi, i: (bi, 0, i, 0))
    out_shape = jax.ShapeDtypeStruct((b, halves, n, LANES), BF16)
    return pl.pallas_call(
        functools.partial(_hy_pre_kernel, n_tiles=n_tiles),
        grid=(b, n_tiles),
        in_specs=[pl.BlockSpec((1, tm, HY_PACK), lambda bi, i: (bi, i + tile0, 0)),
                  pl.BlockSpec((1, 8, HY_PACK), lambda bi, i: (bi, jnp.maximum((i + tile0) * sub - 1, 0), 0)),
                  pl.BlockSpec((1, 8, HY_PACK), lambda bi, i: (bi, jnp.minimum((i + tile0 + 1) * sub, last_sub), 0)),
                  pl.BlockSpec((3, 3 * HY_W), const2),
                  pl.BlockSpec((1, 3 * HY_W), const2)],
        out_specs=[out_spec] * 4,
        out_shape=[out_shape] * 4,
        compiler_params=_cparams("parallel", "parallel"),
        name="hyena_pre",
    )(p_hy, p_hy, p_hy, conv_w, conv_b)


def _bitrev(p, bits):
    r = 0
    for i in range(bits):
        r = (r << 1) | ((p >> i) & 1)
    return r


@functools.lru_cache(maxsize=None)
def _fft_plan(n):
    r = FFT_R
    nb = n // r
    m = 2 * nb
    big = 2 * n
    bits = int(round(math.log2(nb)))
    assert nb >= 2 and (1 << bits) == nb
    jmap = np.zeros(m, np.int64)
    for p in range(nb):
        jmap[p] = 2 * _bitrev(p, bits)
        jmap[nb + p] = 2 * _bitrev(p, bits) + 1
    k2 = np.arange(r)
    n2 = np.arange(r)
    g = np.zeros((m, 2 * r, 2 * r), np.float32)
    for s in range(m):
        ang = -2.0 * np.pi * (np.outer(k2, n2) / r + np.outer(np.ones(r), n2) * jmap[s] / big)
        gr, gi = np.cos(ang), np.sin(ang)
        g[s] = np.block([[gr, -gi], [gi, gr]])
    half0 = nb // 2
    k = np.arange(half0)
    wnb = np.exp(-2j * np.pi * k / nb)
    wm0 = np.exp(-2j * np.pi * k / m)
    wm1 = np.exp(-2j * np.pi * (k + half0) / m)
    tw_edge = np.stack([wnb.real, wnb.imag, wm0.real, wm0.imag, wm1.real, wm1.imag]).astype(np.float32)
    mids = []
    h = half0 // 2
    while h >= 1:
        kk = np.arange(nb // 2) % h
        w = np.exp(-2j * np.pi * kk / (2 * h))
        mids.append(np.stack([w.real, w.imag]))
        h //= 2
    tw_mid = (np.concatenate(mids, 0) if mids else np.zeros((2, max(nb // 2, 1)))).astype(np.float32)
    freq = (jmap[:, None] + m * k2[None, :])
    return dict(nb=nb, m=m, g=g, tw_edge=tw_edge, tw_mid=tw_mid, n_mid=len(mids), freq=freq)


def _cmul(ar, ai, wr, wi):
    return ar * wr - ai * wi, ar * wi + ai * wr


def _hy_conv_kernel(*refs, nb, n_mid, n_post):
    tw_edge, tw_mid, a_ref, bias_ref, kh_ref, g_ref = refs[:6]
    post_refs = refs[6:6 + n_post]
    o_ref, w = refs[6 + n_post], refs[7 + n_post]
    r = FFT_R
    m = 2 * nb
    half0 = nb // 2
    re, im = slice(0, r), slice(r, 2 * r)

    def rows(k):
        return pl.ds(pl.multiple_of(k * r, r), r)

    def first(k, c):
        wr, wi = tw_edge[0, k], tw_edge[1, k]
        ar, ai = a_ref[0, 0, rows(k), :].astype(F32), a_ref[1, 0, rows(k), :].astype(F32)
        br, bi = a_ref[0, 0, rows(k + half0), :].astype(F32), a_ref[1, 0, rows(k + half0), :].astype(F32)
        w[k, re], w[k, im] = ar + br, ai + bi
        dr, di = _cmul(ar - br, ai - bi, wr, wi)
        w[k + half0, re], w[k + half0, im] = dr, di
        a2r, a2i = _cmul(ar, ai, tw_edge[2, k], tw_edge[3, k])
        b2r, b2i = _cmul(br, bi, tw_edge[4, k], tw_edge[5, k])
        w[nb + k, re], w[nb + k, im] = a2r + b2r, a2i + b2i
        dr, di = _cmul(a2r - b2r, a2i - b2i, wr, wi)
        w[nb + k + half0, re], w[nb + k + half0, im] = dr, di
        return c

    lax.fori_loop(0, half0, first, 0)

    def slot_pair(bb, lg):
        branch = bb // half0
        bf = bb - branch * half0
        i0 = branch * nb + ((bf >> lg) << (lg + 1)) + (bf & ((1 << lg) - 1))
        return bf, i0, i0 + (1 << lg)

    for s in range(n_mid):
        lg = int(round(math.log2(half0))) - 1 - s

        def dif(bb, c, s=s, lg=lg):
            bf, i0, i1 = slot_pair(bb, lg)
            wr, wi = tw_mid[2 * s, bf], tw_mid[2 * s + 1, bf]
            ar, ai, br, bi = w[i0, re], w[i0, im], w[i1, re], w[i1, im]
            w[i0, re], w[i0, im] = ar + br, ai + bi
            dr, di = _cmul(ar - br, ai - bi, wr, wi)
            w[i1, re], w[i1, im] = dr, di
            return c

        lax.fori_loop(0, nb, dif, 0)

    def spectral(s, c):
        gm = g_ref[s]
        x = jnp.dot(gm, w[s].astype(BF16), preferred_element_type=F32)
        yr, yi = _cmul(x[re], x[im], kh_ref[0, s, re, :], kh_ref[0, s, im, :])
        y = jnp.concatenate([yr, yi], axis=0).astype(BF16)
        w[s] = lax.dot_general(gm, y, (((0,), (0,)), ((), ())), preferred_element_type=F32)
        return c

    lax.fori_loop(0, m, spectral, 0)

    for s in reversed(range(n_mid)):
        lg = int(round(math.log2(half0))) - 1 - s

        def dit(bb, c, s=s, lg=lg):
            bf, i0, i1 = slot_pair(bb, lg)
            wr, wi = tw_mid[2 * s, bf], -tw_mid[2 * s + 1, bf]
            ar, ai = w[i0, re], w[i0, im]
            br, bi = _cmul(w[i1, re], w[i1, im], wr, wi)
            w[i0, re], w[i0, im] = ar + br, ai + bi
            w[i1, re], w[i1, im] = ar - br, ai - bi
            return c

        lax.fori_loop(0, nb, dit, 0)

    bias = bias_ref[0]

    def emit(k, yr, yi):
        for bsel, y in ((0, yr), (1, yi)):
            a = a_ref[bsel, 0, rows(k), :].astype(F32)
            val = y + bias * a
            for p_ref in post_refs:
                val = val * p_ref[bsel, 0, rows(k), :].astype(F32)
            o_ref[bsel, 0, rows(k), :] = val.astype(o_ref.dtype)

    def last(k, c):
        wr, wi = tw_edge[0, k], -tw_edge[1, k]
        ar, ai = w[k, re], w[k, im]
        br, bi = _cmul(w[k + half0, re], w[k + half0, im], wr, wi)
        cr, ci = w[nb + k, re], w[nb + k, im]
        dr, di = _cmul(w[nb + k + half0, re], w[nb + k + half0, im], wr, wi)
        o0r, o0i = _cmul(cr + dr, ci + di, tw_edge[2, k], -tw_edge[3, k])
        o1r, o1i = _cmul(cr - dr, ci - di, tw_edge[4, k], -tw_edge[5, k])
        emit(k, ar + br + o0r, ai + bi + o0i)
        emit(k + half0, ar - br + o1r, ai - bi + o1i)
        return c

    lax.fori_loop(0, half0, last, 0)


def _hy_conv(a, khat, bias, posts, out_dtype):
    b, halves, n, lanes = a.shape
    plan = _fft_plan(n)
    nb, m = plan['nb'], plan['m']
    r = FFT_R
    sig = pl.BlockSpec((2, 1, n, lanes), lambda hf, pr: (pr, hf, 0, 0))
    smem = pl.BlockSpec(memory_space=pltpu.SMEM)
    once = pl.Buffered(1)
    in_specs = [smem, smem, sig,
                pl.BlockSpec((1, 1, lanes), lambda hf, pr: (hf, 0, 0)),
                pl.BlockSpec((1, m, 2 * r, lanes), lambda hf, pr: (hf, 0, 0, 0), pipeline_mode=once),
                pl.BlockSpec((m, 2 * r, 2 * r), lambda hf, pr: (0, 0, 0), pipeline_mode=once)]
    in_specs += [sig] * len(posts)
    return pl.pallas_call(
        functools.partial(_hy_conv_kernel, nb=nb, n_mid=plan['n_mid'], n_post=len(posts)),
        grid=(halves, b // 2),
        in_specs=in_specs,
        out_specs=sig,
        out_shape=jax.ShapeDtypeStruct(a.shape, out_dtype),
        scratch_shapes=[pltpu.VMEM((m, 2 * r, lanes), F32)],
        compiler_params=_cparams("arbitrary", "arbitrary"),
        name="hyena_conv",
    )(jnp.asarray(plan['tw_edge']), jnp.asarray(plan['tw_mid']), a, bias, khat,
      jnp.asarray(plan['g'], BF16), *posts)


def _hyena_filter_spectrum(n, w1, b1, fr1, w2, b2, fr2, w3):
    t = jnp.linspace(0.0, 1.0, n, dtype=F32)[:, None]
    omega = 2.0 * math.pi * jnp.arange(n, dtype=F32)[:, None] / n
    bands = jnp.linspace(1e-4, HY_BANDS - 1, HY_BANDS, dtype=F32)[None, :]
    z = jnp.concatenate([t, jnp.cos(bands * omega), -jnp.sin(bands * omega)], axis=-1)
    hp = lax.Precision.HIGHEST
    h = jnp.sin(fr1.astype(F32) * (jnp.dot(z, w1.astype(F32), precision=hp) + b1.astype(F32)))
    h = jnp.sin(fr2.astype(F32) * (jnp.dot(h, w2.astype(F32), precision=hp) + b2.astype(F32)))
    h = jnp.dot(h, w3.astype(F32), precision=hp).reshape(n, HY_ORDER, 2, HY_W)
    max_decay = math.log(HY_DECAY_TARGET) / HY_FAST_DECAY
    min_decay = math.log(HY_DECAY_TARGET) / HY_SLOW_DECAY
    deltas = jnp.linspace(min_decay, max_decay, HY_W, dtype=F32)
    h = h * jnp.exp(-t[:, :, None, None] * jnp.abs(deltas))
    h = h * lax.rsqrt(jnp.sum(h * h, axis=(0, 2), keepdims=True) + EPS)
    fwd, bwd = h[:, :, 0], h[:, :, 1]
    zero = jnp.zeros((1, HY_ORDER, HY_W), F32)
    k_circ = jnp.concatenate([fwd, zero, bwd[1:][::-1]], axis=0)
    spec = jnp.fft.fft(k_circ, axis=0) * (1.0 / (2 * n))
    plan = _fft_plan(n)
    sel = spec[jnp.asarray(plan['freq'])]
    tab = jnp.concatenate([jnp.real(sel), jnp.imag(sel)], axis=1)
    tab = tab.reshape(plan['m'], 2 * FFT_R, HY_ORDER, HY_W // LANES, LANES)
    return jnp.transpose(tab, (2, 3, 0, 1, 4)).astype(F32)


def _pack_w_in(w_in):
    d = w_in.shape[0]
    z = lambda n: jnp.zeros((d, n), w_in.dtype)
    o = 0
    q_lat = w_in[:, o:o + MLA_Q_RANK]; o += MLA_Q_RANK
    kv_lat = w_in[:, o:o + MLA_KV_RANK]; o += MLA_KV_RANK
    k_rope = w_in[:, o:o + MLA_ROPE]; o += MLA_ROPE
    gate = w_in[:, o:o + GROUP_W]; o += GROUP_W
    mla = jnp.concatenate([q_lat, z(256 - MLA_Q_RANK), kv_lat, z(MLA_NOPE), k_rope,
                           z(MLA_DK - MLA_NOPE - MLA_ROPE), gate], axis=1)
    return jnp.concatenate([mla, w_in[:, o:]], axis=1).astype(BF16)


def _pack_mla_up(w_uq, w_ukv):
    dq = MLA_NOPE + MLA_ROPE
    wq = w_uq.reshape(MLA_Q_RANK, MLA_HEADS, dq).transpose(1, 0, 2)
    wq = jnp.pad(wq, ((0, 0), (0, 256 - MLA_Q_RANK), (0, MLA_DK - dq))).astype(BF16)
    wkv = w_ukv.reshape(MLA_KV_RANK, MLA_HEADS, MLA_NOPE + MLA_V).transpose(1, 0, 2)
    wk = jnp.pad(wkv[:, :, :MLA_NOPE], ((0, 0), (0, 0), (0, MLA_DK - MLA_NOPE))).astype(BF16)
    wv = wkv[:, :, MLA_NOPE:].astype(BF16)
    return wq, wk, wv


def _merge_halves(y_ctx, y_lat):
    y = jnp.concatenate([y_ctx, y_lat], axis=2)
    b, halves, lt, lanes = y.shape
    return jnp.transpose(y, (0, 2, 1, 3)).reshape(b, lt, halves * lanes)


def kernel(x, c, ctx, c_ctx, w_mod, b_mod, g_pre, g_post, w_in, w_out, mla_g_cq, mla_w_uq, mla_g_ckv, mla_w_ukv, gqa_g_q, gqa_g_k, ssm_lambda_re, ssm_lambda_im, ssm_log_step, ssm_b_re, ssm_b_im, ssm_c_re, ssm_c_im, ssm_d, ssm_glu_w, ssm_glu_b, hy_conv_w, hy_conv_b, hy_f_w1, hy_f_b1, hy_f_freq1, hy_f_w2, hy_f_b2, hy_f_freq2, hy_f_w3, hy_bias):
    b, n_lat, d = x.shape
    n_ctx = ctx.shape[1]
    depth = w_in.shape[0]
    lt = n_ctx + n_lat
    nct = n_ctx // ROW_TILE
    assert n_ctx % ROW_TILE == 0 and n_lat % ROW_TILE == 0 and b % 2 == 0 and b % 8 == 0

    n_cond = -(-(b + 1) // 8) * 8
    cond = jnp.zeros((n_cond, d), F32).at[:b].set(c).at[b].set(c_ctx)
    mod = _modulation(cond, w_mod, b_mod)[:, :b + 1]
    shift, scale, gate = mod[..., :d], mod[..., d:2 * d], mod[..., 2 * d:]

    mla_cos, mla_sin, gqa_cos, gqa_sin = _rope_tables(n_ctx, n_lat)
    nq = GQA_HEADS * GQA_DIM
    ones_bd = jnp.asarray(np.kron(np.eye(GQA_HEADS), np.ones((GQA_DIM, GQA_DIM))), BF16)

    xa = jnp.concatenate([ctx, x], axis=1)
    for l in range(depth):
        sc = (g_pre[l][None, :] * (1.0 + scale[l]))[:, None, :]
        sh = shift[l][:, None, :]
        p_mla, p_gqa, p_ssm, p_hy = _inproj(xa, sc, sh, _pack_w_in(w_in[l]), nct)

        wq, wk, wv = _pack_mla_up(mla_w_uq[l], mla_w_ukv[l])
        g_cq = jnp.pad(mla_g_cq[l], (0, 256 - MLA_Q_RANK)).reshape(1, 256)
        q, k, v = _mla_prep(p_mla, mla_cos, mla_sin, g_cq, mla_g_ckv[l].reshape(1, -1), wq, wk, wv)
        a_out = _attention(q, k, v, p_mla, 2, n_ctx)

        q, k, v = _gqa_prep(p_gqa, gqa_cos, gqa_sin, jnp.tile(gqa_g_q[l], GQA_HEADS).reshape(1, nq),
                            jnp.tile(gqa_g_k[l], GQA_KV_HEADS).reshape(1, -1), ones_bd)
        g_out = _attention(q, k, v, p_gqa, 2, n_ctx)

        u_tm = jnp.transpose(p_ssm, (1, 0, 2)).reshape(lt * b, SSM_PACK)
        mats = [_ssm_matrices(ssm_lambda_re[l, di], ssm_lambda_im[l, di], ssm_log_step[l, di],
                              ssm_b_re[l, di], ssm_b_im[l, di], ssm_c_re[l, di], ssm_c_im[l, di])
                for di in range(2)]
        n_ctx_chunks = n_ctx // SSM_T
        y_f = _ssm_direction(u_tm, mats[0], b, n_ctx_chunks, False)
        fin = (y_f, ssm_d[l].reshape(1, -1), ssm_glu_w[l].astype(BF16), ssm_glu_b[l].reshape(1, -1))
        s_tm = _ssm_direction(u_tm, mats[1], b, n_ctx_chunks, True, fin)
        s_out = jnp.transpose(s_tm.reshape(lt, b, GROUP_W), (1, 0, 2))

        filt = (hy_f_w1[l], hy_f_b1[l], hy_f_freq1[l], hy_f_w2[l], hy_f_b2[l], hy_f_freq2[l], hy_f_w3[l])
        conv_b = hy_conv_b[l].reshape(1, -1)
        parts = []
        for tile0, n_tiles in ((0, nct), (nct, n_lat // ROW_TILE)):
            n = n_tiles * ROW_TILE
            hv, hx1, hx2, hsg = _hy_pre(p_hy, hy_conv_w[l], conv_b, tile0, n_tiles)
            khat = _hyena_filter_spectrum(n, *filt)
            bias = hy_bias[l].astype(F32).reshape(HY_ORDER, HY_W // LANES, 1, LANES)
            z1 = _hy_conv(hv, khat[0], bias[0], (hx1,), BF16)
            parts.append(_hy_conv(z1, khat[1], bias[1], (hx2, hsg), BF16))
        y_out = _merge_halves(*parts)

        xa = _outproj(a_out, g_out, s_out, y_out, xa, gate[l][:, None, :], g_post[l].reshape(1, d),
                      w_out[l].astype(BF16), nct)
    return xa[:, n_ctx:]
```

```python
import functools
import math

import numpy as np
import jax
import jax.numpy as jnp
from jax import lax
from jax.experimental import pallas as pl
from jax.experimental.pallas import tpu as pltpu

F32 = jnp.float32
BF16 = jnp.bfloat16

GRID_W = 64
ROPE_BASE = 10000.0
EPS = 1e-6
GROUP_W = 256
MLA_HEADS, MLA_NOPE, MLA_ROPE, MLA_V = 4, 64, 32, 64
MLA_Q_RANK, MLA_KV_RANK = 192, 128
GQA_HEADS, GQA_KV_HEADS, GQA_DIM = 4, 2, 64
SSM_GROUPS, SSM_GROUP, SSM_STATE = 16, 16, 64
HY_W, HY_ORDER, HY_EMB, HY_HIDDEN = 256, 2, 33, 64
HY_BANDS = (HY_EMB - 1) // 2
HY_FAST_DECAY, HY_SLOW_DECAY, HY_DECAY_TARGET = 0.3, 1.5, 1e-2

MLA_PACK = 768
GQA_PACK = 768
SSM_PACK = 512
HY_PACK = 1024
IN_PACK = MLA_PACK + GQA_PACK + SSM_PACK + HY_PACK
MLA_DK = 128

LANES = 128
ROW_TILE = 256
ATT_TQ = 256
ATT_TK = 256
SSM_T = 64
SSM_LANE_SPLIT = 512
FFT_R = 128
VMEM_LIMIT = 52 * 1024 * 1024


def _cparams(*sem):
    return pltpu.CompilerParams(dimension_semantics=sem, vmem_limit_bytes=VMEM_LIMIT)


def _silu(x):
    return x * jax.nn.sigmoid(x)


def _mod_kernel(c_ref, w_ref, b_ref, o_ref):
    c = c_ref[...]
    s = _silu(c).astype(BF16)
    o_ref[0] = jnp.dot(s, w_ref[0].astype(BF16), preferred_element_type=F32) + b_ref[0]


def _modulation(cond, w_mod, b_mod):
    depth, d, n3 = w_mod.shape
    r = cond.shape[0]
    tn = 512
    return pl.pallas_call(
        _mod_kernel,
        grid=(depth, n3 // tn),
        in_specs=[pl.BlockSpec((r, d), lambda l, j: (0, 0)),
                  pl.BlockSpec((1, d, tn), lambda l, j: (l, 0, j)),
                  pl.BlockSpec((1, 1, tn), lambda l, j: (l, 0, j))],
        out_specs=pl.BlockSpec((1, r, tn), lambda l, j: (l, 0, j)),
        out_shape=jax.ShapeDtypeStruct((depth, r, n3), F32),
        compiler_params=_cparams("parallel", "parallel"),
        name="modulation",
    )(cond, w_mod, b_mod.reshape(depth, 1, n3))


def _inproj_kernel(x_ref, sc_ref, sh_ref, w_ref, o_mla, o_gqa, o_ssm, o_hy):
    x = x_ref[0]
    ms = jnp.mean(x * x, axis=-1, keepdims=True)
    h = x * lax.rsqrt(ms + EPS) * sc_ref[0] + sh_ref[0]
    hb = h.astype(BF16)
    c0 = 0
    for o_ref, width in ((o_mla, MLA_PACK), (o_gqa, GQA_PACK), (o_ssm, SSM_PACK), (o_hy, HY_PACK)):
        o_ref[0] = jnp.dot(hb, w_ref[:, c0:c0 + width], preferred_element_type=F32).astype(BF16)
        c0 += width


def _inproj(xa, scale, shift, w, nct):
    b, lt, d = xa.shape
    tm = ROW_TILE
    widths = (MLA_PACK, GQA_PACK, SSM_PACK, HY_PACK)

    def mod_idx(bi, i):
        return (jnp.where(i < nct, b, bi), 0, 0)

    return pl.pallas_call(
        _inproj_kernel,
        grid=(b, lt // tm),
        in_specs=[pl.BlockSpec((1, tm, d), lambda bi, i: (bi, i, 0)),
                  pl.BlockSpec((1, 1, d), mod_idx),
                  pl.BlockSpec((1, 1, d), mod_idx),
                  pl.BlockSpec((d, IN_PACK), lambda bi, i: (0, 0))],
        out_specs=[pl.BlockSpec((1, tm, wd), lambda bi, i: (bi, i, 0)) for wd in widths],
        out_shape=[jax.ShapeDtypeStruct((b, lt, wd), BF16) for wd in widths],
        compiler_params=_cparams("parallel", "parallel"),
        name="inproj",
    )(xa, scale, shift, w)


def _outproj_kernel(a_ref, g_ref, s_ref, y_ref, x_ref, gt_ref, gp_ref, w_ref, o_ref):
    acc = jnp.dot(a_ref[0], w_ref[0:GROUP_W], preferred_element_type=F32)
    acc += jnp.dot(g_ref[0], w_ref[GROUP_W:2 * GROUP_W], preferred_element_type=F32)
    acc += jnp.dot(s_ref[0], w_ref[2 * GROUP_W:3 * GROUP_W], preferred_element_type=F32)
    acc += jnp.dot(y_ref[0], w_ref[3 * GROUP_W:4 * GROUP_W], preferred_element_type=F32)
    ms = jnp.mean(acc * acc, axis=-1, keepdims=True)
    o_ref[0] = x_ref[0] + gt_ref[0] * (acc * lax.rsqrt(ms + EPS) * gp_ref[...])


def _outproj(a, g, s, y, xa, gate, g_post, w, nct):
    b, lt, d = xa.shape
    tm = ROW_TILE

    def mod_idx(bi, i):
        return (jnp.where(i < nct, b, bi), 0, 0)

    row = lambda bi, i: (bi, i, 0)
    return pl.pallas_call(
        _outproj_kernel,
        grid=(b, lt // tm),
        in_specs=[pl.BlockSpec((1, tm, GROUP_W), row)] * 4 + [
            pl.BlockSpec((1, tm, d), row),
            pl.BlockSpec((1, 1, d), mod_idx),
            pl.BlockSpec((1, d), lambda bi, i: (0, 0)),
            pl.BlockSpec((4 * GROUP_W, d), lambda bi, i: (0, 0))],
        out_specs=pl.BlockSpec((1, tm, d), row),
        out_shape=jax.ShapeDtypeStruct((b, lt, d), F32),
        compiler_params=_cparams("parallel", "parallel"),
        name="outproj",
    )(a, g, s, y, xa, gate, g_post, w)


def _rope(x, cos, sin, shift):
    w = x.shape[-1]
    lane = lax.broadcasted_iota(jnp.int32, x.shape, 1)
    first = (lane & shift) == 0
    swapped = jnp.where(first, -pltpu.roll(x, w - shift, 1), pltpu.roll(x, shift, 1))
    return x * cos + swapped * sin


def _rope_tables(n_ctx, n_lat):
    t = np.arange(n_lat)
    row = (t // GRID_W).astype(np.float64)
    col = (t % GRID_W).astype(np.float64)

    def block(pos, h):
        inv = ROPE_BASE ** (-np.arange(0, h, 2, dtype=np.float64) / h)
        ang = (pos[:, None].astype(np.float32) * inv[None, :].astype(np.float32)).astype(np.float32)
        c, s = np.cos(ang), np.sin(ang)
        return np.concatenate([c, c], -1), np.concatenate([s, s], -1)

    def full(h, lead, width):
        cr, sr = block(row, h)
        cc, sc = block(col, h)
        cos = np.ones((n_ctx + n_lat, width), np.float32)
        sin = np.zeros((n_ctx + n_lat, width), np.float32)
        cos[n_ctx:, lead:lead + 2 * h] = np.concatenate([cr, cc], -1)
        sin[n_ctx:, lead:lead + 2 * h] = np.concatenate([sr, sc], -1)
        return cos, sin

    mc, ms = full(MLA_ROPE // 2, MLA_NOPE, MLA_DK)
    gc, gs = full(GQA_DIM // 2, 0, GQA_DIM)
    reps = GQA_HEADS
    return (jnp.asarray(mc), jnp.asarray(ms),
            jnp.asarray(np.tile(gc, (1, reps))), jnp.asarray(np.tile(gs, (1, reps))))


_NT = (((1,), (1,)), ((), ()))
_TN = (((0,), (0,)), ((), ()))
LOG2E = math.log2(math.e)


def _mla_prep_kernel(p_ref, cos_ref, sin_ref, gq_ref, gkv_ref, wq_ref, wk_ref, wvt_ref, eye_ref,
                     qt_ref, k_ref, vt_ref):
    p = p_ref[0].astype(F32)
    cos, sin = cos_ref[...], sin_ref[...]
    ql = p[:, 0:256]
    rq = lax.rsqrt(jnp.sum(ql * ql, axis=-1, keepdims=True) * (1.0 / MLA_Q_RANK) + EPS)
    qn = (ql * rq * gq_ref[...]).astype(BF16)
    kvl = p[:, 256:384]
    rk = lax.rsqrt(jnp.mean(kvl * kvl, axis=-1, keepdims=True) + EPS)
    kvn = (kvl * rk * gkv_ref[...]).astype(BF16)
    k_rope = _rope(p[:, 384:512], cos, sin, MLA_ROPE // 4)
    scale = (MLA_NOPE + MLA_ROPE) ** -0.5 * LOG2E
    for h in range(MLA_HEADS):
        q = jnp.dot(qn, wq_ref[h], preferred_element_type=F32)
        q = (_rope(q, cos, sin, MLA_ROPE // 4) * scale).astype(BF16)
        qt_ref[0, h] = lax.dot_general(eye_ref[...], q, _NT, preferred_element_type=F32).astype(BF16)
        k = jnp.dot(kvn, wk_ref[h], preferred_element_type=F32) + k_rope
        k_ref[0, h] = k.astype(BF16)
        vt_ref[0, h, 0] = lax.dot_general(wvt_ref[h], kvn, _NT, preferred_element_type=F32).astype(BF16)


def _mla_prep(p_mla, cos, sin, g_cq, g_ckv, wq, wk, wvt):
    b, lt, _ = p_mla.shape
    tm = ROW_TILE
    const2 = lambda bi, i: (0, 0)
    const3 = lambda bi, i: (0, 0, 0)
    return pl.pallas_call(
        _mla_prep_kernel,
        grid=(b, lt // tm),
        in_specs=[pl.BlockSpec((1, tm, 512), lambda bi, i: (bi, i, 0)),
                  pl.BlockSpec((tm, MLA_DK), lambda bi, i: (i, 0)),
                  pl.BlockSpec((tm, MLA_DK), lambda bi, i: (i, 0)),
                  pl.BlockSpec((1, 256), const2),
                  pl.BlockSpec((1, MLA_KV_RANK), const2),
                  pl.BlockSpec((MLA_HEADS, 256, MLA_DK), const3),
                  pl.BlockSpec((MLA_HEADS, MLA_KV_RANK, MLA_DK), const3),
                  pl.BlockSpec((MLA_HEADS, MLA_V, MLA_KV_RANK), const3),
                  pl.BlockSpec((MLA_DK, MLA_DK), const2)],
        out_specs=[pl.BlockSpec((1, MLA_HEADS, MLA_DK, tm), lambda bi, i: (bi, 0, 0, i)),
                   pl.BlockSpec((1, MLA_HEADS, tm, MLA_DK), lambda bi, i: (bi, 0, i, 0)),
                   pl.BlockSpec((1, MLA_HEADS, 1, MLA_V, tm), lambda bi, i: (bi, 0, i, 0, 0))],
        out_shape=[jax.ShapeDtypeStruct((b, MLA_HEADS, MLA_DK, lt), BF16),
                   jax.ShapeDtypeStruct((b, MLA_HEADS, lt, MLA_DK), BF16),
                   jax.ShapeDtypeStruct((b, MLA_HEADS, lt // tm, MLA_V, tm), BF16)],
        compiler_params=_cparams("parallel", "parallel"),
        name="mla_prep",
    )(p_mla, cos, sin, g_cq, g_ckv, wq, wk, wvt, jnp.eye(MLA_DK, dtype=BF16))


def _head_mean_sq(x, ones_bd):
    sq = x * x
    hi = sq.astype(BF16)
    lo = (sq - hi.astype(F32)).astype(BF16)
    s = jnp.dot(hi, ones_bd, preferred_element_type=F32) + jnp.dot(lo, ones_bd, preferred_element_type=F32)
    return s * (1.0 / GQA_DIM)


def _gqa_prep_kernel(p_ref, cos_ref, sin_ref, gq_ref, gk_ref, ones_ref, eye_ref, qt_ref, k_ref, vt_ref):
    p = p_ref[0].astype(F32)
    cos, sin = cos_ref[...], sin_ref[...]
    nq = GQA_HEADS * GQA_DIM
    nk = GQA_KV_HEADS * GQA_DIM
    q = p[:, 0:nq]
    qn = q * lax.rsqrt(_head_mean_sq(q, ones_ref[...]) + EPS) * gq_ref[...]
    qr = (_rope(qn, cos, sin, GQA_DIM // 4) * (GQA_DIM ** -0.5 * LOG2E)).astype(BF16)
    k = p[:, nq:nq + nk]
    kn = k * lax.rsqrt(_head_mean_sq(k, ones_ref[0:nk, 0:nk]) + EPS) * gk_ref[...]
    kr = _rope(kn, cos[:, 0:nk], sin[:, 0:nk], GQA_DIM // 4)
    v = p_ref[0, :, nq + nk:nq + 2 * nk]
    for h in range(GQA_HEADS):
        sel = eye_ref[h * GQA_DIM:(h + 1) * GQA_DIM, :]
        qt_ref[0, h] = lax.dot_general(sel, qr, _NT, preferred_element_type=F32).astype(BF16)
    for h in range(GQA_KV_HEADS):
        k_ref[0, h] = kr[:, h * GQA_DIM:(h + 1) * GQA_DIM].astype(BF16)
        sel = eye_ref[h * GQA_DIM:(h + 1) * GQA_DIM, 0:nk]
        vt_ref[0, h, 0] = lax.dot_general(sel, v, _NT, preferred_element_type=F32).astype(BF16)


def _gqa_prep(p_gqa, cos, sin, g_q, g_k, ones_bd):
    b, lt, _ = p_gqa.shape
    tm = ROW_TILE
    nq = GQA_HEADS * GQA_DIM
    nk = GQA_KV_HEADS * GQA_DIM
    const2 = lambda bi, i: (0, 0)
    return pl.pallas_call(
        _gqa_prep_kernel,
        grid=(b, lt // tm),
        in_specs=[pl.BlockSpec((1, tm, 512), lambda bi, i: (bi, i, 0)),
                  pl.BlockSpec((tm, nq), lambda bi, i: (i, 0)),
                  pl.BlockSpec((tm, nq), lambda bi, i: (i, 0)),
                  pl.BlockSpec((1, nq), const2),
                  pl.BlockSpec((1, nk), const2),
                  pl.BlockSpec((nq, nq), const2),
                  pl.BlockSpec((nq, nq), const2)],
        out_specs=[pl.BlockSpec((1, GQA_HEADS, GQA_DIM, tm), lambda bi, i: (bi, 0, 0, i)),
                   pl.BlockSpec((1, GQA_KV_HEADS, tm, GQA_DIM), lambda bi, i: (bi, 0, i, 0)),
                   pl.BlockSpec((1, GQA_KV_HEADS, 1, GQA_DIM, tm), lambda bi, i: (bi, 0, i, 0, 0))],
        out_shape=[jax.ShapeDtypeStruct((b, GQA_HEADS, GQA_DIM, lt), BF16),
                   jax.ShapeDtypeStruct((b, GQA_KV_HEADS, lt, GQA_DIM), BF16),
                   jax.ShapeDtypeStruct((b, GQA_KV_HEADS, lt // tm, GQA_DIM, tm), BF16)],
        compiler_params=_cparams("parallel", "parallel"),
        name="gqa_prep",
    )(p_gqa, cos, sin, g_q, g_k, ones_bd, jnp.eye(nq, dtype=BF16))


def _attn_kernel(qt_ref, k_ref, vt_ref, gate_ref, place_ref, o_ref, sa_ref, sb_ref, *, heads, group,
                 nct_q, ctx_chunks, all_chunks):
    i = pl.program_id(1)
    n_pairs = jnp.where(i < nct_q, (ctx_chunks - 1) // 2, (all_chunks - 1) // 2)
    tq = qt_ref.shape[3]
    dv = vt_ref.shape[3]

    def score(j, h, dst):
        rows = pl.ds(pl.multiple_of(j * ATT_TK, ATT_TK), ATT_TK)
        dst[h] = jnp.dot(k_ref[0, h // group, rows, :], qt_ref[0, h], preferred_element_type=F32)

    def step(j, carry, cur, nxt):
        new = []
        for h in range(heads):
            if nxt is not None:
                score(j + 1, h, nxt)
            m, l, acc = carry[h]
            s = cur[h]
            m_new = jnp.maximum(m, jnp.max(s, axis=0, keepdims=True))
            alpha = jnp.exp2(m - m_new)
            p = jnp.exp2(s - m_new)
            l = alpha * l + jnp.sum(p, axis=0, keepdims=True)
            pv = jnp.dot(vt_ref[0, h // group, j], p.astype(BF16), preferred_element_type=F32)
            new.append((m_new, l, alpha * acc + pv))
        return tuple(new)

    def pair(t, carry):
        carry = step(2 * t, carry, sa_ref, sb_ref)
        return step(2 * t + 1, carry, sb_ref, sa_ref)

    for h in range(heads):
        score(0, h, sa_ref)
    init = tuple((jnp.full((1, tq), -jnp.inf, F32), jnp.zeros((1, tq), F32), jnp.zeros((dv, tq), F32))
                 for _ in range(heads))
    carry = lax.fori_loop(0, n_pairs, pair, init)
    res = step(2 * n_pairs, carry, sa_ref, None)
    out = jnp.zeros((tq, GROUP_W), F32)
    for h in range(heads):
        _, l, acc = res[h]
        o = (acc * (1.0 / l)).astype(BF16)
        out = out + lax.dot_general(o, place_ref[h], _TN, preferred_element_type=F32)
    g = gate_ref[0].astype(F32)
    o_ref[0] = (out * _silu(g)).astype(BF16)


def _attention(qt, k, vt, gate_src, gate_block, n_ctx):
    b, heads, dk, lt = qt.shape
    hk, n_chunks, dv = k.shape[1], vt.shape[2], vt.shape[3]
    place = np.zeros((heads, dv, GROUP_W), np.float32)
    for h in range(heads):
        place[h, np.arange(dv), h * dv + np.arange(dv)] = 1.0
    ctx_chunks, all_chunks = n_ctx // ATT_TK, lt // ATT_TK
    assert ctx_chunks % 2 == 1 and all_chunks % 2 == 1, "chunk pairs are pipelined with one chunk left over"
    kern = functools.partial(_attn_kernel, heads=heads, group=heads // hk, nct_q=n_ctx // ATT_TQ,
                             ctx_chunks=ctx_chunks, all_chunks=all_chunks)
    return pl.pallas_call(
        kern,
        grid=(b, lt // ATT_TQ),
        in_specs=[pl.BlockSpec((1, heads, dk, ATT_TQ), lambda bi, i: (bi, 0, 0, i)),
                  pl.BlockSpec((1, hk, lt, dk), lambda bi, i: (bi, 0, 0, 0)),
                  pl.BlockSpec((1, hk, n_chunks, dv, ATT_TK), lambda bi, i: (bi, 0, 0, 0, 0)),
                  pl.BlockSpec((1, ATT_TQ, GROUP_W), lambda bi, i: (bi, i, gate_block)),
                  pl.BlockSpec((heads, dv, GROUP_W), lambda bi, i: (0, 0, 0))],
        out_specs=pl.BlockSpec((1, ATT_TQ, GROUP_W), lambda bi, i: (bi, i, 0)),
        out_shape=jax.ShapeDtypeStruct((b, lt, GROUP_W), BF16),
        scratch_shapes=[pltpu.VMEM((heads, ATT_TK, ATT_TQ), F32), pltpu.VMEM((heads, ATT_TK, ATT_TQ), F32)],
        compiler_params=_cparams("parallel", "arbitrary"),
        name="attention",
    )(qt, k, vt, gate_src, jnp.asarray(place, BF16))


def _ssm_kernel(*refs, steps, nb, reverse, finish):
    if finish:
        (u_ref, bre_ref, bim_ref, are_ref, aim_ref, cre_ref, cim_ref, yf_ref, d_ref, gw_ref, gb_ref,
         o_ref, sre, sim, car_re, car_im) = refs
    else:
        (u_ref, bre_ref, bim_ref, are_ref, aim_ref, cre_ref, cim_ref,
         o_ref, sre, sim, car_re, car_im) = refs

    @pl.when(pl.program_id(0) == 0)
    def _():
        car_re[...] = jnp.zeros_like(car_re)
        car_im[...] = jnp.zeros_like(car_im)

    u = u_ref[:, 0:GROUP_W]
    sre[...] = jnp.dot(u, bre_ref[...], preferred_element_type=F32)
    sim[...] = jnp.dot(u, bim_ref[...], preferred_element_type=F32)
    n_state = sre.shape[1]
    lw = min(SSM_LANE_SPLIT, n_state)
    for c0 in range(0, n_state, lw):
        ls = slice(c0, c0 + lw)
        ar = jnp.broadcast_to(are_ref[:, ls], (nb, lw))
        ai = jnp.broadcast_to(aim_ref[:, ls], (nb, lw))

        def body(tt, carry, ls=ls, ar=ar, ai=ai):
            sr, si = carry
            t = (steps - 1 - tt) if reverse else tt
            rows = pl.ds(pl.multiple_of(t * nb, nb), nb)
            nr = ar * sr - ai * si + sre[rows, ls]
            ni = ar * si + ai * sr + sim[rows, ls]
            sre[rows, ls] = nr
            sim[rows, ls] = ni
            return nr, ni

        sr, si = lax.fori_loop(0, steps, body, (car_re[:, ls], car_im[:, ls]), unroll=2)
        car_re[:, ls] = sr
        car_im[:, ls] = si
    y = jnp.dot(sre[...].astype(BF16), cre_ref[...], preferred_element_type=F32)
    y += jnp.dot(sim[...].astype(BF16), cim_ref[...], preferred_element_type=F32)
    if not finish:
        o_ref[...] = y
    else:
        uf = u.astype(F32)
        y = y + yf_ref[...] + d_ref[...] * uf
        z = jax.nn.gelu(y, approximate=True)
        gl = jnp.dot(z.astype(BF16), gw_ref[...], preferred_element_type=F32) + gb_ref[...]
        z = z * jax.nn.sigmoid(gl)
        gate = u_ref[:, GROUP_W:2 * GROUP_W].astype(F32)
        o_ref[...] = (z * _silu(gate)).astype(BF16)


def _ssm_direction(u_tm, mats, nb, n_ctx_chunks, reverse, finish_args=None):
    rows, _ = u_tm.shape
    steps = SSM_T
    blk = steps * nb
    n_chunks = rows // blk
    n_state = mats[0].shape[1]
    if reverse:
        cidx = lambda i: (jnp.where(i < n_ctx_chunks, n_ctx_chunks - 1 - i, n_chunks - 1 - (i - n_ctx_chunks)), 0)
    else:
        cidx = lambda i: (i, 0)
    const = lambda i: (0, 0)
    in_specs = [pl.BlockSpec((blk, SSM_PACK), cidx)] + [pl.BlockSpec(m.shape, const) for m in mats]
    args = [u_tm, *mats]
    finish = finish_args is not None
    if finish:
        yf, d_skip, glu_w, glu_b = finish_args
        in_specs += [pl.BlockSpec((blk, GROUP_W), cidx), pl.BlockSpec(d_skip.shape, const),
                     pl.BlockSpec(glu_w.shape, const), pl.BlockSpec(glu_b.shape, const)]
        args += [yf, d_skip, glu_w, glu_b]
    return pl.pallas_call(
        functools.partial(_ssm_kernel, steps=steps, nb=nb, reverse=reverse, finish=finish),
        grid=(n_chunks,),
        in_specs=in_specs,
        out_specs=pl.BlockSpec((blk, GROUP_W), cidx),
        out_shape=jax.ShapeDtypeStruct((rows, GROUP_W), BF16 if finish else F32),
        scratch_shapes=[pltpu.VMEM((blk, n_state), F32), pltpu.VMEM((blk, n_state), F32),
                        pltpu.VMEM((nb, n_state), F32), pltpu.VMEM((nb, n_state), F32)],
        compiler_params=_cparams("arbitrary"),
        name="ssm_rev" if reverse else "ssm_fwd",
    )(*args)


def _ssm_matrices(lam_re, lam_im, log_step, b_re, b_im, c_re, c_im):
    lam = lax.complex(lam_re.astype(F32), lam_im.astype(F32))
    step = jnp.exp(log_step.astype(F32))[:, None]
    a_bar = jnp.exp(lam * step)
    b_bar = ((a_bar - 1.0) / lam)[..., None] * lax.complex(b_re.astype(F32), b_im.astype(F32))
    eye = jnp.eye(SSM_GROUPS, dtype=F32)
    n_in = SSM_GROUPS * SSM_GROUP
    n_state = SSM_GROUPS * SSM_STATE

    def drive(m):
        return jnp.einsum('gph,gk->ghkp', m, eye).reshape(n_in, n_state).astype(BF16)

    def readout(m):
        return jnp.einsum('ghp,gk->gpkh', m, eye).reshape(n_state, n_in).astype(BF16)

    return (drive(jnp.real(b_bar)), drive(jnp.imag(b_bar)),
            jnp.real(a_bar).reshape(1, n_state), jnp.imag(a_bar).reshape(1, n_state),
            readout(c_re.astype(F32)), readout(-c_im.astype(F32)))


def _hy_pre_kernel(x_ref, prev_ref, next_ref, w_ref, b_ref, v_ref, x1_ref, x2_ref, sg_ref, *, n_tiles):
    i = pl.program_id(1)
    nconv = 3 * HY_W
    x = x_ref[0, :, 0:nconv].astype(F32)
    tm = x.shape[0]
    prev_row = jnp.where(i > 0, prev_ref[0, 7:8, 0:nconv].astype(F32), 0.0)
    next_row = jnp.where(i < n_tiles - 1, next_ref[0, 0:1, 0:nconv].astype(F32), 0.0)
    rid = lax.broadcasted_iota(jnp.int32, x.shape, 0)
    xm = jnp.where(rid == 0, prev_row, pltpu.roll(x, 1, 0))
    xp = jnp.where(rid == tm - 1, next_row, pltpu.roll(x, tm - 1, 0))
    proj = xm * w_ref[0:1] + x * w_ref[1:2] + xp * w_ref[2:3] + b_ref[...]
    gate = x_ref[0, :, nconv:nconv + HY_W].astype(F32)
    for o_ref, val in ((v_ref, proj[:, 0:HY_W]), (x1_ref, proj[:, HY_W:2 * HY_W]),
                       (x2_ref, proj[:, 2 * HY_W:3 * HY_W]), (sg_ref, _silu(gate))):
        for hf in range(HY_W // LANES):
            o_ref[0, hf] = val[:, hf * LANES:(hf + 1) * LANES].astype(BF16)


def _hy_pre(p_hy, conv_w, conv_b, tile0, n_tiles):
    b = p_hy.shape[0]
    tm = ROW_TILE
    sub = tm // 8
    n = n_tiles * tm
    last_sub = p_hy.shape[1] // 8 - 1
    halves = HY_W // LANES
    const2 = lambda bi, i: (0, 0)
    out_spec = pl.BlockSpec((1, halves, tm, LANES), lambda bi, i: (bi, 0, i, 0))
    out_shape = jax.ShapeDtypeStruct((b, halves, n, LANES), BF16)
    return pl.pallas_call(
        functools.partial(_hy_pre_kernel, n_tiles=n_tiles),
        grid=(b, n_tiles),
        in_specs=[pl.BlockSpec((1, tm, HY_PACK), lambda bi, i: (bi, i + tile0, 0)),
                  pl.BlockSpec((1, 8, HY_PACK), lambda bi, i: (bi, jnp.maximum((i + tile0) * sub - 1, 0), 0)),
                  pl.BlockSpec((1, 8, HY_PACK), lambda bi, i: (bi, jnp.minimum((i + tile0 + 1) * sub, last_sub), 0)),
                  pl.BlockSpec((3, 3 * HY_W), const2),
                  pl.BlockSpec((1, 3 * HY_W), const2)],
        out_specs=[out_spec] * 4,
        out_shape=[out_shape] * 4,
        compiler_params=_cparams("parallel", "parallel"),
        name="hyena_pre",
    )(p_hy, p_hy, p_hy, conv_w, conv_b)


def _bitrev(p, bits):
    r = 0
    for i in range(bits):
        r = (r << 1) | ((p >> i) & 1)
    return r


@functools.lru_cache(maxsize=None)
def _fft_plan(n):
    r = FFT_R
    nb = n // r
    m = 2 * nb
    big = 2 * n
    bits = int(round(math.log2(nb)))
    assert nb >= 2 and (1 << bits) == nb
    jmap = np.zeros(m, np.int64)
    for p in range(nb):
        jmap[p] = 2 * _bitrev(p, bits)
        jmap[nb + p] = 2 * _bitrev(p, bits) + 1
    k2 = np.arange(r)
    n2 = np.arange(r)
    g = np.zeros((m, 2 * r, 2 * r), np.float32)
    for s in range(m):
        ang = -2.0 * np.pi * (np.outer(k2, n2) / r + np.outer(np.ones(r), n2) * jmap[s] / big)
        gr, gi = np.cos(ang), np.sin(ang)
        g[s] = np.block([[gr, -gi], [gi, gr]])
    half0 = nb // 2
    k = np.arange(half0)
    wnb = np.exp(-2j * np.pi * k / nb)
    wm0 = np.exp(-2j * np.pi * k / m)
    wm1 = np.exp(-2j * np.pi * (k + half0) / m)
    tw_edge = np.stack([wnb.real, wnb.imag, wm0.real, wm0.imag, wm1.real, wm1.imag]).astype(np.float32)
    mids = []
    h = half0 // 2
    while h >= 1:
        kk = np.arange(nb // 2) % h
        w = np.exp(-2j * np.pi * kk / (2 * h))
        mids.append(np.stack([w.real, w.imag]))
        h //= 2
    tw_mid = (np.concatenate(mids, 0) if mids else np.zeros((2, max(nb // 2, 1)))).astype(np.float32)
    freq = (jmap[:, None] + m * k2[None, :])
    return dict(nb=nb, m=m, g=g, tw_edge=tw_edge, tw_mid=tw_mid, n_mid=len(mids), freq=freq)


def _cmul(ar, ai, wr, wi):
    return ar * wr - ai * wi, ar * wi + ai * wr


def _hy_conv_kernel(*refs, nb, n_mid, n_post):
    tw_edge, tw_mid, a_ref, bias_ref, kh_ref, g_ref = refs[:6]
    post_refs = refs[6:6 + n_post]
    o_ref, w = refs[6 + n_post], refs[7 + n_post]
    r = FFT_R
    m = 2 * nb
    half0 = nb // 2
    re, im = slice(0, r), slice(r, 2 * r)

    def rows(k):
        return pl.ds(pl.multiple_of(k * r, r), r)

    def first(k, c):
        wr, wi = tw_edge[0, k], tw_edge[1, k]
        ar, ai = a_ref[0, 0, rows(k), :].astype(F32), a_ref[1, 0, rows(k), :].astype(F32)
        br, bi = a_ref[0, 0, rows(k + half0), :].astype(F32), a_ref[1, 0, rows(k + half0), :].astype(F32)
        w[k, re], w[k, im] = ar + br, ai + bi
        dr, di = _cmul(ar - br, ai - bi, wr, wi)
        w[k + half0, re], w[k + half0, im] = dr, di
        a2r, a2i = _cmul(ar, ai, tw_edge[2, k], tw_edge[3, k])
        b2r, b2i = _cmul(br, bi, tw_edge[4, k], tw_edge[5, k])
        w[nb + k, re], w[nb + k, im] = a2r + b2r, a2i + b2i
        dr, di = _cmul(a2r - b2r, a2i - b2i, wr, wi)
        w[nb + k + half0, re], w[nb + k + half0, im] = dr, di
        return c

    lax.fori_loop(0, half0, first, 0)

    def slot_pair(bb, lg):
        branch = bb // half0
        bf = bb - branch * half0
        i0 = branch * nb + ((bf >> lg) << (lg + 1)) + (bf & ((1 << lg) - 1))
        return bf, i0, i0 + (1 << lg)

    for s in range(n_mid):
        lg = int(round(math.log2(half0))) - 1 - s

        def dif(bb, c, s=s, lg=lg):
            bf, i0, i1 = slot_pair(bb, lg)
            wr, wi = tw_mid[2 * s, bf], tw_mid[2 * s + 1, bf]
            ar, ai, br, bi = w[i0, re], w[i0, im], w[i1, re], w[i1, im]
            w[i0, re], w[i0, im] = ar + br, ai + bi
            dr, di = _cmul(ar - br, ai - bi, wr, wi)
            w[i1, re], w[i1, im] = dr, di
            return c

        lax.fori_loop(0, nb, dif, 0)

    def spectral(s, c):
        gm = g_ref[s]
        x = jnp.dot(gm, w[s].astype(BF16), preferred_element_type=F32)
        yr, yi = _cmul(x[re], x[im], kh_ref[0, s, re, :], kh_ref[0, s, im, :])
        y = jnp.concatenate([yr, yi], axis=0).astype(BF16)
        w[s] = lax.dot_general(gm, y, (((0,), (0,)), ((), ())), preferred_element_type=F32)
        return c

    lax.fori_loop(0, m, spectral, 0)

    for s in reversed(range(n_mid)):
        lg = int(round(math.log2(half0))) - 1 - s

        def dit(bb, c, s=s, lg=lg):
            bf, i0, i1 = slot_pair(bb, lg)
            wr, wi = tw_mid[2 * s, bf], -tw_mid[2 * s + 1, bf]
            ar, ai = w[i0, re], w[i0, im]
            br, bi = _cmul(w[i1, re], w[i1, im], wr, wi)
            w[i0, re], w[i0, im] = ar + br, ai + bi
            w[i1, re], w[i1, im] = ar - br, ai - bi
            return c

        lax.fori_loop(0, nb, dit, 0)

    bias = bias_ref[0]

    def emit(k, yr, yi):
        for bsel, y in ((0, yr), (1, yi)):
            a = a_ref[bsel, 0, rows(k), :].astype(F32)
            val = y + bias * a
            for p_ref in post_refs:
                val = val * p_ref[bsel, 0, rows(k), :].astype(F32)
            o_ref[bsel, 0, rows(k), :] = val.astype(o_ref.dtype)

    def last(k, c):
        wr, wi = tw_edge[0, k], -tw_edge[1, k]
        ar, ai = w[k, re], w[k, im]
        br, bi = _cmul(w[k + half0, re], w[k + half0, im], wr, wi)
        cr, ci = w[nb + k, re], w[nb + k, im]
        dr, di = _cmul(w[nb + k + half0, re], w[nb + k + half0, im], wr, wi)
        o0r, o0i = _cmul(cr + dr, ci + di, tw_edge[2, k], -tw_edge[3, k])
        o1r, o1i = _cmul(cr - dr, ci - di, tw_edge[4, k], -tw_edge[5, k])
        emit(k, ar + br + o0r, ai + bi + o0i)
        emit(k + half0, ar - br + o1r, ai - bi + o1i)
        return c

    lax.fori_loop(0, half0, last, 0)


def _hy_conv(a, khat, bias, posts, out_dtype):
    b, halves, n, lanes = a.shape
    plan = _fft_plan(n)
    nb, m = plan['nb'], plan['m']
    r = FFT_R
    sig = pl.BlockSpec((2, 1, n, lanes), lambda hf, pr: (pr, hf, 0, 0))
    smem = pl.BlockSpec(memory_space=pltpu.SMEM)
    once = pl.Buffered(1)
    in_specs = [smem, smem, sig,
                pl.BlockSpec((1, 1, lanes), lambda hf, pr: (hf, 0, 0)),
                pl.BlockSpec((1, m, 2 * r, lanes), lambda hf, pr: (hf, 0, 0, 0), pipeline_mode=once),
                pl.BlockSpec((m, 2 * r, 2 * r), lambda hf, pr: (0, 0, 0), pipeline_mode=once)]
    in_specs += [sig] * len(posts)
    return pl.pallas_call(
        functools.partial(_hy_conv_kernel, nb=nb, n_mid=plan['n_mid'], n_post=len(posts)),
        grid=(halves, b // 2),
        in_specs=in_specs,
        out_specs=sig,
        out_shape=jax.ShapeDtypeStruct(a.shape, out_dtype),
        scratch_shapes=[pltpu.VMEM((m, 2 * r, lanes), F32)],
        compiler_params=_cparams("arbitrary", "arbitrary"),
        name="hyena_conv",
    )(jnp.asarray(plan['tw_edge']), jnp.asarray(plan['tw_mid']), a, bias, khat,
      jnp.asarray(plan['g'], BF16), *posts)


def _hyena_filter_spectrum(n, w1, b1, fr1, w2, b2, fr2, w3):
    t = jnp.linspace(0.0, 1.0, n, dtype=F32)[:, None]
    omega = 2.0 * math.pi * jnp.arange(n, dtype=F32)[:, None] / n
    bands = jnp.linspace(1e-4, HY_BANDS - 1, HY_BANDS, dtype=F32)[None, :]
    z = jnp.concatenate([t, jnp.cos(bands * omega), -jnp.sin(bands * omega)], axis=-1)
    hp = lax.Precision.HIGHEST
    h = jnp.sin(fr1.astype(F32) * (jnp.dot(z, w1.astype(F32), precision=hp) + b1.astype(F32)))
    h = jnp.sin(fr2.astype(F32) * (jnp.dot(h, w2.astype(F32), precision=hp) + b2.astype(F32)))
    h = jnp.dot(h, w3.astype(F32), precision=hp).reshape(n, HY_ORDER, 2, HY_W)
    max_decay = math.log(HY_DECAY_TARGET) / HY_FAST_DECAY
    min_decay = math.log(HY_DECAY_TARGET) / HY_SLOW_DECAY
    deltas = jnp.linspace(min_decay, max_decay, HY_W, dtype=F32)
    h = h * jnp.exp(-t[:, :, None, None] * jnp.abs(deltas))
    h = h * lax.rsqrt(jnp.sum(h * h, axis=(0, 2), keepdims=True) + EPS)
    fwd, bwd = h[:, :, 0], h[:, :, 1]
    zero = jnp.zeros((1, HY_ORDER, HY_W), F32)
    k_circ = jnp.concatenate([fwd, zero, bwd[1:][::-1]], axis=0)
    spec = jnp.fft.fft(k_circ, axis=0) * (1.0 / (2 * n))
    plan = _fft_plan(n)
    sel = spec[jnp.asarray(plan['freq'])]
    tab = jnp.concatenate([jnp.real(sel), jnp.imag(sel)], axis=1)
    tab = tab.reshape(plan['m'], 2 * FFT_R, HY_ORDER, HY_W // LANES, LANES)
    return jnp.transpose(tab, (2, 3, 0, 1, 4)).astype(F32)


def _pack_w_in(w_in):
    d = w_in.shape[0]
    z = lambda n: jnp.zeros((d, n), w_in.dtype)
    o = 0
    q_lat = w_in[:, o:o + MLA_Q_RANK]; o += MLA_Q_RANK
    kv_lat = w_in[:, o:o + MLA_KV_RANK]; o += MLA_KV_RANK
    k_rope = w_in[:, o:o + MLA_ROPE]; o += MLA_ROPE
    gate = w_in[:, o:o + GROUP_W]; o += GROUP_W
    mla = jnp.concatenate([q_lat, z(256 - MLA_Q_RANK), kv_lat, z(MLA_NOPE), k_rope,
                           z(MLA_DK - MLA_NOPE - MLA_ROPE), gate], axis=1)
    return jnp.concatenate([mla, w_in[:, o:]], axis=1).astype(BF16)


def _pack_mla_up(w_uq, w_ukv):
    dq = MLA_NOPE + MLA_ROPE
    wq = w_uq.reshape(MLA_Q_RANK, MLA_HEADS, dq).transpose(1, 0, 2)
    wq = jnp.pad(wq, ((0, 0), (0, 256 - MLA_Q_RANK), (0, MLA_DK - dq))).astype(BF16)
    wkv = w_ukv.reshape(MLA_KV_RANK, MLA_HEADS, MLA_NOPE + MLA_V).transpose(1, 0, 2)
    wk = jnp.pad(wkv[:, :, :MLA_NOPE], ((0, 0), (0, 0), (0, MLA_DK - MLA_NOPE))).astype(BF16)
    wvt = jnp.swapaxes(wkv[:, :, MLA_NOPE:], 1, 2).astype(BF16)
    return wq, wk, wvt


def _merge_halves(y_ctx, y_lat):
    y = jnp.concatenate([y_ctx, y_lat], axis=2)
    b, halves, lt, lanes = y.shape
    return jnp.transpose(y, (0, 2, 1, 3)).reshape(b, lt, halves * lanes)


def kernel(x, c, ctx, c_ctx, w_mod, b_mod, g_pre, g_post, w_in, w_out, mla_g_cq, mla_w_uq, mla_g_ckv, mla_w_ukv, gqa_g_q, gqa_g_k, ssm_lambda_re, ssm_lambda_im, ssm_log_step, ssm_b_re, ssm_b_im, ssm_c_re, ssm_c_im, ssm_d, ssm_glu_w, ssm_glu_b, hy_conv_w, hy_conv_b, hy_f_w1, hy_f_b1, hy_f_freq1, hy_f_w2, hy_f_b2, hy_f_freq2, hy_f_w3, hy_bias):
    b, n_lat, d = x.shape
    n_ctx = ctx.shape[1]
    depth = w_in.shape[0]
    lt = n_ctx + n_lat
    nct = n_ctx // ROW_TILE
    assert n_ctx % ROW_TILE == 0 and n_lat % ROW_TILE == 0 and b % 2 == 0 and b % 8 == 0

    n_cond = -(-(b + 1) // 8) * 8
    cond = jnp.zeros((n_cond, d), F32).at[:b].set(c).at[b].set(c_ctx)
    mod = _modulation(cond, w_mod, b_mod)[:, :b + 1]
    shift, scale, gate = mod[..., :d], mod[..., d:2 * d], mod[..., 2 * d:]

    mla_cos, mla_sin, gqa_cos, gqa_sin = _rope_tables(n_ctx, n_lat)
    nq = GQA_HEADS * GQA_DIM
    ones_bd = jnp.asarray(np.kron(np.eye(GQA_HEADS), np.ones((GQA_DIM, GQA_DIM))), BF16)

    xa = jnp.concatenate([ctx, x], axis=1)
    for l in range(depth):
        sc = (g_pre[l][None, :] * (1.0 + scale[l]))[:, None, :]
        sh = shift[l][:, None, :]
        p_mla, p_gqa, p_ssm, p_hy = _inproj(xa, sc, sh, _pack_w_in(w_in[l]), nct)

        wq, wk, wvt = _pack_mla_up(mla_w_uq[l], mla_w_ukv[l])
        g_cq = jnp.pad(mla_g_cq[l], (0, 256 - MLA_Q_RANK)).reshape(1, 256)
        qt, k, vt = _mla_prep(p_mla, mla_cos, mla_sin, g_cq, mla_g_ckv[l].reshape(1, -1), wq, wk, wvt)
        a_out = _attention(qt, k, vt, p_mla, 2, n_ctx)

        qt, k, vt = _gqa_prep(p_gqa, gqa_cos, gqa_sin, jnp.tile(gqa_g_q[l], GQA_HEADS).reshape(1, nq),
                              jnp.tile(gqa_g_k[l], GQA_KV_HEADS).reshape(1, -1), ones_bd)
        g_out = _attention(qt, k, vt, p_gqa, 2, n_ctx)

        u_tm = jnp.transpose(p_ssm, (1, 0, 2)).reshape(lt * b, SSM_PACK)
        mats = [_ssm_matrices(ssm_lambda_re[l, di], ssm_lambda_im[l, di], ssm_log_step[l, di],
                              ssm_b_re[l, di], ssm_b_im[l, di], ssm_c_re[l, di], ssm_c_im[l, di])
                for di in range(2)]
        n_ctx_chunks = n_ctx // SSM_T
        y_f = _ssm_direction(u_tm, mats[0], b, n_ctx_chunks, False)
        fin = (y_f, ssm_d[l].reshape(1, -1), ssm_glu_w[l].astype(BF16), ssm_glu_b[l].reshape(1, -1))
        s_tm = _ssm_direction(u_tm, mats[1], b, n_ctx_chunks, True, fin)
        s_out = jnp.transpose(s_tm.reshape(lt, b, GROUP_W), (1, 0, 2))

        filt = (hy_f_w1[l], hy_f_b1[l], hy_f_freq1[l], hy_f_w2[l], hy_f_b2[l], hy_f_freq2[l], hy_f_w3[l])
        conv_b = hy_conv_b[l].reshape(1, -1)
        parts = []
        for tile0, n_tiles in ((0, nct), (nct, n_lat // ROW_TILE)):
            n = n_tiles * ROW_TILE
            hv, hx1, hx2, hsg = _hy_pre(p_hy, hy_conv_w[l], conv_b, tile0, n_tiles)
            khat = _hyena_filter_spectrum(n, *filt)
            bias = hy_bias[l].astype(F32).reshape(HY_ORDER, HY_W // LANES, 1, LANES)
            z1 = _hy_conv(hv, khat[0], bias[0], (hx1,), BF16)
            parts.append(_hy_conv(z1, khat[1], bias[1], (hx2, hsg), BF16))
        y_out = _merge_halves(*parts)

        xa = _outproj(a_out, g_out, s_out, y_out, xa, gate[l][:, None, :], g_post[l].reshape(1, d),
                      w_out[l].astype(BF16), nct)
    return xa[:, n_ctx:]
```

```python
import functools
import math

import numpy as np
import jax
import jax.numpy as jnp
from jax import lax
from jax.experimental import pallas as pl
from jax.experimental.pallas import tpu as pltpu

F32 = jnp.float32
BF16 = jnp.bfloat16

GRID_W = 64
ROPE_BASE = 10000.0
EPS = 1e-6
GROUP_W = 256
MLA_HEADS, MLA_NOPE, MLA_ROPE, MLA_V = 4, 64, 32, 64
MLA_Q_RANK, MLA_KV_RANK = 192, 128
GQA_HEADS, GQA_KV_HEADS, GQA_DIM = 4, 2, 64
SSM_GROUPS, SSM_GROUP, SSM_STATE = 16, 16, 64
HY_W, HY_ORDER, HY_EMB, HY_HIDDEN = 256, 2, 33, 64
HY_BANDS = (HY_EMB - 1) // 2
HY_FAST_DECAY, HY_SLOW_DECAY, HY_DECAY_TARGET = 0.3, 1.5, 1e-2

MLA_PACK = 768
GQA_PACK = 768
SSM_PACK = 512
HY_PACK = 1024
IN_PACK = MLA_PACK + GQA_PACK + SSM_PACK + HY_PACK
MLA_DK = 128

LANES = 128
ROW_TILE = 256
ATT_TQ = 256
ATT_TK = 256
ATT_UNROLLS = (8, 4, 2)
ONES_ROWS = 16
SSM_T = 64
SSM_LANE_SPLIT = 512
FFT_R = 128
FFT_SLOTS = 4
VMEM_LIMIT = 52 * 1024 * 1024
HY_VMEM_LIMIT = 58 * 1024 * 1024


def _cparams(*sem):
    return pltpu.CompilerParams(dimension_semantics=sem, vmem_limit_bytes=VMEM_LIMIT)


def _silu(x):
    return x * jax.nn.sigmoid(x)


def _mod_kernel(c_ref, w_ref, b_ref, o_ref):
    c = c_ref[...]
    s = _silu(c).astype(BF16)
    o_ref[0] = jnp.dot(s, w_ref[0].astype(BF16), preferred_element_type=F32) + b_ref[0]


def _modulation(cond, w_mod, b_mod):
    depth, d, n3 = w_mod.shape
    r = cond.shape[0]
    tn = 512
    return pl.pallas_call(
        _mod_kernel,
        grid=(depth, n3 // tn),
        in_specs=[pl.BlockSpec((r, d), lambda l, j: (0, 0)),
                  pl.BlockSpec((1, d, tn), lambda l, j: (l, 0, j)),
                  pl.BlockSpec((1, 1, tn), lambda l, j: (l, 0, j))],
        out_specs=pl.BlockSpec((1, r, tn), lambda l, j: (l, 0, j)),
        out_shape=jax.ShapeDtypeStruct((depth, r, n3), F32),
        compiler_params=_cparams("parallel", "parallel"),
        name="modulation",
    )(cond, w_mod, b_mod.reshape(depth, 1, n3))


def _inproj_kernel(x_ref, sc_ref, sh_ref, w_ref, o_mla, o_gqa, o_ssm, o_hy):
    x = x_ref[0]
    ms = jnp.mean(x * x, axis=-1, keepdims=True)
    h = x * lax.rsqrt(ms + EPS) * sc_ref[0] + sh_ref[0]
    hb = h.astype(BF16)
    c0 = 0
    for o_ref, width in ((o_mla, MLA_PACK), (o_gqa, GQA_PACK), (o_ssm, SSM_PACK), (o_hy, HY_PACK)):
        o_ref[0] = jnp.dot(hb, w_ref[:, c0:c0 + width], preferred_element_type=F32).astype(BF16)
        c0 += width


def _inproj(xa, scale, shift, w, nct):
    b, lt, d = xa.shape
    tm = ROW_TILE
    widths = (MLA_PACK, GQA_PACK, SSM_PACK, HY_PACK)

    def mod_idx(bi, i):
        return (jnp.where(i < nct, b, bi), 0, 0)

    return pl.pallas_call(
        _inproj_kernel,
        grid=(b, lt // tm),
        in_specs=[pl.BlockSpec((1, tm, d), lambda bi, i: (bi, i, 0)),
                  pl.BlockSpec((1, 1, d), mod_idx),
                  pl.BlockSpec((1, 1, d), mod_idx),
                  pl.BlockSpec((d, IN_PACK), lambda bi, i: (0, 0))],
        out_specs=[pl.BlockSpec((1, tm, wd), lambda bi, i: (bi, i, 0)) for wd in widths],
        out_shape=[jax.ShapeDtypeStruct((b, lt, wd), BF16) for wd in widths],
        compiler_params=_cparams("parallel", "parallel"),
        name="inproj",
    )(xa, scale, shift, w)


def _outproj_kernel(a_ref, g_ref, s_ref, y_ref, x_ref, gt_ref, gp_ref, w_ref, o_ref):
    half = a_ref.shape[1] // 2
    for hs in (slice(0, half), slice(half, 2 * half)):
        acc = jnp.dot(a_ref[0, hs], w_ref[0:GROUP_W], preferred_element_type=F32)
        acc += jnp.dot(g_ref[0, hs], w_ref[GROUP_W:2 * GROUP_W], preferred_element_type=F32)
        acc += jnp.dot(s_ref[0, hs], w_ref[2 * GROUP_W:3 * GROUP_W], preferred_element_type=F32)
        acc += jnp.dot(y_ref[0, hs], w_ref[3 * GROUP_W:4 * GROUP_W], preferred_element_type=F32)
        ms = jnp.mean(acc * acc, axis=-1, keepdims=True)
        o_ref[0, hs] = x_ref[0, hs] + gt_ref[0] * (acc * lax.rsqrt(ms + EPS) * gp_ref[...])


def _outproj(a, g, s, y, xa, gate, g_post, w, nct, tile0=0):
    b, lt, d = xa.shape
    tm = ROW_TILE

    def mod_idx(bi, i):
        return (jnp.where(i + tile0 < nct, b, bi), 0, 0)

    row = lambda bi, i: (bi, i + tile0, 0)
    return pl.pallas_call(
        _outproj_kernel,
        grid=(b, lt // tm - tile0),
        in_specs=[pl.BlockSpec((1, tm, GROUP_W), row)] * 4 + [
            pl.BlockSpec((1, tm, d), row),
            pl.BlockSpec((1, 1, d), mod_idx),
            pl.BlockSpec((1, d), lambda bi, i: (0, 0)),
            pl.BlockSpec((4 * GROUP_W, d), lambda bi, i: (0, 0))],
        out_specs=pl.BlockSpec((1, tm, d), lambda bi, i: (bi, i, 0)),
        out_shape=jax.ShapeDtypeStruct((b, lt - tile0 * tm, d), F32),
        compiler_params=_cparams("parallel", "parallel"),
        name="outproj",
    )(a, g, s, y, xa, gate, g_post, w)


def _rope(x, cos, sin, shift):
    w = x.shape[-1]
    lane = lax.broadcasted_iota(jnp.int32, x.shape, 1)
    first = (lane & shift) == 0
    swapped = jnp.where(first, -pltpu.roll(x, w - shift, 1), pltpu.roll(x, shift, 1))
    return x * cos + swapped * sin


def _rope_tables(n_ctx, n_lat):
    t = np.arange(n_lat)
    row = (t // GRID_W).astype(np.float64)
    col = (t % GRID_W).astype(np.float64)

    def block(pos, h):
        inv = ROPE_BASE ** (-np.arange(0, h, 2, dtype=np.float64) / h)
        ang = (pos[:, None].astype(np.float32) * inv[None, :].astype(np.float32)).astype(np.float32)
        c, s = np.cos(ang), np.sin(ang)
        return np.concatenate([c, c], -1), np.concatenate([s, s], -1)

    def full(h, lead, width):
        cr, sr = block(row, h)
        cc, sc = block(col, h)
        cos = np.ones((n_ctx + n_lat, width), np.float32)
        sin = np.zeros((n_ctx + n_lat, width), np.float32)
        cos[n_ctx:, lead:lead + 2 * h] = np.concatenate([cr, cc], -1)
        sin[n_ctx:, lead:lead + 2 * h] = np.concatenate([sr, sc], -1)
        return cos, sin

    mc, ms = full(MLA_ROPE // 2, MLA_NOPE, MLA_DK)
    gc, gs = full(GQA_DIM // 2, 0, GQA_DIM)
    reps = GQA_HEADS
    return (jnp.asarray(mc), jnp.asarray(ms),
            jnp.asarray(np.tile(gc, (1, reps))), jnp.asarray(np.tile(gs, (1, reps))))


_NT = (((1,), (1,)), ((), ()))
_TN = (((0,), (0,)), ((), ()))
LOG2E = math.log2(math.e)


def _mla_prep_kernel(p_ref, cos_ref, sin_ref, gq_ref, gkv_ref, wq_ref, wk_ref, wvt_ref, eye_ref,
                     qt_ref, k_ref, vt_ref):
    p = p_ref[0].astype(F32)
    cos, sin = cos_ref[...], sin_ref[...]
    ql = p[:, 0:256]
    rq = lax.rsqrt(jnp.sum(ql * ql, axis=-1, keepdims=True) * (1.0 / MLA_Q_RANK) + EPS)
    qn = (ql * rq * gq_ref[...]).astype(BF16)
    kvl = p[:, 256:384]
    rk = lax.rsqrt(jnp.mean(kvl * kvl, axis=-1, keepdims=True) + EPS)
    kvn = (kvl * rk * gkv_ref[...]).astype(BF16)
    k_rope = _rope(p[:, 384:512], cos, sin, MLA_ROPE // 4)
    scale = (MLA_NOPE + MLA_ROPE) ** -0.5 * LOG2E
    for h in range(MLA_HEADS):
        q = jnp.dot(qn, wq_ref[h], preferred_element_type=F32)
        q = (_rope(q, cos, sin, MLA_ROPE // 4) * scale).astype(BF16)
        qt_ref[0, h] = lax.dot_general(eye_ref[...], q, _NT, preferred_element_type=F32).astype(BF16)
        k = jnp.dot(kvn, wk_ref[h], preferred_element_type=F32) + k_rope
        k_ref[0, h] = k.astype(BF16)
        vt_ref[0, h, 0, 0:MLA_V] = lax.dot_general(wvt_ref[h], kvn, _NT, preferred_element_type=F32).astype(BF16)
        vt_ref[0, h, 0, MLA_V:MLA_V + ONES_ROWS] = jnp.ones((ONES_ROWS, kvn.shape[0]), BF16)


def _mla_prep(p_mla, cos, sin, g_cq, g_ckv, wq, wk, wvt):
    b, lt, _ = p_mla.shape
    tm = ROW_TILE
    const2 = lambda bi, i: (0, 0)
    const3 = lambda bi, i: (0, 0, 0)
    return pl.pallas_call(
        _mla_prep_kernel,
        grid=(b, lt // tm),
        in_specs=[pl.BlockSpec((1, tm, 512), lambda bi, i: (bi, i, 0)),
                  pl.BlockSpec((tm, MLA_DK), lambda bi, i: (i, 0)),
                  pl.BlockSpec((tm, MLA_DK), lambda bi, i: (i, 0)),
                  pl.BlockSpec((1, 256), const2),
                  pl.BlockSpec((1, MLA_KV_RANK), const2),
                  pl.BlockSpec((MLA_HEADS, 256, MLA_DK), const3),
                  pl.BlockSpec((MLA_HEADS, MLA_KV_RANK, MLA_DK), const3),
                  pl.BlockSpec((MLA_HEADS, MLA_V, MLA_KV_RANK), const3),
                  pl.BlockSpec((MLA_DK, MLA_DK), const2)],
        out_specs=[pl.BlockSpec((1, MLA_HEADS, MLA_DK, tm), lambda bi, i: (bi, 0, 0, i)),
                   pl.BlockSpec((1, MLA_HEADS, tm, MLA_DK), lambda bi, i: (bi, 0, i, 0)),
                   pl.BlockSpec((1, MLA_HEADS, 1, MLA_V + ONES_ROWS, tm), lambda bi, i: (bi, 0, i, 0, 0))],
        out_shape=[jax.ShapeDtypeStruct((b, MLA_HEADS, MLA_DK, lt), BF16),
                   jax.ShapeDtypeStruct((b, MLA_HEADS, lt, MLA_DK), BF16),
                   jax.ShapeDtypeStruct((b, MLA_HEADS, lt // tm, MLA_V + ONES_ROWS, tm), BF16)],
        compiler_params=_cparams("parallel", "parallel"),
        name="mla_prep",
    )(p_mla, cos, sin, g_cq, g_ckv, wq, wk, wvt, jnp.eye(MLA_DK, dtype=BF16))


def _head_mean_sq(x, ones_bd):
    sq = x * x
    hi = sq.astype(BF16)
    lo = (sq - hi.astype(F32)).astype(BF16)
    s = jnp.dot(hi, ones_bd, preferred_element_type=F32) + jnp.dot(lo, ones_bd, preferred_element_type=F32)
    return s * (1.0 / GQA_DIM)


def _gqa_prep_kernel(p_ref, cos_ref, sin_ref, gq_ref, gk_ref, ones_ref, eye_ref, qt_ref, k_ref, vt_ref):
    p = p_ref[0].astype(F32)
    cos, sin = cos_ref[...], sin_ref[...]
    nq = GQA_HEADS * GQA_DIM
    nk = GQA_KV_HEADS * GQA_DIM
    q = p[:, 0:nq]
    qn = q * lax.rsqrt(_head_mean_sq(q, ones_ref[...]) + EPS) * gq_ref[...]
    qr = (_rope(qn, cos, sin, GQA_DIM // 4) * (GQA_DIM ** -0.5 * LOG2E)).astype(BF16)
    k = p[:, nq:nq + nk]
    kn = k * lax.rsqrt(_head_mean_sq(k, ones_ref[0:nk, 0:nk]) + EPS) * gk_ref[...]
    kr = _rope(kn, cos[:, 0:nk], sin[:, 0:nk], GQA_DIM // 4)
    v = p_ref[0, :, nq + nk:nq + 2 * nk]
    for h in range(GQA_HEADS):
        sel = eye_ref[h * GQA_DIM:(h + 1) * GQA_DIM, :]
        qt_ref[0, h] = lax.dot_general(sel, qr, _NT, preferred_element_type=F32).astype(BF16)
    for h in range(GQA_KV_HEADS):
        k_ref[0, h] = kr[:, h * GQA_DIM:(h + 1) * GQA_DIM].astype(BF16)
        sel = eye_ref[h * GQA_DIM:(h + 1) * GQA_DIM, 0:nk]
        vt_ref[0, h, 0, 0:GQA_DIM] = lax.dot_general(sel, v, _NT, preferred_element_type=F32).astype(BF16)
        vt_ref[0, h, 0, GQA_DIM:GQA_DIM + ONES_ROWS] = jnp.ones((ONES_ROWS, v.shape[0]), BF16)


def _gqa_prep(p_gqa, cos, sin, g_q, g_k, ones_bd):
    b, lt, _ = p_gqa.shape
    tm = ROW_TILE
    nq = GQA_HEADS * GQA_DIM
    nk = GQA_KV_HEADS * GQA_DIM
    const2 = lambda bi, i: (0, 0)
    return pl.pallas_call(
        _gqa_prep_kernel,
        grid=(b, lt // tm),
        in_specs=[pl.BlockSpec((1, tm, 512), lambda bi, i: (bi, i, 0)),
                  pl.BlockSpec((tm, nq), lambda bi, i: (i, 0)),
                  pl.BlockSpec((tm, nq), lambda bi, i: (i, 0)),
                  pl.BlockSpec((1, nq), const2),
                  pl.BlockSpec((1, nk), const2),
                  pl.BlockSpec((nq, nq), const2),
                  pl.BlockSpec((nq, nq), const2)],
        out_specs=[pl.BlockSpec((1, GQA_HEADS, GQA_DIM, tm), lambda bi, i: (bi, 0, 0, i)),
                   pl.BlockSpec((1, GQA_KV_HEADS, tm, GQA_DIM), lambda bi, i: (bi, 0, i, 0)),
                   pl.BlockSpec((1, GQA_KV_HEADS, 1, GQA_DIM + ONES_ROWS, tm), lambda bi, i: (bi, 0, i, 0, 0))],
        out_shape=[jax.ShapeDtypeStruct((b, GQA_HEADS, GQA_DIM, lt), BF16),
                   jax.ShapeDtypeStruct((b, GQA_KV_HEADS, lt, GQA_DIM), BF16),
                   jax.ShapeDtypeStruct((b, GQA_KV_HEADS, lt // tm, GQA_DIM + ONES_ROWS, tm), BF16)],
        compiler_params=_cparams("parallel", "parallel"),
        name="gqa_prep",
    )(p_gqa, cos, sin, g_q, g_k, ones_bd, jnp.eye(nq, dtype=BF16))


def _attn_kernel(qt_ref, k_ref, vt_ref, gate_ref, place_ref, o_ref, sa_ref, sb_ref, pa_ref, pb_ref, *,
                 heads, group, nct_q, ctx_chunks, all_chunks, unroll):
    i = pl.program_id(1)
    n_loops = jnp.where(i < nct_q, (ctx_chunks - 1) // unroll, (all_chunks - 1) // unroll)
    tq = qt_ref.shape[3]
    dv = place_ref.shape[1]
    dva = vt_ref.shape[3]

    def score(j, h, dst):
        rows = pl.ds(pl.multiple_of(j * ATT_TK, ATT_TK), ATT_TK)
        s = jnp.dot(k_ref[0, h // group, rows, :], qt_ref[0, h], preferred_element_type=F32)
        dst[h] = s
        return jnp.max(s, axis=0, keepdims=True)

    def value(j, h, p_ref):
        return jnp.dot(vt_ref[0, h // group, j], p_ref[h], preferred_element_type=F32)

    def step(j, carry, s_cur, s_nxt, p_prev, p_cur):
        state, cmax = carry
        new, nmax = [], []
        for h in range(heads):
            pv = value(jnp.maximum(j - 1, 0), h, p_prev)
            if s_nxt is not None:
                nmax.append(score(j + 1, h, s_nxt))
            m, acc = state[h]
            m_new = jnp.maximum(m, cmax[h])
            alpha = jnp.exp2(m - m_new)
            p_cur[h] = jnp.exp2(s_cur[h] - m_new).astype(BF16)
            new.append((m_new, alpha * (acc + pv)))
        return tuple(new), tuple(nmax)

    bufs = ((sa_ref, sb_ref, pb_ref, pa_ref), (sb_ref, sa_ref, pa_ref, pb_ref))

    def body(t, carry):
        for u in range(unroll):
            carry = step(unroll * t + u, carry, *bufs[u % 2])
        return carry

    cmax0 = tuple(score(0, h, sa_ref) for h in range(heads))
    pb_ref[...] = jnp.zeros_like(pb_ref)
    init = tuple((jnp.full((1, tq), -jnp.inf, F32), jnp.zeros((dva, tq), F32)) for _ in range(heads))
    carry = lax.fori_loop(0, n_loops, body, (init, cmax0))
    last = unroll * n_loops
    res, _ = step(last, carry, sa_ref, None, pb_ref, pa_ref)
    out = jnp.zeros((tq, GROUP_W), F32)
    for h in range(heads):
        acc = res[h][1] + value(last, h, pa_ref)
        o = (acc[0:dv] * (1.0 / acc[dv:dv + 1])).astype(BF16)
        out = out + lax.dot_general(o, place_ref[h], _TN, preferred_element_type=F32)
    g = gate_ref[0].astype(F32)
    o_ref[0] = (out * _silu(g)).astype(BF16)


def _attention(qt, k, vt, gate_src, gate_block, n_ctx):
    b, heads, dk, lt = qt.shape
    hk, n_chunks, dva = k.shape[1], vt.shape[2], vt.shape[3]
    dv = dva - ONES_ROWS
    place = np.zeros((heads, dv, GROUP_W), np.float32)
    for h in range(heads):
        place[h, np.arange(dv), h * dv + np.arange(dv)] = 1.0
    ctx_chunks, all_chunks = n_ctx // ATT_TK, lt // ATT_TK
    unroll = next(u for u in ATT_UNROLLS if (ctx_chunks - 1) % u == 0 and (all_chunks - 1) % u == 0)
    kern = functools.partial(_attn_kernel, heads=heads, group=heads // hk, nct_q=n_ctx // ATT_TQ,
                             ctx_chunks=ctx_chunks, all_chunks=all_chunks, unroll=unroll)
    return pl.pallas_call(
        kern,
        grid=(b, lt // ATT_TQ),
        in_specs=[pl.BlockSpec((1, heads, dk, ATT_TQ), lambda bi, i: (bi, 0, 0, i)),
                  pl.BlockSpec((1, hk, lt, dk), lambda bi, i: (bi, 0, 0, 0)),
                  pl.BlockSpec((1, hk, n_chunks, dva, ATT_TK), lambda bi, i: (bi, 0, 0, 0, 0)),
                  pl.BlockSpec((1, ATT_TQ, GROUP_W), lambda bi, i: (bi, i, gate_block)),
                  pl.BlockSpec((heads, dv, GROUP_W), lambda bi, i: (0, 0, 0))],
        out_specs=pl.BlockSpec((1, ATT_TQ, GROUP_W), lambda bi, i: (bi, i, 0)),
        out_shape=jax.ShapeDtypeStruct((b, lt, GROUP_W), BF16),
        scratch_shapes=[pltpu.VMEM((heads, ATT_TK, ATT_TQ), F32), pltpu.VMEM((heads, ATT_TK, ATT_TQ), F32),
                        pltpu.VMEM((heads, ATT_TK, ATT_TQ), BF16), pltpu.VMEM((heads, ATT_TK, ATT_TQ), BF16)],
        compiler_params=_cparams("parallel", "arbitrary"),
        name="attention",
    )(qt, k, vt, gate_src, jnp.asarray(place, BF16))


def _ssm_kernel(*refs, steps, nb, reverse, finish):
    if finish:
        u_ref, b_ref, a_ref, c_ref, yf_ref, d_ref, gw_ref, gb_ref, o_ref, st, car = refs
    else:
        u_ref, b_ref, a_ref, c_ref, o_ref, st, car = refs

    @pl.when(pl.program_id(0) == 0)
    def _():
        car[...] = jnp.zeros_like(car)

    half = st.shape[0] // 2
    halves = (slice(0, half), slice(half, 2 * half))
    for hs in halves:
        st[hs] = jnp.dot(u_ref[hs, 0:GROUP_W], b_ref[...], preferred_element_type=F32)
    n_state = st.shape[1] // 2
    lw = min(SSM_LANE_SPLIT, n_state)
    for c0 in range(0, n_state, lw):
        lr = slice(c0, c0 + lw)
        li = slice(n_state + c0, n_state + c0 + lw)
        ar = jnp.broadcast_to(a_ref[:, lr], (nb, lw))
        ai = jnp.broadcast_to(a_ref[:, li], (nb, lw))

        def body(tt, carry, lr=lr, li=li, ar=ar, ai=ai):
            sr, si = carry
            t = (steps - 1 - tt) if reverse else tt
            rows = pl.ds(pl.multiple_of(t * nb, nb), nb)
            nr = ar * sr - ai * si + st[rows, lr]
            ni = ar * si + ai * sr + st[rows, li]
            st[rows, lr] = nr
            st[rows, li] = ni
            return nr, ni

        sr, si = lax.fori_loop(0, steps, body, (car[:, lr], car[:, li]), unroll=2)
        car[:, lr] = sr
        car[:, li] = si
    ys = [jnp.dot(st[hs].astype(BF16), c_ref[...], preferred_element_type=F32) for hs in halves]
    if not finish:
        for hs, y in zip(halves, ys):
            o_ref[hs] = y
    else:
        zs = [jax.nn.gelu(y + yf_ref[hs] + d_ref[...] * u_ref[hs, 0:GROUP_W].astype(F32), approximate=True)
              for hs, y in zip(halves, ys)]
        gls = [jnp.dot(z.astype(BF16), gw_ref[...], preferred_element_type=F32) + gb_ref[...] for z in zs]
        for hs, z, gl in zip(halves, zs, gls):
            gate = u_ref[hs, GROUP_W:2 * GROUP_W].astype(F32)
            o_ref[hs] = (z * jax.nn.sigmoid(gl) * _silu(gate)).astype(BF16)


def _ssm_direction(u_tm, mats, nb, n_ctx_chunks, reverse, finish_args=None):
    rows, _ = u_tm.shape
    steps = SSM_T
    blk = steps * nb
    n_chunks = rows // blk
    n_state2 = mats[0].shape[1]
    if reverse:
        cidx = lambda i: (jnp.where(i < n_ctx_chunks, n_ctx_chunks - 1 - i, n_chunks - 1 - (i - n_ctx_chunks)), 0)
    else:
        cidx = lambda i: (i, 0)
    const = lambda i: (0, 0)
    in_specs = [pl.BlockSpec((blk, SSM_PACK), cidx)] + [pl.BlockSpec(m.shape, const) for m in mats]
    args = [u_tm, *mats]
    finish = finish_args is not None
    if finish:
        yf, d_skip, glu_w, glu_b = finish_args
        in_specs += [pl.BlockSpec((blk, GROUP_W), cidx), pl.BlockSpec(d_skip.shape, const),
                     pl.BlockSpec(glu_w.shape, const), pl.BlockSpec(glu_b.shape, const)]
        args += [yf, d_skip, glu_w, glu_b]
    return pl.pallas_call(
        functools.partial(_ssm_kernel, steps=steps, nb=nb, reverse=reverse, finish=finish),
        grid=(n_chunks,),
        in_specs=in_specs,
        out_specs=pl.BlockSpec((blk, GROUP_W), cidx),
        out_shape=jax.ShapeDtypeStruct((rows, GROUP_W), BF16 if finish else F32),
        scratch_shapes=[pltpu.VMEM((blk, n_state2), F32), pltpu.VMEM((nb, n_state2), F32)],
        compiler_params=_cparams("arbitrary"),
        name="ssm_rev" if reverse else "ssm_fwd",
    )(*args)


def _ssm_matrices(lam_re, lam_im, log_step, b_re, b_im, c_re, c_im):
    lam = lax.complex(lam_re.astype(F32), lam_im.astype(F32))
    step = jnp.exp(log_step.astype(F32))[:, None]
    a_bar = jnp.exp(lam * step)
    b_bar = ((a_bar - 1.0) / lam)[..., None] * lax.complex(b_re.astype(F32), b_im.astype(F32))
    eye = jnp.eye(SSM_GROUPS, dtype=F32)
    n_in = SSM_GROUPS * SSM_GROUP
    n_state = SSM_GROUPS * SSM_STATE

    def drive(m):
        return jnp.einsum('gph,gk->ghkp', m, eye).reshape(n_in, n_state).astype(BF16)

    def readout(m):
        return jnp.einsum('ghp,gk->gpkh', m, eye).reshape(n_state, n_in).astype(BF16)

    return (jnp.concatenate([drive(jnp.real(b_bar)), drive(jnp.imag(b_bar))], axis=1),
            jnp.concatenate([jnp.real(a_bar).reshape(1, n_state), jnp.imag(a_bar).reshape(1, n_state)], axis=1),
            jnp.concatenate([readout(c_re.astype(F32)), readout(-c_im.astype(F32))], axis=0))


def _hy_pre_kernel(x_ref, prev_ref, next_ref, w_ref, b_ref, v_ref, x1_ref, x2_ref, sg_ref, *, n_tiles):
    i = pl.program_id(1)
    nconv = 3 * HY_W
    x = x_ref[0, :, 0:nconv].astype(F32)
    tm = x.shape[0]
    prev_row = jnp.where(i > 0, prev_ref[0, 7:8, 0:nconv].astype(F32), 0.0)
    next_row = jnp.where(i < n_tiles - 1, next_ref[0, 0:1, 0:nconv].astype(F32), 0.0)
    rid = lax.broadcasted_iota(jnp.int32, x.shape, 0)
    xm = jnp.where(rid == 0, prev_row, pltpu.roll(x, 1, 0))
    xp = jnp.where(rid == tm - 1, next_row, pltpu.roll(x, tm - 1, 0))
    proj = xm * w_ref[0:1] + x * w_ref[1:2] + xp * w_ref[2:3] + b_ref[...]
    gate = x_ref[0, :, nconv:nconv + HY_W].astype(F32)
    for o_ref, val in ((v_ref, proj[:, 0:HY_W]), (x1_ref, proj[:, HY_W:2 * HY_W]),
                       (x2_ref, proj[:, 2 * HY_W:3 * HY_W]), (sg_ref, _silu(gate))):
        for hf in range(HY_W // LANES):
            o_ref[0, hf] = val[:, hf * LANES:(hf + 1) * LANES].astype(BF16)


def _hy_pre(p_hy, conv_w, conv_b, tile0, n_tiles):
    b = p_hy.shape[0]
    tm = ROW_TILE
    sub = tm // 8
    n = n_tiles * tm
    last_sub = p_hy.shape[1] // 8 - 1
    halves = HY_W // LANES
    const2 = lambda bi, i: (0, 0)
    out_spec = pl.BlockSpec((1, halves, tm, LANES), lambda bi, i: (bi, 0, i, 0))
    out_shape = jax.ShapeDtypeStruct((b, halves, n, LANES), BF16)
    return pl.pallas_call(
        functools.partial(_hy_pre_kernel, n_tiles=n_tiles),
        grid=(b, n_tiles),
        in_specs=[pl.BlockSpec((1, tm, HY_PACK), lambda bi, i: (bi, i + tile0, 0)),
                  pl.BlockSpec((1, 8, HY_PACK), lambda bi, i: (bi, jnp.maximum((i + tile0) * sub - 1, 0), 0)),
                  pl.BlockSpec((1, 8, HY_PACK), lambda bi, i: (bi, jnp.minimum((i + tile0 + 1) * sub, last_sub), 0)),
                  pl.BlockSpec((3, 3 * HY_W), const2),
                  pl.BlockSpec((1, 3 * HY_W), const2)],
        out_specs=[out_spec] * 4,
        out_shape=[out_shape] * 4,
        compiler_params=_cparams("parallel", "parallel"),
        name="hyena_pre",
    )(p_hy, p_hy, p_hy, conv_w, conv_b)


def _bitrev(p, bits):
    r = 0
    for i in range(bits):
        r = (r << 1) | ((p >> i) & 1)
    return r


@functools.lru_cache(maxsize=None)
def _fft_plan(n):
    r = FFT_R
    nb = n // r
    m = 2 * nb
    big = 2 * n
    bits = int(round(math.log2(nb)))
    assert nb >= 2 and (1 << bits) == nb
    jmap = np.zeros(m, np.int64)
    for p in range(nb):
        jmap[p] = 2 * _bitrev(p, bits)
        jmap[nb + p] = 2 * _bitrev(p, bits) + 1
    k2 = np.arange(r)
    n2 = np.arange(r)
    g = np.zeros((m, 2 * r, 2 * r), np.float32)
    for s in range(m):
        ang = -2.0 * np.pi * (np.outer(k2, n2) / r + np.outer(np.ones(r), n2) * jmap[s] / big)
        gr, gi = np.cos(ang), np.sin(ang)
        g[s] = np.block([[gr, -gi], [gi, gr]])
    half0 = nb // 2
    k = np.arange(half0)
    wnb = np.exp(-2j * np.pi * k / nb)
    wm0 = np.exp(-2j * np.pi * k / m)
    wm1 = np.exp(-2j * np.pi * (k + half0) / m)
    tw_edge = np.stack([wnb.real, wnb.imag, wm0.real, wm0.imag, wm1.real, wm1.imag]).astype(np.float32)
    mids = []
    h = half0 // 2
    while h >= 1:
        kk = np.arange(nb // 2) % h
        w = np.exp(-2j * np.pi * kk / (2 * h))
        mids.append(np.stack([w.real, w.imag]))
        h //= 2
    tw_mid = (np.concatenate(mids, 0) if mids else np.zeros((2, max(nb // 2, 1)))).astype(np.float32)
    freq = (jmap[:, None] + m * k2[None, :])
    return dict(nb=nb, m=m, g=g, tw_edge=tw_edge, tw_mid=tw_mid, n_mid=len(mids), freq=freq)


def _cmul(ar, ai, wr, wi):
    return ar * wr - ai * wi, ar * wi + ai * wr


def _hy_conv_kernel(*refs, nb, n_mid, n_post):
    tw_edge, tw_mid, a_ref, bias_ref, kh_ref, g_ref, gi_ref = refs[:7]
    post_refs = refs[7:7 + n_post]
    o_ref, w = refs[7 + n_post], refs[8 + n_post]
    r = FFT_R
    m = 2 * nb
    half0 = nb // 2
    re, im = slice(0, r), slice(r, 2 * r)

    def rows(k):
        return pl.ds(pl.multiple_of(k * r, r), r)

    def first(k, c):
        wr, wi = tw_edge[0, k], tw_edge[1, k]
        ar, ai = a_ref[0, 0, rows(k), :].astype(F32), a_ref[1, 0, rows(k), :].astype(F32)
        br, bi = a_ref[0, 0, rows(k + half0), :].astype(F32), a_ref[1, 0, rows(k + half0), :].astype(F32)
        w[k, re], w[k, im] = ar + br, ai + bi
        dr, di = _cmul(ar - br, ai - bi, wr, wi)
        w[k + half0, re], w[k + half0, im] = dr, di
        a2r, a2i = _cmul(ar, ai, tw_edge[2, k], tw_edge[3, k])
        b2r, b2i = _cmul(br, bi, tw_edge[4, k], tw_edge[5, k])
        w[nb + k, re], w[nb + k, im] = a2r + b2r, a2i + b2i
        dr, di = _cmul(a2r - b2r, a2i - b2i, wr, wi)
        w[nb + k + half0, re], w[nb + k + half0, im] = dr, di
        return c

    lax.fori_loop(0, half0, first, 0)

    def slot_pair(bb, lg):
        branch = bb // half0
        bf = bb - branch * half0
        i0 = branch * nb + ((bf >> lg) << (lg + 1)) + (bf & ((1 << lg) - 1))
        return bf, i0, i0 + (1 << lg)

    for s in range(n_mid):
        lg = int(round(math.log2(half0))) - 1 - s

        def dif(bb, c, s=s, lg=lg):
            bf, i0, i1 = slot_pair(bb, lg)
            wr, wi = tw_mid[2 * s, bf], tw_mid[2 * s + 1, bf]
            ar, ai, br, bi = w[i0, re], w[i0, im], w[i1, re], w[i1, im]
            w[i0, re], w[i0, im] = ar + br, ai + bi
            dr, di = _cmul(ar - br, ai - bi, wr, wi)
            w[i1, re], w[i1, im] = dr, di
            return c

        lax.fori_loop(0, nb, dif, 0)

    def spectral(t, c):
        slots = [FFT_SLOTS * t + u for u in range(FFT_SLOTS)]
        xs = [jnp.dot(g_ref[s], w[s].astype(BF16), preferred_element_type=F32) for s in slots]
        outs = []
        for s, x in zip(slots, xs):
            yr, yi = _cmul(x[re], x[im], kh_ref[0, s, re, :], kh_ref[0, s, im, :])
            y = jnp.concatenate([yr, yi], axis=0).astype(BF16)
            outs.append(jnp.dot(gi_ref[s], y, preferred_element_type=F32))
        for s, o in zip(slots, outs):
            w[s] = o
        return c

    lax.fori_loop(0, m // FFT_SLOTS, spectral, 0)

    for s in reversed(range(n_mid)):
        lg = int(round(math.log2(half0))) - 1 - s

        def dit(bb, c, s=s, lg=lg):
            bf, i0, i1 = slot_pair(bb, lg)
            wr, wi = tw_mid[2 * s, bf], -tw_mid[2 * s + 1, bf]
            ar, ai = w[i0, re], w[i0, im]
            br, bi = _cmul(w[i1, re], w[i1, im], wr, wi)
            w[i0, re], w[i0, im] = ar + br, ai + bi
            w[i1, re], w[i1, im] = ar - br, ai - bi
            return c

        lax.fori_loop(0, nb, dit, 0)

    bias = bias_ref[0]

    def emit(k, yr, yi):
        for bsel, y in ((0, yr), (1, yi)):
            a = a_ref[bsel, 0, rows(k), :].astype(F32)
            val = y + bias * a
            for p_ref in post_refs:
                val = val * p_ref[bsel, 0, rows(k), :].astype(F32)
            o_ref[bsel, 0, rows(k), :] = val.astype(o_ref.dtype)

    def last(k, c):
        wr, wi = tw_edge[0, k], -tw_edge[1, k]
        ar, ai = w[k, re], w[k, im]
        br, bi = _cmul(w[k + half0, re], w[k + half0, im], wr, wi)
        cr, ci = w[nb + k, re], w[nb + k, im]
        dr, di = _cmul(w[nb + k + half0, re], w[nb + k + half0, im], wr, wi)
        o0r, o0i = _cmul(cr + dr, ci + di, tw_edge[2, k], -tw_edge[3, k])
        o1r, o1i = _cmul(cr - dr, ci - di, tw_edge[4, k], -tw_edge[5, k])
        emit(k, ar + br + o0r, ai + bi + o0i)
        emit(k + half0, ar - br + o1r, ai - bi + o1i)
        return c

    lax.fori_loop(0, half0, last, 0)


def _hy_conv(a, khat, bias, posts, out_dtype):
    b, halves, n, lanes = a.shape
    plan = _fft_plan(n)
    nb, m = plan['nb'], plan['m']
    r = FFT_R
    sig = pl.BlockSpec((2, 1, n, lanes), lambda hf, pr: (pr, hf, 0, 0))
    smem = pl.BlockSpec(memory_space=pltpu.SMEM)
    once = pl.Buffered(1)
    in_specs = [smem, smem, sig,
                pl.BlockSpec((1, 1, lanes), lambda hf, pr: (hf, 0, 0)),
                pl.BlockSpec((1, m, 2 * r, lanes), lambda hf, pr: (hf, 0, 0, 0), pipeline_mode=once),
                pl.BlockSpec((m, 2 * r, 2 * r), lambda hf, pr: (0, 0, 0), pipeline_mode=once),
                pl.BlockSpec((m, 2 * r, 2 * r), lambda hf, pr: (0, 0, 0), pipeline_mode=once)]
    in_specs += [sig] * len(posts)
    g_fwd = jnp.asarray(plan['g'], BF16)
    g_inv = jnp.asarray(np.swapaxes(plan['g'], 1, 2), BF16)
    return pl.pallas_call(
        functools.partial(_hy_conv_kernel, nb=nb, n_mid=plan['n_mid'], n_post=len(posts)),
        grid=(halves, b // 2),
        in_specs=in_specs,
        out_specs=sig,
        out_shape=jax.ShapeDtypeStruct(a.shape, out_dtype),
        scratch_shapes=[pltpu.VMEM((m, 2 * r, lanes), F32)],
        compiler_params=pltpu.CompilerParams(dimension_semantics=("arbitrary", "arbitrary"),
                                             vmem_limit_bytes=HY_VMEM_LIMIT),
        name="hyena_conv",
    )(jnp.asarray(plan['tw_edge']), jnp.asarray(plan['tw_mid']), a, bias, khat, g_fwd, g_inv, *posts)


def _hyena_filter_spectrum(n, w1, b1, fr1, w2, b2, fr2, w3):
    t = jnp.linspace(0.0, 1.0, n, dtype=F32)[:, None]
    omega = 2.0 * math.pi * jnp.arange(n, dtype=F32)[:, None] / n
    bands = jnp.linspace(1e-4, HY_BANDS - 1, HY_BANDS, dtype=F32)[None, :]
    z = jnp.concatenate([t, jnp.cos(bands * omega), -jnp.sin(bands * omega)], axis=-1)
    hp = lax.Precision.HIGHEST
    h = jnp.sin(fr1.astype(F32) * (jnp.dot(z, w1.astype(F32), precision=hp) + b1.astype(F32)))
    h = jnp.sin(fr2.astype(F32) * (jnp.dot(h, w2.astype(F32), precision=hp) + b2.astype(F32)))
    h = jnp.dot(h, w3.astype(F32), precision=hp).reshape(n, HY_ORDER, 2, HY_W)
    max_decay = math.log(HY_DECAY_TARGET) / HY_FAST_DECAY
    min_decay = math.log(HY_DECAY_TARGET) / HY_SLOW_DECAY
    deltas = jnp.linspace(min_decay, max_decay, HY_W, dtype=F32)
    h = h * jnp.exp(-t[:, :, None, None] * jnp.abs(deltas))
    h = h * lax.rsqrt(jnp.sum(h * h, axis=(0, 2), keepdims=True) + EPS)
    fwd, bwd = h[:, :, 0], h[:, :, 1]
    zero = jnp.zeros((1, HY_ORDER, HY_W), F32)
    k_circ = jnp.concatenate([fwd, zero, bwd[1:][::-1]], axis=0)
    spec = jnp.fft.fft(k_circ, axis=0) * (1.0 / (2 * n))
    plan = _fft_plan(n)
    sel = spec[jnp.asarray(plan['freq'])]
    tab = jnp.concatenate([jnp.real(sel), jnp.imag(sel)], axis=1)
    tab = tab.reshape(plan['m'], 2 * FFT_R, HY_ORDER, HY_W // LANES, LANES)
    return jnp.transpose(tab, (2, 3, 0, 1, 4)).astype(F32)


def _pack_w_in(w_in):
    d = w_in.shape[0]
    z = lambda n: jnp.zeros((d, n), w_in.dtype)
    o = 0
    q_lat = w_in[:, o:o + MLA_Q_RANK]; o += MLA_Q_RANK
    kv_lat = w_in[:, o:o + MLA_KV_RANK]; o += MLA_KV_RANK
    k_rope = w_in[:, o:o + MLA_ROPE]; o += MLA_ROPE
    gate = w_in[:, o:o + GROUP_W]; o += GROUP_W
    mla = jnp.concatenate([q_lat, z(256 - MLA_Q_RANK), kv_lat, z(MLA_NOPE), k_rope,
                           z(MLA_DK - MLA_NOPE - MLA_ROPE), gate], axis=1)
    return jnp.concatenate([mla, w_in[:, o:]], axis=1).astype(BF16)


def _pack_mla_up(w_uq, w_ukv):
    dq = MLA_NOPE + MLA_ROPE
    wq = w_uq.reshape(MLA_Q_RANK, MLA_HEADS, dq).transpose(1, 0, 2)
    wq = jnp.pad(wq, ((0, 0), (0, 256 - MLA_Q_RANK), (0, MLA_DK - dq))).astype(BF16)
    wkv = w_ukv.reshape(MLA_KV_RANK, MLA_HEADS, MLA_NOPE + MLA_V).transpose(1, 0, 2)
    wk = jnp.pad(wkv[:, :, :MLA_NOPE], ((0, 0), (0, 0), (0, MLA_DK - MLA_NOPE))).astype(BF16)
    wvt = jnp.swapaxes(wkv[:, :, MLA_NOPE:], 1, 2).astype(BF16)
    return wq, wk, wvt


def _merge_halves(y_ctx, y_lat):
    y = jnp.concatenate([y_ctx, y_lat], axis=2)
    b, halves, lt, lanes = y.shape
    return jnp.transpose(y, (0, 2, 1, 3)).reshape(b, lt, halves * lanes)


def kernel(x, c, ctx, c_ctx, w_mod, b_mod, g_pre, g_post, w_in, w_out, mla_g_cq, mla_w_uq, mla_g_ckv, mla_w_ukv, gqa_g_q, gqa_g_k, ssm_lambda_re, ssm_lambda_im, ssm_log_step, ssm_b_re, ssm_b_im, ssm_c_re, ssm_c_im, ssm_d, ssm_glu_w, ssm_glu_b, hy_conv_w, hy_conv_b, hy_f_w1, hy_f_b1, hy_f_freq1, hy_f_w2, hy_f_b2, hy_f_freq2, hy_f_w3, hy_bias):
    b, n_lat, d = x.shape
    n_ctx = ctx.shape[1]
    depth = w_in.shape[0]
    lt = n_ctx + n_lat
    nct = n_ctx // ROW_TILE
    assert n_ctx % ROW_TILE == 0 and n_lat % ROW_TILE == 0 and b % 2 == 0 and b % 8 == 0

    n_cond = -(-(b + 1) // 8) * 8
    cond = jnp.zeros((n_cond, d), F32).at[:b].set(c).at[b].set(c_ctx)
    mod = _modulation(cond, w_mod, b_mod)[:, :b + 1]
    shift, scale, gate = mod[..., :d], mod[..., d:2 * d], mod[..., 2 * d:]

    mla_cos, mla_sin, gqa_cos, gqa_sin = _rope_tables(n_ctx, n_lat)
    nq = GQA_HEADS * GQA_DIM
    ones_bd = jnp.asarray(np.kron(np.eye(GQA_HEADS), np.ones((GQA_DIM, GQA_DIM))), BF16)

    xa = jnp.concatenate([ctx, x], axis=1)
    for l in range(depth):
        sc = (g_pre[l][None, :] * (1.0 + scale[l]))[:, None, :]
        sh = shift[l][:, None, :]
        p_mla, p_gqa, p_ssm, p_hy = _inproj(xa, sc, sh, _pack_w_in(w_in[l]), nct)

        wq, wk, wvt = _pack_mla_up(mla_w_uq[l], mla_w_ukv[l])
        g_cq = jnp.pad(mla_g_cq[l], (0, 256 - MLA_Q_RANK)).reshape(1, 256)
        qt, k, vt = _mla_prep(p_mla, mla_cos, mla_sin, g_cq, mla_g_ckv[l].reshape(1, -1), wq, wk, wvt)
        a_out = _attention(qt, k, vt, p_mla, 2, n_ctx)

        qt, k, vt = _gqa_prep(p_gqa, gqa_cos, gqa_sin, jnp.tile(gqa_g_q[l], GQA_HEADS).reshape(1, nq),
                              jnp.tile(gqa_g_k[l], GQA_KV_HEADS).reshape(1, -1), ones_bd)
        g_out = _attention(qt, k, vt, p_gqa, 2, n_ctx)

        u_tm = jnp.transpose(p_ssm, (1, 0, 2)).reshape(lt * b, SSM_PACK)
        mats = [_ssm_matrices(ssm_lambda_re[l, di], ssm_lambda_im[l, di], ssm_log_step[l, di],
                              ssm_b_re[l, di], ssm_b_im[l, di], ssm_c_re[l, di], ssm_c_im[l, di])
                for di in range(2)]
        n_ctx_chunks = n_ctx // SSM_T
        y_f = _ssm_direction(u_tm, mats[0], b, n_ctx_chunks, False)
        fin = (y_f, ssm_d[l].reshape(1, -1), ssm_glu_w[l].astype(BF16), ssm_glu_b[l].reshape(1, -1))
        s_tm = _ssm_direction(u_tm, mats[1], b, n_ctx_chunks, True, fin)
        s_out = jnp.transpose(s_tm.reshape(lt, b, GROUP_W), (1, 0, 2))

        filt = (hy_f_w1[l], hy_f_b1[l], hy_f_freq1[l], hy_f_w2[l], hy_f_b2[l], hy_f_freq2[l], hy_f_w3[l])
        conv_b = hy_conv_b[l].reshape(1, -1)
        parts = []
        for tile0, n_tiles in ((0, nct), (nct, n_lat // ROW_TILE)):
            n = n_tiles * ROW_TILE
            hv, hx1, hx2, hsg = _hy_pre(p_hy, hy_conv_w[l], conv_b, tile0, n_tiles)
            khat = _hyena_filter_spectrum(n, *filt)
            bias = hy_bias[l].astype(F32).reshape(HY_ORDER, HY_W // LANES, 1, LANES)
            z1 = _hy_conv(hv, khat[0], bias[0], (hx1,), BF16)
            parts.append(_hy_conv(z1, khat[1], bias[1], (hx2, hsg), BF16))
        y_out = _merge_halves(*parts)

        xa = _outproj(a_out, g_out, s_out, y_out, xa, gate[l][:, None, :], g_post[l].reshape(1, d),
                      w_out[l].astype(BF16), nct, tile0=nct if l == depth - 1 else 0)
    return xa
```

```python
import functools
import math

import numpy as np
import jax
import jax.numpy as jnp
from jax import lax
from jax.experimental import pallas as pl
from jax.experimental.pallas import tpu as pltpu

F32 = jnp.float32
BF16 = jnp.bfloat16

GRID_W = 64
ROPE_BASE = 10000.0
EPS = 1e-6
GROUP_W = 256
MLA_HEADS, MLA_NOPE, MLA_ROPE, MLA_V = 4, 64, 32, 64
MLA_Q_RANK, MLA_KV_RANK = 192, 128
GQA_HEADS, GQA_KV_HEADS, GQA_DIM = 4, 2, 64
SSM_GROUPS, SSM_GROUP, SSM_STATE = 16, 16, 64
HY_W, HY_ORDER, HY_EMB, HY_HIDDEN = 256, 2, 33, 64
HY_BANDS = (HY_EMB - 1) // 2
HY_FAST_DECAY, HY_SLOW_DECAY, HY_DECAY_TARGET = 0.3, 1.5, 1e-2

MLA_PACK = 768
GQA_PACK = 768
SSM_PACK = 512
HY_PACK = 1024
IN_PACK = MLA_PACK + GQA_PACK + SSM_PACK + HY_PACK
MLA_DK = 128

LANES = 128
ROW_TILE = 256
ATT_TQ = 256
ATT_TK = 256
ATT_UNROLLS = (8, 4, 2)
ONES_ROWS = 16
SSM_T = 64
SSM_LANE_SPLIT = 512
FFT_R = 128
FFT_SLOTS = 4
VMEM_LIMIT = 52 * 1024 * 1024
HY_VMEM_LIMIT = 58 * 1024 * 1024


def _cparams(*sem):
    return pltpu.CompilerParams(dimension_semantics=sem, vmem_limit_bytes=VMEM_LIMIT)


def _silu(x):
    return x * jax.nn.sigmoid(x)


def _mod_kernel(c_ref, w_ref, b_ref, o_ref):
    c = c_ref[...]
    s = _silu(c).astype(BF16)
    o_ref[0] = jnp.dot(s, w_ref[0].astype(BF16), preferred_element_type=F32) + b_ref[0]


def _modulation(cond, w_mod, b_mod):
    depth, d, n3 = w_mod.shape
    r = cond.shape[0]
    tn = 512
    return pl.pallas_call(
        _mod_kernel,
        grid=(depth, n3 // tn),
        in_specs=[pl.BlockSpec((r, d), lambda l, j: (0, 0)),
                  pl.BlockSpec((1, d, tn), lambda l, j: (l, 0, j)),
                  pl.BlockSpec((1, 1, tn), lambda l, j: (l, 0, j))],
        out_specs=pl.BlockSpec((1, r, tn), lambda l, j: (l, 0, j)),
        out_shape=jax.ShapeDtypeStruct((depth, r, n3), F32),
        compiler_params=_cparams("parallel", "parallel"),
        name="modulation",
    )(cond, w_mod, b_mod.reshape(depth, 1, n3))


def _inproj_kernel(x_ref, sc_ref, sh_ref, w_ref, o_mla, o_gqa, o_ssm, o_hy):
    x = x_ref[0]
    ms = jnp.mean(x * x, axis=-1, keepdims=True)
    h = x * lax.rsqrt(ms + EPS) * sc_ref[0] + sh_ref[0]
    hb = h.astype(BF16)
    c0 = 0
    for o_ref, width in ((o_mla, MLA_PACK), (o_gqa, GQA_PACK), (o_ssm, SSM_PACK), (o_hy, HY_PACK)):
        o_ref[0] = jnp.dot(hb, w_ref[:, c0:c0 + width], preferred_element_type=F32).astype(BF16)
        c0 += width


def _inproj(xa, scale, shift, w, nct):
    b, lt, d = xa.shape
    tm = ROW_TILE
    widths = (MLA_PACK, GQA_PACK, SSM_PACK, HY_PACK)

    def mod_idx(bi, i):
        return (jnp.where(i < nct, b, bi), 0, 0)

    return pl.pallas_call(
        _inproj_kernel,
        grid=(b, lt // tm),
        in_specs=[pl.BlockSpec((1, tm, d), lambda bi, i: (bi, i, 0)),
                  pl.BlockSpec((1, 1, d), mod_idx),
                  pl.BlockSpec((1, 1, d), mod_idx),
                  pl.BlockSpec((d, IN_PACK), lambda bi, i: (0, 0))],
        out_specs=[pl.BlockSpec((1, tm, wd), lambda bi, i: (bi, i, 0)) for wd in widths],
        out_shape=[jax.ShapeDtypeStruct((b, lt, wd), BF16) for wd in widths],
        compiler_params=_cparams("parallel", "parallel"),
        name="inproj",
    )(xa, scale, shift, w)


def _outproj_kernel(a_ref, g_ref, s_ref, y_ref, x_ref, gt_ref, gp_ref, w_ref, o_ref):
    half = a_ref.shape[1] // 2
    for hs in (slice(0, half), slice(half, 2 * half)):
        acc = jnp.dot(a_ref[0, hs], w_ref[0:GROUP_W], preferred_element_type=F32)
        acc += jnp.dot(g_ref[0, hs], w_ref[GROUP_W:2 * GROUP_W], preferred_element_type=F32)
        acc += jnp.dot(s_ref[0, hs], w_ref[2 * GROUP_W:3 * GROUP_W], preferred_element_type=F32)
        acc += jnp.dot(y_ref[0, hs], w_ref[3 * GROUP_W:4 * GROUP_W], preferred_element_type=F32)
        ms = jnp.mean(acc * acc, axis=-1, keepdims=True)
        o_ref[0, hs] = x_ref[0, hs] + gt_ref[0] * (acc * lax.rsqrt(ms + EPS) * gp_ref[...])


def _outproj(a, g, s, y, xa, gate, g_post, w, nct, tile0=0):
    b, lt, d = xa.shape
    tm = ROW_TILE

    def mod_idx(bi, i):
        return (jnp.where(i + tile0 < nct, b, bi), 0, 0)

    row = lambda bi, i: (bi, i + tile0, 0)
    return pl.pallas_call(
        _outproj_kernel,
        grid=(b, lt // tm - tile0),
        in_specs=[pl.BlockSpec((1, tm, GROUP_W), row)] * 4 + [
            pl.BlockSpec((1, tm, d), row),
            pl.BlockSpec((1, 1, d), mod_idx),
            pl.BlockSpec((1, d), lambda bi, i: (0, 0)),
            pl.BlockSpec((4 * GROUP_W, d), lambda bi, i: (0, 0))],
        out_specs=pl.BlockSpec((1, tm, d), lambda bi, i: (bi, i, 0)),
        out_shape=jax.ShapeDtypeStruct((b, lt - tile0 * tm, d), F32),
        compiler_params=_cparams("parallel", "parallel"),
        name="outproj",
    )(a, g, s, y, xa, gate, g_post, w)


def _rope(x, cos, sin, shift):
    w = x.shape[-1]
    lane = lax.broadcasted_iota(jnp.int32, x.shape, 1)
    first = (lane & shift) == 0
    swapped = jnp.where(first, -pltpu.roll(x, w - shift, 1), pltpu.roll(x, shift, 1))
    return x * cos + swapped * sin


def _rope_tables(n_ctx, n_lat):
    t = np.arange(n_lat)
    row = (t // GRID_W).astype(np.float64)
    col = (t % GRID_W).astype(np.float64)

    def block(pos, h):
        inv = ROPE_BASE ** (-np.arange(0, h, 2, dtype=np.float64) / h)
        ang = (pos[:, None].astype(np.float32) * inv[None, :].astype(np.float32)).astype(np.float32)
        c, s = np.cos(ang), np.sin(ang)
        return np.concatenate([c, c], -1), np.concatenate([s, s], -1)

    def full(h, lead, width):
        cr, sr = block(row, h)
        cc, sc = block(col, h)
        cos = np.ones((n_ctx + n_lat, width), np.float32)
        sin = np.zeros((n_ctx + n_lat, width), np.float32)
        cos[n_ctx:, lead:lead + 2 * h] = np.concatenate([cr, cc], -1)
        sin[n_ctx:, lead:lead + 2 * h] = np.concatenate([sr, sc], -1)
        return cos, sin

    mc, ms = full(MLA_ROPE // 2, MLA_NOPE, MLA_DK)
    gc, gs = full(GQA_DIM // 2, 0, GQA_DIM)
    reps = GQA_HEADS
    return (jnp.asarray(mc), jnp.asarray(ms),
            jnp.asarray(np.tile(gc, (1, reps))), jnp.asarray(np.tile(gs, (1, reps))))


_NT = (((1,), (1,)), ((), ()))
_TN = (((0,), (0,)), ((), ()))
LOG2E = math.log2(math.e)


def _mla_prep_kernel(p_ref, cos_ref, sin_ref, gq_ref, gkv_ref, wq_ref, wk_ref, wvt_ref, eye_ref,
                     qt_ref, k_ref, vt_ref):
    p = p_ref[0].astype(F32)
    cos, sin = cos_ref[...], sin_ref[...]
    ql = p[:, 0:256]
    rq = lax.rsqrt(jnp.sum(ql * ql, axis=-1, keepdims=True) * (1.0 / MLA_Q_RANK) + EPS)
    qn = (ql * rq * gq_ref[...]).astype(BF16)
    kvl = p[:, 256:384]
    rk = lax.rsqrt(jnp.mean(kvl * kvl, axis=-1, keepdims=True) + EPS)
    kvn = (kvl * rk * gkv_ref[...]).astype(BF16)
    k_rope = _rope(p[:, 384:512], cos, sin, MLA_ROPE // 4)
    scale = (MLA_NOPE + MLA_ROPE) ** -0.5 * LOG2E
    heads = range(MLA_HEADS)
    qs = [jnp.dot(qn, wq_ref[h], preferred_element_type=F32) for h in heads]
    ks = [jnp.dot(kvn, wk_ref[h], preferred_element_type=F32) for h in heads]
    vts = [lax.dot_general(wvt_ref[h], kvn, _NT, preferred_element_type=F32) for h in heads]
    for h in heads:
        k_ref[0, h] = (ks[h] + k_rope).astype(BF16)
        vt_ref[0, h, 0, 0:MLA_V] = vts[h].astype(BF16)
        vt_ref[0, h, 0, MLA_V:MLA_V + ONES_ROWS] = jnp.ones((ONES_ROWS, kvn.shape[0]), BF16)
    qr = [(_rope(q, cos, sin, MLA_ROPE // 4) * scale).astype(BF16) for q in qs]
    for h in heads:
        qt_ref[0, h] = lax.dot_general(eye_ref[...], qr[h], _NT, preferred_element_type=F32).astype(BF16)


def _mla_prep(p_mla, cos, sin, g_cq, g_ckv, wq, wk, wvt):
    b, lt, _ = p_mla.shape
    tm = ROW_TILE
    const2 = lambda bi, i: (0, 0)
    const3 = lambda bi, i: (0, 0, 0)
    return pl.pallas_call(
        _mla_prep_kernel,
        grid=(b, lt // tm),
        in_specs=[pl.BlockSpec((1, tm, 512), lambda bi, i: (bi, i, 0)),
                  pl.BlockSpec((tm, MLA_DK), lambda bi, i: (i, 0)),
                  pl.BlockSpec((tm, MLA_DK), lambda bi, i: (i, 0)),
                  pl.BlockSpec((1, 256), const2),
                  pl.BlockSpec((1, MLA_KV_RANK), const2),
                  pl.BlockSpec((MLA_HEADS, 256, MLA_DK), const3),
                  pl.BlockSpec((MLA_HEADS, MLA_KV_RANK, MLA_DK), const3),
                  pl.BlockSpec((MLA_HEADS, MLA_V, MLA_KV_RANK), const3),
                  pl.BlockSpec((MLA_DK, MLA_DK), const2)],
        out_specs=[pl.BlockSpec((1, MLA_HEADS, MLA_DK, tm), lambda bi, i: (bi, 0, 0, i)),
                   pl.BlockSpec((1, MLA_HEADS, tm, MLA_DK), lambda bi, i: (bi, 0, i, 0)),
                   pl.BlockSpec((1, MLA_HEADS, 1, MLA_V + ONES_ROWS, tm), lambda bi, i: (bi, 0, i, 0, 0))],
        out_shape=[jax.ShapeDtypeStruct((b, MLA_HEADS, MLA_DK, lt), BF16),
                   jax.ShapeDtypeStruct((b, MLA_HEADS, lt, MLA_DK), BF16),
                   jax.ShapeDtypeStruct((b, MLA_HEADS, lt // tm, MLA_V + ONES_ROWS, tm), BF16)],
        compiler_params=_cparams("parallel", "parallel"),
        name="mla_prep",
    )(p_mla, cos, sin, g_cq, g_ckv, wq, wk, wvt, jnp.eye(MLA_DK, dtype=BF16))


def _head_mean_sq(x, ones_bd):
    sq = x * x
    hi = sq.astype(BF16)
    lo = (sq - hi.astype(F32)).astype(BF16)
    s = jnp.dot(hi, ones_bd, preferred_element_type=F32) + jnp.dot(lo, ones_bd, preferred_element_type=F32)
    return s * (1.0 / GQA_DIM)


def _gqa_prep_kernel(p_ref, cos_ref, sin_ref, gq_ref, gk_ref, ones_ref, eye_ref, qt_ref, k_ref, vt_ref):
    p = p_ref[0].astype(F32)
    cos, sin = cos_ref[...], sin_ref[...]
    nq = GQA_HEADS * GQA_DIM
    nk = GQA_KV_HEADS * GQA_DIM
    q = p[:, 0:nq]
    k = p[:, nq:nq + nk]
    v = p_ref[0, :, nq + nk:nq + 2 * nk]
    q_ms = _head_mean_sq(q, ones_ref[...])
    k_ms = _head_mean_sq(k, ones_ref[0:nk, 0:nk])
    for h in range(GQA_KV_HEADS):
        sel = eye_ref[h * GQA_DIM:(h + 1) * GQA_DIM, 0:nk]
        vt_ref[0, h, 0, 0:GQA_DIM] = lax.dot_general(sel, v, _NT, preferred_element_type=F32).astype(BF16)
        vt_ref[0, h, 0, GQA_DIM:GQA_DIM + ONES_ROWS] = jnp.ones((ONES_ROWS, v.shape[0]), BF16)
    qn = q * lax.rsqrt(q_ms + EPS) * gq_ref[...]
    qr = (_rope(qn, cos, sin, GQA_DIM // 4) * (GQA_DIM ** -0.5 * LOG2E)).astype(BF16)
    kn = k * lax.rsqrt(k_ms + EPS) * gk_ref[...]
    kr = _rope(kn, cos[:, 0:nk], sin[:, 0:nk], GQA_DIM // 4)
    for h in range(GQA_HEADS):
        sel = eye_ref[h * GQA_DIM:(h + 1) * GQA_DIM, :]
        qt_ref[0, h] = lax.dot_general(sel, qr, _NT, preferred_element_type=F32).astype(BF16)
    for h in range(GQA_KV_HEADS):
        k_ref[0, h] = kr[:, h * GQA_DIM:(h + 1) * GQA_DIM].astype(BF16)


def _gqa_prep(p_gqa, cos, sin, g_q, g_k, ones_bd):
    b, lt, _ = p_gqa.shape
    tm = ROW_TILE
    nq = GQA_HEADS * GQA_DIM
    nk = GQA_KV_HEADS * GQA_DIM
    const2 = lambda bi, i: (0, 0)
    return pl.pallas_call(
        _gqa_prep_kernel,
        grid=(b, lt // tm),
        in_specs=[pl.BlockSpec((1, tm, 512), lambda bi, i: (bi, i, 0)),
                  pl.BlockSpec((tm, nq), lambda bi, i: (i, 0)),
                  pl.BlockSpec((tm, nq), lambda bi, i: (i, 0)),
                  pl.BlockSpec((1, nq), const2),
                  pl.BlockSpec((1, nk), const2),
                  pl.BlockSpec((nq, nq), const2),
                  pl.BlockSpec((nq, nq), const2)],
        out_specs=[pl.BlockSpec((1, GQA_HEADS, GQA_DIM, tm), lambda bi, i: (bi, 0, 0, i)),
                   pl.BlockSpec((1, GQA_KV_HEADS, tm, GQA_DIM), lambda bi, i: (bi, 0, i, 0)),
                   pl.BlockSpec((1, GQA_KV_HEADS, 1, GQA_DIM + ONES_ROWS, tm), lambda bi, i: (bi, 0, i, 0, 0))],
        out_shape=[jax.ShapeDtypeStruct((b, GQA_HEADS, GQA_DIM, lt), BF16),
                   jax.ShapeDtypeStruct((b, GQA_KV_HEADS, lt, GQA_DIM), BF16),
                   jax.ShapeDtypeStruct((b, GQA_KV_HEADS, lt // tm, GQA_DIM + ONES_ROWS, tm), BF16)],
        compiler_params=_cparams("parallel", "parallel"),
        name="gqa_prep",
    )(p_gqa, cos, sin, g_q, g_k, ones_bd, jnp.eye(nq, dtype=BF16))


def _attn_kernel(qt_ref, k_ref, vt_ref, gate_ref, place_ref, o_ref, sa_ref, sb_ref, pa_ref, pb_ref, *,
                 heads, group, nct_q, ctx_chunks, all_chunks, unroll):
    i = pl.program_id(1)
    n_loops = jnp.where(i < nct_q, (ctx_chunks - 1) // unroll, (all_chunks - 1) // unroll)
    tq = qt_ref.shape[3]
    dv = place_ref.shape[1]
    dva = vt_ref.shape[3]

    def score(j, h, dst):
        rows = pl.ds(pl.multiple_of(j * ATT_TK, ATT_TK), ATT_TK)
        s = jnp.dot(k_ref[0, h // group, rows, :], qt_ref[0, h], preferred_element_type=F32)
        dst[h] = s
        return jnp.max(s, axis=0, keepdims=True)

    def value(j, h, p_ref):
        return jnp.dot(vt_ref[0, h // group, j], p_ref[h], preferred_element_type=F32)

    def step(j, carry, s_cur, s_nxt, p_prev, p_cur):
        state, cmax = carry
        new, nmax = [], []
        for h in range(heads):
            pv = value(jnp.maximum(j - 1, 0), h, p_prev)
            if s_nxt is not None:
                nmax.append(score(j + 1, h, s_nxt))
            m, acc = state[h]
            m_new = jnp.maximum(m, cmax[h])
            alpha = jnp.exp2(m - m_new)
            p_cur[h] = jnp.exp2(s_cur[h] - m_new).astype(BF16)
            new.append((m_new, alpha * (acc + pv)))
        return tuple(new), tuple(nmax)

    bufs = ((sa_ref, sb_ref, pb_ref, pa_ref), (sb_ref, sa_ref, pa_ref, pb_ref))

    def body(t, carry):
        for u in range(unroll):
            carry = step(unroll * t + u, carry, *bufs[u % 2])
        return carry

    cmax0 = tuple(score(0, h, sa_ref) for h in range(heads))
    pb_ref[...] = jnp.zeros_like(pb_ref)
    init = tuple((jnp.full((1, tq), -jnp.inf, F32), jnp.zeros((dva, tq), F32)) for _ in range(heads))
    carry = lax.fori_loop(0, n_loops, body, (init, cmax0))
    last = unroll * n_loops
    res, _ = step(last, carry, sa_ref, None, pb_ref, pa_ref)
    out = jnp.zeros((tq, GROUP_W), F32)
    for h in range(heads):
        acc = res[h][1] + value(last, h, pa_ref)
        o = (acc[0:dv] * (1.0 / acc[dv:dv + 1])).astype(BF16)
        out = out + lax.dot_general(o, place_ref[h], _TN, preferred_element_type=F32)
    g = gate_ref[0].astype(F32)
    o_ref[0] = (out * _silu(g)).astype(BF16)


def _attention(qt, k, vt, gate_src, gate_block, n_ctx):
    b, heads, dk, lt = qt.shape
    hk, n_chunks, dva = k.shape[1], vt.shape[2], vt.shape[3]
    dv = dva - ONES_ROWS
    place = np.zeros((heads, dv, GROUP_W), np.float32)
    for h in range(heads):
        place[h, np.arange(dv), h * dv + np.arange(dv)] = 1.0
    ctx_chunks, all_chunks = n_ctx // ATT_TK, lt // ATT_TK
    unroll = next(u for u in ATT_UNROLLS if (ctx_chunks - 1) % u == 0 and (all_chunks - 1) % u == 0)
    kern = functools.partial(_attn_kernel, heads=heads, group=heads // hk, nct_q=n_ctx // ATT_TQ,
                             ctx_chunks=ctx_chunks, all_chunks=all_chunks, unroll=unroll)
    return pl.pallas_call(
        kern,
        grid=(b, lt // ATT_TQ),
        in_specs=[pl.BlockSpec((1, heads, dk, ATT_TQ), lambda bi, i: (bi, 0, 0, i)),
                  pl.BlockSpec((1, hk, lt, dk), lambda bi, i: (bi, 0, 0, 0)),
                  pl.BlockSpec((1, hk, n_chunks, dva, ATT_TK), lambda bi, i: (bi, 0, 0, 0, 0)),
                  pl.BlockSpec((1, ATT_TQ, GROUP_W), lambda bi, i: (bi, i, gate_block)),
                  pl.BlockSpec((heads, dv, GROUP_W), lambda bi, i: (0, 0, 0))],
        out_specs=pl.BlockSpec((1, ATT_TQ, GROUP_W), lambda bi, i: (bi, i, 0)),
        out_shape=jax.ShapeDtypeStruct((b, lt, GROUP_W), BF16),
        scratch_shapes=[pltpu.VMEM((heads, ATT_TK, ATT_TQ), F32), pltpu.VMEM((heads, ATT_TK, ATT_TQ), F32),
                        pltpu.VMEM((heads, ATT_TK, ATT_TQ), BF16), pltpu.VMEM((heads, ATT_TK, ATT_TQ), BF16)],
        compiler_params=_cparams("parallel", "arbitrary"),
        name="attention",
    )(qt, k, vt, gate_src, jnp.asarray(place, BF16))


def _ssm_kernel(*refs, steps, nb, reverse, finish):
    if finish:
        u_ref, b_ref, a_ref, c_ref, yf_ref, d_ref, gw_ref, gb_ref, o_ref, st, car = refs
    else:
        u_ref, b_ref, a_ref, c_ref, o_ref, st, car = refs

    @pl.when(pl.program_id(0) == 0)
    def _():
        car[...] = jnp.zeros_like(car)

    half = st.shape[0] // 2
    halves = (slice(0, half), slice(half, 2 * half))
    for hs in halves:
        st[hs] = jnp.dot(u_ref[hs, 0:GROUP_W], b_ref[...], preferred_element_type=F32)
    n_state = st.shape[1] // 2
    lw = min(SSM_LANE_SPLIT, n_state)
    for c0 in range(0, n_state, lw):
        lr = slice(c0, c0 + lw)
        li = slice(n_state + c0, n_state + c0 + lw)
        ar = jnp.broadcast_to(a_ref[:, lr], (nb, lw))
        ai = jnp.broadcast_to(a_ref[:, li], (nb, lw))

        def body(tt, carry, lr=lr, li=li, ar=ar, ai=ai):
            sr, si = carry
            t = (steps - 1 - tt) if reverse else tt
            rows = pl.ds(pl.multiple_of(t * nb, nb), nb)
            nr = ar * sr - ai * si + st[rows, lr]
            ni = ar * si + ai * sr + st[rows, li]
            st[rows, lr] = nr
            st[rows, li] = ni
            return nr, ni

        sr, si = lax.fori_loop(0, steps, body, (car[:, lr], car[:, li]), unroll=2)
        car[:, lr] = sr
        car[:, li] = si
    ys = [jnp.dot(st[hs].astype(BF16), c_ref[...], preferred_element_type=F32) for hs in halves]
    if not finish:
        for hs, y in zip(halves, ys):
            o_ref[hs] = y
    else:
        zs = [jax.nn.gelu(y + yf_ref[hs] + d_ref[...] * u_ref[hs, 0:GROUP_W].astype(F32), approximate=True)
              for hs, y in zip(halves, ys)]
        gls = [jnp.dot(z.astype(BF16), gw_ref[...], preferred_element_type=F32) + gb_ref[...] for z in zs]
        for hs, z, gl in zip(halves, zs, gls):
            gate = u_ref[hs, GROUP_W:2 * GROUP_W].astype(F32)
            o_ref[hs] = (z * jax.nn.sigmoid(gl) * _silu(gate)).astype(BF16)


def _ssm_direction(u_tm, mats, nb, n_ctx_chunks, reverse, finish_args=None):
    rows, _ = u_tm.shape
    steps = SSM_T
    blk = steps * nb
    n_chunks = rows // blk
    n_state2 = mats[0].shape[1]
    if reverse:
        cidx = lambda i: (jnp.where(i < n_ctx_chunks, n_ctx_chunks - 1 - i, n_chunks - 1 - (i - n_ctx_chunks)), 0)
    else:
        cidx = lambda i: (i, 0)
    const = lambda i: (0, 0)
    in_specs = [pl.BlockSpec((blk, SSM_PACK), cidx)] + [pl.BlockSpec(m.shape, const) for m in mats]
    args = [u_tm, *mats]
    finish = finish_args is not None
    if finish:
        yf, d_skip, glu_w, glu_b = finish_args
        in_specs += [pl.BlockSpec((blk, GROUP_W), cidx), pl.BlockSpec(d_skip.shape, const),
                     pl.BlockSpec(glu_w.shape, const), pl.BlockSpec(glu_b.shape, const)]
        args += [yf, d_skip, glu_w, glu_b]
    return pl.pallas_call(
        functools.partial(_ssm_kernel, steps=steps, nb=nb, reverse=reverse, finish=finish),
        grid=(n_chunks,),
        in_specs=in_specs,
        out_specs=pl.BlockSpec((blk, GROUP_W), cidx),
        out_shape=jax.ShapeDtypeStruct((rows, GROUP_W), BF16 if finish else F32),
        scratch_shapes=[pltpu.VMEM((blk, n_state2), F32), pltpu.VMEM((nb, n_state2), F32)],
        compiler_params=_cparams("arbitrary"),
        name="ssm_rev" if reverse else "ssm_fwd",
    )(*args)


def _ssm_matrices(lam_re, lam_im, log_step, b_re, b_im, c_re, c_im):
    lr, li = lam_re.astype(F32), lam_im.astype(F32)
    step = jnp.exp(log_step.astype(F32))[:, None]
    mag = jnp.exp(lr * step)
    a_re, a_im = mag * jnp.cos(li * step), mag * jnp.sin(li * step)
    den = lr * lr + li * li
    q_re = ((a_re - 1.0) * lr + a_im * li) / den
    q_im = (a_im * lr - (a_re - 1.0) * li) / den
    bb_re, bb_im = b_re.astype(F32), b_im.astype(F32)
    bbar_re = q_re[..., None] * bb_re - q_im[..., None] * bb_im
    bbar_im = q_re[..., None] * bb_im + q_im[..., None] * bb_re
    eye = jnp.eye(SSM_GROUPS, dtype=F32)
    n_in = SSM_GROUPS * SSM_GROUP
    n_state = SSM_GROUPS * SSM_STATE

    def drive(m):
        return jnp.einsum('gph,gk->ghkp', m, eye).reshape(n_in, n_state).astype(BF16)

    def readout(m):
        return jnp.einsum('ghp,gk->gpkh', m, eye).reshape(n_state, n_in).astype(BF16)

    return (jnp.concatenate([drive(bbar_re), drive(bbar_im)], axis=1),
            jnp.concatenate([a_re.reshape(1, n_state), a_im.reshape(1, n_state)], axis=1),
            jnp.concatenate([readout(c_re.astype(F32)), readout(-c_im.astype(F32))], axis=0))


def _hy_pre_kernel(x_ref, prev_ref, next_ref, w_ref, b_ref, v_ref, x1_ref, x2_ref, sg_ref, *, n_tiles):
    i = pl.program_id(1)
    nconv = 3 * HY_W
    x = x_ref[0, :, 0:nconv].astype(F32)
    tm = x.shape[0]
    prev_row = jnp.where(i > 0, prev_ref[0, 7:8, 0:nconv].astype(F32), 0.0)
    next_row = jnp.where(i < n_tiles - 1, next_ref[0, 0:1, 0:nconv].astype(F32), 0.0)
    rid = lax.broadcasted_iota(jnp.int32, x.shape, 0)
    xm = jnp.where(rid == 0, prev_row, pltpu.roll(x, 1, 0))
    xp = jnp.where(rid == tm - 1, next_row, pltpu.roll(x, tm - 1, 0))
    proj = xm * w_ref[0:1] + x * w_ref[1:2] + xp * w_ref[2:3] + b_ref[...]
    gate = x_ref[0, :, nconv:nconv + HY_W].astype(F32)
    for o_ref, val in ((v_ref, proj[:, 0:HY_W]), (x1_ref, proj[:, HY_W:2 * HY_W]),
                       (x2_ref, proj[:, 2 * HY_W:3 * HY_W]), (sg_ref, _silu(gate))):
        for hf in range(HY_W // LANES):
            o_ref[0, hf] = val[:, hf * LANES:(hf + 1) * LANES].astype(BF16)


def _hy_pre(p_hy, conv_w, conv_b, tile0, n_tiles):
    b = p_hy.shape[0]
    tm = ROW_TILE
    sub = tm // 8
    n = n_tiles * tm
    last_sub = p_hy.shape[1] // 8 - 1
    halves = HY_W // LANES
    const2 = lambda bi, i: (0, 0)
    out_spec = pl.BlockSpec((1, halves, tm, LANES), lambda bi, i: (bi, 0, i, 0))
    out_shape = jax.ShapeDtypeStruct((b, halves, n, LANES), BF16)
    return pl.pallas_call(
        functools.partial(_hy_pre_kernel, n_tiles=n_tiles),
        grid=(b, n_tiles),
        in_specs=[pl.BlockSpec((1, tm, HY_PACK), lambda bi, i: (bi, i + tile0, 0)),
                  pl.BlockSpec((1, 8, HY_PACK), lambda bi, i: (bi, jnp.maximum((i + tile0) * sub - 1, 0), 0)),
                  pl.BlockSpec((1, 8, HY_PACK), lambda bi, i: (bi, jnp.minimum((i + tile0 + 1) * sub, last_sub), 0)),
                  pl.BlockSpec((3, 3 * HY_W), const2),
                  pl.BlockSpec((1, 3 * HY_W), const2)],
        out_specs=[out_spec] * 4,
        out_shape=[out_shape] * 4,
        compiler_params=_cparams("parallel", "parallel"),
        name="hyena_pre",
    )(p_hy, p_hy, p_hy, conv_w, conv_b)


def _bitrev(p, bits):
    r = 0
    for i in range(bits):
        r = (r << 1) | ((p >> i) & 1)
    return r


@functools.lru_cache(maxsize=None)
def _fft_plan(n):
    r = FFT_R
    nb = n // r
    m = 2 * nb
    big = 2 * n
    bits = int(round(math.log2(nb)))
    assert nb >= 2 and (1 << bits) == nb
    jmap = np.zeros(m, np.int64)
    for p in range(nb):
        jmap[p] = 2 * _bitrev(p, bits)
        jmap[nb + p] = 2 * _bitrev(p, bits) + 1
    k2 = np.arange(r)
    n2 = np.arange(r)
    g = np.zeros((m, 2 * r, 2 * r), np.float32)
    for s in range(m):
        ang = -2.0 * np.pi * (np.outer(k2, n2) / r + np.outer(np.ones(r), n2) * jmap[s] / big)
        gr, gi = np.cos(ang), np.sin(ang)
        g[s] = np.block([[gr, -gi], [gi, gr]])
    half0 = nb // 2
    k = np.arange(half0)
    wnb = np.exp(-2j * np.pi * k / nb)
    wm0 = np.exp(-2j * np.pi * k / m)
    wm1 = np.exp(-2j * np.pi * (k + half0) / m)
    tw_edge = np.stack([wnb.real, wnb.imag, wm0.real, wm0.imag, wm1.real, wm1.imag]).astype(np.float32)
    mids = []
    h = half0 // 2
    while h >= 1:
        kk = np.arange(nb // 2) % h
        w = np.exp(-2j * np.pi * kk / (2 * h))
        mids.append(np.stack([w.real, w.imag]))
        h //= 2
    tw_mid = (np.concatenate(mids, 0) if mids else np.zeros((2, max(nb // 2, 1)))).astype(np.float32)
    freq = (jmap[:, None] + m * k2[None, :])
    return dict(nb=nb, m=m, g=g, tw_edge=tw_edge, tw_mid=tw_mid, n_mid=len(mids), freq=freq)


def _cmul(ar, ai, wr, wi):
    return ar * wr - ai * wi, ar * wi + ai * wr


def _slot_pair(bb, lg, nb):
    half0 = nb // 2
    branch = bb // half0
    bf = bb - branch * half0
    i0 = branch * nb + ((bf >> lg) << (lg + 1)) + (bf & ((1 << lg) - 1))
    return bf, i0, i0 + (1 << lg)


def _block_dft_forward(read, w, tw_edge, tw_mid, nb, n_mid):
    r = FFT_R
    half0 = nb // 2
    re, im = slice(0, r), slice(r, 2 * r)

    def first(k, c):
        wr, wi = tw_edge[0, k], tw_edge[1, k]
        ar, ai = read(k)
        br, bi = read(k + half0)
        w[k, re], w[k, im] = ar + br, ai + bi
        dr, di = _cmul(ar - br, ai - bi, wr, wi)
        w[k + half0, re], w[k + half0, im] = dr, di
        a2r, a2i = _cmul(ar, ai, tw_edge[2, k], tw_edge[3, k])
        b2r, b2i = _cmul(br, bi, tw_edge[4, k], tw_edge[5, k])
        w[nb + k, re], w[nb + k, im] = a2r + b2r, a2i + b2i
        dr, di = _cmul(a2r - b2r, a2i - b2i, wr, wi)
        w[nb + k + half0, re], w[nb + k + half0, im] = dr, di
        return c

    lax.fori_loop(0, half0, first, 0)

    for s in range(n_mid):
        lg = int(round(math.log2(half0))) - 1 - s

        def dif(bb, c, s=s, lg=lg):
            bf, i0, i1 = _slot_pair(bb, lg, nb)
            wr, wi = tw_mid[2 * s, bf], tw_mid[2 * s + 1, bf]
            ar, ai, br, bi = w[i0, re], w[i0, im], w[i1, re], w[i1, im]
            w[i0, re], w[i0, im] = ar + br, ai + bi
            dr, di = _cmul(ar - br, ai - bi, wr, wi)
            w[i1, re], w[i1, im] = dr, di
            return c

        lax.fori_loop(0, nb, dif, 0)


def _hy_conv_kernel(*refs, nb, n_mid, n_post):
    tw_edge, tw_mid, a_ref, bias_ref, kh_ref, g_ref, gi_ref = refs[:7]
    post_refs = refs[7:7 + n_post]
    o_ref, w = refs[7 + n_post], refs[8 + n_post]
    r = FFT_R
    m = 2 * nb
    half0 = nb // 2
    re, im = slice(0, r), slice(r, 2 * r)

    def rows(k):
        return pl.ds(pl.multiple_of(k * r, r), r)

    _block_dft_forward(lambda k: (a_ref[0, 0, rows(k), :].astype(F32), a_ref[1, 0, rows(k), :].astype(F32)),
                       w, tw_edge, tw_mid, nb, n_mid)

    def spectral(t, c):
        slots = [FFT_SLOTS * t + u for u in range(FFT_SLOTS)]
        xs = [jnp.dot(g_ref[s], w[s].astype(BF16), preferred_element_type=F32) for s in slots]
        outs = []
        for s, x in zip(slots, xs):
            yr, yi = _cmul(x[re], x[im], kh_ref[0, s, re, :], kh_ref[0, s, im, :])
            y = jnp.concatenate([yr, yi], axis=0).astype(BF16)
            outs.append(jnp.dot(gi_ref[s], y, preferred_element_type=F32))
        for s, o in zip(slots, outs):
            w[s] = o
        return c

    lax.fori_loop(0, m // FFT_SLOTS, spectral, 0)

    for s in reversed(range(n_mid)):
        lg = int(round(math.log2(half0))) - 1 - s

        def dit(bb, c, s=s, lg=lg):
            bf, i0, i1 = _slot_pair(bb, lg, nb)
            wr, wi = tw_mid[2 * s, bf], -tw_mid[2 * s + 1, bf]
            ar, ai = w[i0, re], w[i0, im]
            br, bi = _cmul(w[i1, re], w[i1, im], wr, wi)
            w[i0, re], w[i0, im] = ar + br, ai + bi
            w[i1, re], w[i1, im] = ar - br, ai - bi
            return c

        lax.fori_loop(0, nb, dit, 0)

    bias = bias_ref[0]

    def emit(k, yr, yi):
        for bsel, y in ((0, yr), (1, yi)):
            a = a_ref[bsel, 0, rows(k), :].astype(F32)
            val = y + bias * a
            for p_ref in post_refs:
                val = val * p_ref[bsel, 0, rows(k), :].astype(F32)
            o_ref[bsel, 0, rows(k), :] = val.astype(o_ref.dtype)

    def last(k, c):
        wr, wi = tw_edge[0, k], -tw_edge[1, k]
        ar, ai = w[k, re], w[k, im]
        br, bi = _cmul(w[k + half0, re], w[k + half0, im], wr, wi)
        cr, ci = w[nb + k, re], w[nb + k, im]
        dr, di = _cmul(w[nb + k + half0, re], w[nb + k + half0, im], wr, wi)
        o0r, o0i = _cmul(cr + dr, ci + di, tw_edge[2, k], -tw_edge[3, k])
        o1r, o1i = _cmul(cr - dr, ci - di, tw_edge[4, k], -tw_edge[5, k])
        emit(k, ar + br + o0r, ai + bi + o0i)
        emit(k + half0, ar - br + o1r, ai - bi + o1i)
        return c

    lax.fori_loop(0, half0, last, 0)


def _hy_conv(a, khat, bias, posts, out_dtype):
    b, halves, n, lanes = a.shape
    plan = _fft_plan(n)
    nb, m = plan['nb'], plan['m']
    r = FFT_R
    sig = pl.BlockSpec((2, 1, n, lanes), lambda hf, pr: (pr, hf, 0, 0))
    smem = pl.BlockSpec(memory_space=pltpu.SMEM)
    once = pl.Buffered(1)
    in_specs = [smem, smem, sig,
                pl.BlockSpec((1, 1, lanes), lambda hf, pr: (hf, 0, 0)),
                pl.BlockSpec((1, m, 2 * r, lanes), lambda hf, pr: (hf, 0, 0, 0), pipeline_mode=once),
                pl.BlockSpec((m, 2 * r, 2 * r), lambda hf, pr: (0, 0, 0), pipeline_mode=once),
                pl.BlockSpec((m, 2 * r, 2 * r), lambda hf, pr: (0, 0, 0), pipeline_mode=once)]
    in_specs += [sig] * len(posts)
    g_fwd = jnp.asarray(plan['g'], BF16)
    g_inv = jnp.asarray(np.swapaxes(plan['g'], 1, 2), BF16)
    return pl.pallas_call(
        functools.partial(_hy_conv_kernel, nb=nb, n_mid=plan['n_mid'], n_post=len(posts)),
        grid=(halves, b // 2),
        in_specs=in_specs,
        out_specs=sig,
        out_shape=jax.ShapeDtypeStruct(a.shape, out_dtype),
        scratch_shapes=[pltpu.VMEM((m, 2 * r, lanes), F32)],
        compiler_params=pltpu.CompilerParams(dimension_semantics=("arbitrary", "arbitrary"),
                                             vmem_limit_bytes=HY_VMEM_LIMIT),
        name="hyena_conv",
    )(jnp.asarray(plan['tw_edge']), jnp.asarray(plan['tw_mid']), a, bias, khat, g_fwd, g_inv, *posts)


_HP = lax.Precision.HIGHEST
HY_TAPS_W = HY_ORDER * 2 * HY_W
HY_FEAT = 128


def _hy_taps_kernel(z_ref, dec_ref, w1_ref, b1_ref, f1_ref, w2_ref, b2_ref, f2_ref, w3_ref, o_ref,
                    h_scr, ss_scr):
    phase, i = pl.program_id(0), pl.program_id(1)
    tm = z_ref.shape[0]
    rows = pl.ds(pl.multiple_of(i * tm, tm), tm)

    @pl.when(phase == 0)
    def _():
        @pl.when(i == 0)
        def _():
            ss_scr[...] = jnp.zeros_like(ss_scr)

        h = jnp.dot(z_ref[...], w1_ref[...], precision=_HP, preferred_element_type=F32)
        h = jnp.sin(f1_ref[...] * (h + b1_ref[...]))
        h = jnp.dot(h, w2_ref[...], precision=_HP, preferred_element_type=F32)
        h = jnp.sin(f2_ref[...] * (h + b2_ref[...]))
        dec = dec_ref[...]
        for c0 in range(0, HY_TAPS_W, HY_W):
            t = jnp.dot(h, w3_ref[:, c0:c0 + HY_W], precision=_HP, preferred_element_type=F32) * dec
            h_scr[rows, c0:c0 + HY_W] = t
            ss_scr[:, c0:c0 + HY_W] += jnp.sum(t * t, axis=0, keepdims=True)

    @pl.when(phase == 1)
    def _():
        first_row = lax.broadcasted_iota(jnp.int32, (tm, HY_W), 0) == jnp.where(i == 0, 0, -1)
        for o in range(HY_ORDER):
            c0 = o * 2 * HY_W
            tot = ss_scr[:, c0:c0 + HY_W] + ss_scr[:, c0 + HY_W:c0 + 2 * HY_W]
            inv = lax.rsqrt(tot + EPS)
            o_ref[:, c0:c0 + HY_W] = h_scr[rows, c0:c0 + HY_W] * inv
            o_ref[:, c0 + HY_W:c0 + 2 * HY_W] = jnp.where(first_row, 0.0, h_scr[rows, c0 + HY_W:c0 + 2 * HY_W] * inv)


@functools.lru_cache(maxsize=None)
def _filter_features(n):
    t = np.linspace(0.0, 1.0, n, dtype=np.float32)[:, None]
    omega = (2.0 * np.pi * np.arange(n, dtype=np.float32)[:, None] / n).astype(np.float32)
    bands = np.linspace(1e-4, HY_BANDS - 1, HY_BANDS, dtype=np.float32)[None, :]
    z = np.zeros((n, HY_FEAT), np.float32)
    z[:, 0:1] = t
    z[:, 1:1 + HY_BANDS] = np.cos(bands * omega)
    z[:, 1 + HY_BANDS:HY_EMB] = -np.sin(bands * omega)
    max_decay = math.log(HY_DECAY_TARGET) / HY_FAST_DECAY
    min_decay = math.log(HY_DECAY_TARGET) / HY_SLOW_DECAY
    deltas = np.linspace(min_decay, max_decay, HY_W, dtype=np.float32)
    dec = np.exp(-t * np.abs(deltas)).astype(np.float32)
    return z, dec


def _hy_taps(n, w1, b1, fr1, w2, b2, fr2, w3):
    z, dec = _filter_features(n)
    tm = min(n, 512)
    pad = HY_FEAT - HY_HIDDEN
    w1p = jnp.pad(w1.astype(F32), ((0, HY_FEAT - HY_EMB), (0, pad)))
    w2p = jnp.pad(w2.astype(F32), ((0, pad), (0, pad)))
    w3p = jnp.pad(w3.astype(F32), ((0, pad), (0, 0)))
    vec = lambda v: jnp.pad(v.astype(F32), (0, pad)).reshape(1, HY_FEAT)
    const = lambda ph, i: (0, 0)
    return pl.pallas_call(
        _hy_taps_kernel,
        grid=(2, n // tm),
        in_specs=[pl.BlockSpec((tm, HY_FEAT), lambda ph, i: (i * (1 - ph), 0)),
                  pl.BlockSpec((tm, HY_W), lambda ph, i: (i * (1 - ph), 0)),
                  pl.BlockSpec((HY_FEAT, HY_FEAT), const), pl.BlockSpec((1, HY_FEAT), const),
                  pl.BlockSpec((1, HY_FEAT), const),
                  pl.BlockSpec((HY_FEAT, HY_FEAT), const), pl.BlockSpec((1, HY_FEAT), const),
                  pl.BlockSpec((1, HY_FEAT), const),
                  pl.BlockSpec((HY_FEAT, HY_TAPS_W), const)],
        out_specs=pl.BlockSpec((tm, HY_TAPS_W), lambda ph, i: (i * ph, 0)),
        out_shape=jax.ShapeDtypeStruct((n, HY_TAPS_W), F32),
        scratch_shapes=[pltpu.VMEM((n, HY_TAPS_W), F32), pltpu.VMEM((1, HY_TAPS_W), F32)],
        compiler_params=_cparams("arbitrary", "arbitrary"),
        name="hyena_taps",
    )(jnp.asarray(z), jnp.asarray(dec), w1p, vec(b1), vec(fr1), w2p, vec(b2), vec(fr2), w3p)


def _hy_spec_kernel(tw_edge, tw_mid, f_ref, b_ref, ghi_ref, glo_ref, o_ref, w, *, nb, n_mid, inv_n):
    r = FFT_R
    m = 2 * nb
    re, im = slice(0, r), slice(r, 2 * r)
    zeros = jnp.zeros((r, LANES), F32)

    def rows(k):
        return pl.ds(pl.multiple_of(k * r, r), r)

    def in_block_dft(s):
        x = w[s]
        xh = x.astype(BF16)
        xl = (x - xh.astype(F32)).astype(BF16)
        gh = ghi_ref[s]
        y = jnp.dot(gh, xh, preferred_element_type=F32)
        y += jnp.dot(gh, xl, preferred_element_type=F32)
        y += jnp.dot(glo_ref[s], xh, preferred_element_type=F32)
        return y * inv_n

    _block_dft_forward(lambda k: (f_ref[rows(k), :], zeros), w, tw_edge, tw_mid, nb, n_mid)

    def fwd_part(s, c):
        o_ref[0, 0, s] = in_block_dft(s)
        return c

    lax.fori_loop(0, m, fwd_part, 0)
    _block_dft_forward(lambda k: (b_ref[rows(k), :], zeros), w, tw_edge, tw_mid, nb, n_mid)

    def bwd_part(s, c):
        y = in_block_dft(s)
        o_ref[0, 0, s, re] += y[re]
        o_ref[0, 0, s, im] -= y[im]
        return c

    lax.fori_loop(0, m, bwd_part, 0)


def _hy_filter_spectrum(taps):
    n = taps.shape[0]
    plan = _fft_plan(n)
    nb, m = plan['nb'], plan['m']
    r = FFT_R
    halves = HY_W // LANES
    g = plan['g']
    g_hi = g.astype(jnp.bfloat16)
    g_lo = (g - np.asarray(g_hi, np.float32)).astype(jnp.bfloat16)
    smem = pl.BlockSpec(memory_space=pltpu.SMEM)
    once = pl.Buffered(1)
    table = lambda o, hf: (0, 0, 0)
    return pl.pallas_call(
        functools.partial(_hy_spec_kernel, nb=nb, n_mid=plan['n_mid'], inv_n=1.0 / (2 * n)),
        grid=(HY_ORDER, halves),
        in_specs=[smem, smem,
                  pl.BlockSpec((n, LANES), lambda o, hf: (0, o * 2 * halves + hf)),
                  pl.BlockSpec((n, LANES), lambda o, hf: (0, o * 2 * halves + halves + hf)),
                  pl.BlockSpec((m, 2 * r, 2 * r), table, pipeline_mode=once),
                  pl.BlockSpec((m, 2 * r, 2 * r), table, pipeline_mode=once)],
        out_specs=pl.BlockSpec((1, 1, m, 2 * r, LANES), lambda o, hf: (o, hf, 0, 0, 0)),
        out_shape=jax.ShapeDtypeStruct((HY_ORDER, halves, m, 2 * r, LANES), F32),
        scratch_shapes=[pltpu.VMEM((m, 2 * r, LANES), F32)],
        compiler_params=pltpu.CompilerParams(dimension_semantics=("arbitrary", "arbitrary"),
                                             vmem_limit_bytes=HY_VMEM_LIMIT),
        name="hyena_spectrum",
    )(jnp.asarray(plan['tw_edge']), jnp.asarray(plan['tw_mid']), taps, taps,
      jnp.asarray(g_hi), jnp.asarray(g_lo))


def _pack_w_in(w_in):
    d = w_in.shape[0]
    z = lambda n: jnp.zeros((d, n), w_in.dtype)
    o = 0
    q_lat = w_in[:, o:o + MLA_Q_RANK]; o += MLA_Q_RANK
    kv_lat = w_in[:, o:o + MLA_KV_RANK]; o += MLA_KV_RANK
    k_rope = w_in[:, o:o + MLA_ROPE]; o += MLA_ROPE
    gate = w_in[:, o:o + GROUP_W]; o += GROUP_W
    mla = jnp.concatenate([q_lat, z(256 - MLA_Q_RANK), kv_lat, z(MLA_NOPE), k_rope,
                           z(MLA_DK - MLA_NOPE - MLA_ROPE), gate], axis=1)
    return jnp.concatenate([mla, w_in[:, o:]], axis=1).astype(BF16)


def _pack_mla_up(w_uq, w_ukv):
    dq = MLA_NOPE + MLA_ROPE
    wq = w_uq.reshape(MLA_Q_RANK, MLA_HEADS, dq).transpose(1, 0, 2)
    wq = jnp.pad(wq, ((0, 0), (0, 256 - MLA_Q_RANK), (0, MLA_DK - dq))).astype(BF16)
    wkv = w_ukv.reshape(MLA_KV_RANK, MLA_HEADS, MLA_NOPE + MLA_V).transpose(1, 0, 2)
    wk = jnp.pad(wkv[:, :, :MLA_NOPE], ((0, 0), (0, 0), (0, MLA_DK - MLA_NOPE))).astype(BF16)
    wvt = jnp.swapaxes(wkv[:, :, MLA_NOPE:], 1, 2).astype(BF16)
    return wq, wk, wvt


def _merge_halves(y_ctx, y_lat):
    y = jnp.concatenate([y_ctx, y_lat], axis=2)
    b, halves, lt, lanes = y.shape
    return jnp.transpose(y, (0, 2, 1, 3)).reshape(b, lt, halves * lanes)


def kernel(x, c, ctx, c_ctx, w_mod, b_mod, g_pre, g_post, w_in, w_out, mla_g_cq, mla_w_uq, mla_g_ckv, mla_w_ukv, gqa_g_q, gqa_g_k, ssm_lambda_re, ssm_lambda_im, ssm_log_step, ssm_b_re, ssm_b_im, ssm_c_re, ssm_c_im, ssm_d, ssm_glu_w, ssm_glu_b, hy_conv_w, hy_conv_b, hy_f_w1, hy_f_b1, hy_f_freq1, hy_f_w2, hy_f_b2, hy_f_freq2, hy_f_w3, hy_bias):
    b, n_lat, d = x.shape
    n_ctx = ctx.shape[1]
    depth = w_in.shape[0]
    lt = n_ctx + n_lat
    nct = n_ctx // ROW_TILE
    assert n_ctx % ROW_TILE == 0 and n_lat % ROW_TILE == 0 and b % 2 == 0 and b % 8 == 0

    n_cond = -(-(b + 1) // 8) * 8
    cond = jnp.zeros((n_cond, d), F32).at[:b].set(c).at[b].set(c_ctx)
    mod = _modulation(cond, w_mod, b_mod)[:, :b + 1]
    shift, scale, gate = mod[..., :d], mod[..., d:2 * d], mod[..., 2 * d:]

    mla_cos, mla_sin, gqa_cos, gqa_sin = _rope_tables(n_ctx, n_lat)
    nq = GQA_HEADS * GQA_DIM
    ones_bd = jnp.asarray(np.kron(np.eye(GQA_HEADS), np.ones((GQA_DIM, GQA_DIM))), BF16)

    xa = jnp.concatenate([ctx, x], axis=1)
    for l in range(depth):
        sc = (g_pre[l][None, :] * (1.0 + scale[l]))[:, None, :]
        sh = shift[l][:, None, :]
        p_mla, p_gqa, p_ssm, p_hy = _inproj(xa, sc, sh, _pack_w_in(w_in[l]), nct)

        wq, wk, wvt = _pack_mla_up(mla_w_uq[l], mla_w_ukv[l])
        g_cq = jnp.pad(mla_g_cq[l], (0, 256 - MLA_Q_RANK)).reshape(1, 256)
        qt, k, vt = _mla_prep(p_mla, mla_cos, mla_sin, g_cq, mla_g_ckv[l].reshape(1, -1), wq, wk, wvt)
        a_out = _attention(qt, k, vt, p_mla, 2, n_ctx)

        qt, k, vt = _gqa_prep(p_gqa, gqa_cos, gqa_sin, jnp.tile(gqa_g_q[l], GQA_HEADS).reshape(1, nq),
                              jnp.tile(gqa_g_k[l], GQA_KV_HEADS).reshape(1, -1), ones_bd)
        g_out = _attention(qt, k, vt, p_gqa, 2, n_ctx)

        u_tm = jnp.transpose(p_ssm, (1, 0, 2)).reshape(lt * b, SSM_PACK)
        mats = [_ssm_matrices(ssm_lambda_re[l, di], ssm_lambda_im[l, di], ssm_log_step[l, di],
                              ssm_b_re[l, di], ssm_b_im[l, di], ssm_c_re[l, di], ssm_c_im[l, di])
                for di in range(2)]
        n_ctx_chunks = n_ctx // SSM_T
        y_f = _ssm_direction(u_tm, mats[0], b, n_ctx_chunks, False)
        fin = (y_f, ssm_d[l].reshape(1, -1), ssm_glu_w[l].astype(BF16), ssm_glu_b[l].reshape(1, -1))
        s_tm = _ssm_direction(u_tm, mats[1], b, n_ctx_chunks, True, fin)
        s_out = jnp.transpose(s_tm.reshape(lt, b, GROUP_W), (1, 0, 2))

        filt = (hy_f_w1[l], hy_f_b1[l], hy_f_freq1[l], hy_f_w2[l], hy_f_b2[l], hy_f_freq2[l], hy_f_w3[l])
        conv_b = hy_conv_b[l].reshape(1, -1)
        parts = []
        for tile0, n_tiles in ((0, nct), (nct, n_lat // ROW_TILE)):
            n = n_tiles * ROW_TILE
            hv, hx1, hx2, hsg = _hy_pre(p_hy, hy_conv_w[l], conv_b, tile0, n_tiles)
            khat = _hy_filter_spectrum(_hy_taps(n, *filt))
            bias = hy_bias[l].astype(F32).reshape(HY_ORDER, HY_W // LANES, 1, LANES)
            z1 = _hy_conv(hv, khat[0], bias[0], (hx1,), BF16)
            parts.append(_hy_conv(z1, khat[1], bias[1], (hx2, hsg), BF16))
        y_out = _merge_halves(*parts)

        xa = _outproj(a_out, g_out, s_out, y_out, xa, gate[l][:, None, :], g_post[l].reshape(1, d),
                      w_out[l].astype(BF16), nct, tile0=nct if l == depth - 1 else 0)
    return xa
```

```python
import functools
import math

import numpy as np
import jax
import jax.numpy as jnp
from jax import lax
from jax.experimental import pallas as pl
from jax.experimental.pallas import tpu as pltpu

F32 = jnp.float32
BF16 = jnp.bfloat16

GRID_W = 64
ROPE_BASE = 10000.0
EPS = 1e-6
GROUP_W = 256
MLA_HEADS, MLA_NOPE, MLA_ROPE, MLA_V = 4, 64, 32, 64
MLA_Q_RANK, MLA_KV_RANK = 192, 128
GQA_HEADS, GQA_KV_HEADS, GQA_DIM = 4, 2, 64
SSM_GROUPS, SSM_GROUP, SSM_STATE = 16, 16, 64
HY_W, HY_ORDER, HY_EMB, HY_HIDDEN = 256, 2, 33, 64
HY_BANDS = (HY_EMB - 1) // 2
HY_FAST_DECAY, HY_SLOW_DECAY, HY_DECAY_TARGET = 0.3, 1.5, 1e-2

MLA_PACK = 768
GQA_PACK = 768
SSM_PACK = 512
HY_PACK = 1024
IN_PACK = MLA_PACK + GQA_PACK + SSM_PACK + HY_PACK
MLA_DK = 128

LANES = 128
ROW_TILE = 256
ATT_TQ = 256
ATT_TK = 256
ATT_UNROLLS = (16, 8, 4, 2)
ONES_ROWS = 16
SSM_T = 64
SSM_LANE_SPLIT = 512
FFT_R = 128
FFT_SLOTS = 4
VMEM_LIMIT = 52 * 1024 * 1024
HY_VMEM_LIMIT = 58 * 1024 * 1024


def _cparams(*sem):
    return pltpu.CompilerParams(dimension_semantics=sem, vmem_limit_bytes=VMEM_LIMIT)


def _silu(x):
    return x * jax.nn.sigmoid(x)


def _mod_kernel(c_ref, w_ref, b_ref, o_ref):
    c = c_ref[...]
    s = _silu(c).astype(BF16)
    o_ref[0] = jnp.dot(s, w_ref[0].astype(BF16), preferred_element_type=F32) + b_ref[0]


def _modulation(cond, w_mod, b_mod):
    depth, d, n3 = w_mod.shape
    r = cond.shape[0]
    tn = 512
    return pl.pallas_call(
        _mod_kernel,
        grid=(depth, n3 // tn),
        in_specs=[pl.BlockSpec((r, d), lambda l, j: (0, 0)),
                  pl.BlockSpec((1, d, tn), lambda l, j: (l, 0, j)),
                  pl.BlockSpec((1, 1, tn), lambda l, j: (l, 0, j))],
        out_specs=pl.BlockSpec((1, r, tn), lambda l, j: (l, 0, j)),
        out_shape=jax.ShapeDtypeStruct((depth, r, n3), F32),
        compiler_params=_cparams("parallel", "parallel"),
        name="modulation",
    )(cond, w_mod, b_mod.reshape(depth, 1, n3))


def _inproj_kernel(x_ref, sc_ref, sh_ref, w_ref, o_mla, o_gqa, o_ssm, o_hy):
    x = x_ref[0]
    ms = jnp.mean(x * x, axis=-1, keepdims=True)
    h = x * lax.rsqrt(ms + EPS) * sc_ref[0] + sh_ref[0]
    hb = h.astype(BF16)
    c0 = 0
    for o_ref, width in ((o_mla, MLA_PACK), (o_gqa, GQA_PACK), (o_ssm, SSM_PACK), (o_hy, HY_PACK)):
        o_ref[0] = jnp.dot(hb, w_ref[:, c0:c0 + width], preferred_element_type=F32).astype(BF16)
        c0 += width


def _inproj(xa, scale, shift, w, nct):
    b, lt, d = xa.shape
    tm = ROW_TILE
    widths = (MLA_PACK, GQA_PACK, SSM_PACK, HY_PACK)

    def mod_idx(bi, i):
        return (jnp.where(i < nct, b, bi), 0, 0)

    return pl.pallas_call(
        _inproj_kernel,
        grid=(b, lt // tm),
        in_specs=[pl.BlockSpec((1, tm, d), lambda bi, i: (bi, i, 0)),
                  pl.BlockSpec((1, 1, d), mod_idx),
                  pl.BlockSpec((1, 1, d), mod_idx),
                  pl.BlockSpec((d, IN_PACK), lambda bi, i: (0, 0))],
        out_specs=[pl.BlockSpec((1, tm, wd), lambda bi, i: (bi, i, 0)) for wd in widths],
        out_shape=[jax.ShapeDtypeStruct((b, lt, wd), BF16) for wd in widths],
        compiler_params=_cparams("parallel", "parallel"),
        name="inproj",
    )(xa, scale, shift, w)


def _outproj_kernel(a_ref, g_ref, s_ref, y_ref, x_ref, gt_ref, gp_ref, w_ref, o_ref):
    half = a_ref.shape[1] // 2
    for hs in (slice(0, half), slice(half, 2 * half)):
        acc = jnp.dot(a_ref[0, hs], w_ref[0:GROUP_W], preferred_element_type=F32)
        acc += jnp.dot(g_ref[0, hs], w_ref[GROUP_W:2 * GROUP_W], preferred_element_type=F32)
        acc += jnp.dot(s_ref[0, hs], w_ref[2 * GROUP_W:3 * GROUP_W], preferred_element_type=F32)
        acc += jnp.dot(y_ref[0, hs], w_ref[3 * GROUP_W:4 * GROUP_W], preferred_element_type=F32)
        ms = jnp.mean(acc * acc, axis=-1, keepdims=True)
        o_ref[0, hs] = x_ref[0, hs] + gt_ref[0] * (acc * lax.rsqrt(ms + EPS) * gp_ref[...])


def _outproj(a, g, s, y, xa, gate, g_post, w, nct, tile0=0):
    b, lt, d = xa.shape
    tm = ROW_TILE

    def mod_idx(bi, i):
        return (jnp.where(i + tile0 < nct, b, bi), 0, 0)

    row = lambda bi, i: (bi, i + tile0, 0)
    return pl.pallas_call(
        _outproj_kernel,
        grid=(b, lt // tm - tile0),
        in_specs=[pl.BlockSpec((1, tm, GROUP_W), row)] * 4 + [
            pl.BlockSpec((1, tm, d), row),
            pl.BlockSpec((1, 1, d), mod_idx),
            pl.BlockSpec((1, d), lambda bi, i: (0, 0)),
            pl.BlockSpec((4 * GROUP_W, d), lambda bi, i: (0, 0))],
        out_specs=pl.BlockSpec((1, tm, d), lambda bi, i: (bi, i, 0)),
        out_shape=jax.ShapeDtypeStruct((b, lt - tile0 * tm, d), F32),
        compiler_params=_cparams("parallel", "parallel"),
        name="outproj",
    )(a, g, s, y, xa, gate, g_post, w)


def _rope(x, cos, sin, shift):
    w = x.shape[-1]
    lane = lax.broadcasted_iota(jnp.int32, x.shape, 1)
    first = (lane & shift) == 0
    swapped = jnp.where(first, -pltpu.roll(x, w - shift, 1), pltpu.roll(x, shift, 1))
    return x * cos + swapped * sin


def _rope_tables(n_ctx, n_lat):
    t = np.arange(n_lat)
    row = (t // GRID_W).astype(np.float64)
    col = (t % GRID_W).astype(np.float64)

    def block(pos, h):
        inv = ROPE_BASE ** (-np.arange(0, h, 2, dtype=np.float64) / h)
        ang = (pos[:, None].astype(np.float32) * inv[None, :].astype(np.float32)).astype(np.float32)
        c, s = np.cos(ang), np.sin(ang)
        return np.concatenate([c, c], -1), np.concatenate([s, s], -1)

    def full(h, lead, width):
        cr, sr = block(row, h)
        cc, sc = block(col, h)
        cos = np.ones((n_ctx + n_lat, width), np.float32)
        sin = np.zeros((n_ctx + n_lat, width), np.float32)
        cos[n_ctx:, lead:lead + 2 * h] = np.concatenate([cr, cc], -1)
        sin[n_ctx:, lead:lead + 2 * h] = np.concatenate([sr, sc], -1)
        return cos, sin

    mc, ms = full(MLA_ROPE // 2, MLA_NOPE, MLA_DK)
    gc, gs = full(GQA_DIM // 2, 0, GQA_DIM)
    reps = GQA_HEADS
    return (jnp.asarray(mc), jnp.asarray(ms),
            jnp.asarray(np.tile(gc, (1, reps))), jnp.asarray(np.tile(gs, (1, reps))))


_NT = (((1,), (1,)), ((), ()))
_TN = (((0,), (0,)), ((), ()))
LOG2E = math.log2(math.e)


def _mla_prep_kernel(p_ref, cos_ref, sin_ref, gq_ref, gkv_ref, wq_ref, wk_ref, wvt_ref, eye_ref,
                     qt_ref, k_ref, vt_ref):
    p = p_ref[0].astype(F32)
    cos, sin = cos_ref[...], sin_ref[...]
    ql = p[:, 0:256]
    rq = lax.rsqrt(jnp.sum(ql * ql, axis=-1, keepdims=True) * (1.0 / MLA_Q_RANK) + EPS)
    qn = (ql * rq * gq_ref[...]).astype(BF16)
    kvl = p[:, 256:384]
    rk = lax.rsqrt(jnp.mean(kvl * kvl, axis=-1, keepdims=True) + EPS)
    kvn = (kvl * rk * gkv_ref[...]).astype(BF16)
    k_rope = _rope(p[:, 384:512], cos, sin, MLA_ROPE // 4)
    scale = (MLA_NOPE + MLA_ROPE) ** -0.5 * LOG2E
    heads = range(MLA_HEADS)
    qs = [jnp.dot(qn, wq_ref[h], preferred_element_type=F32) for h in heads]
    ks = [jnp.dot(kvn, wk_ref[h], preferred_element_type=F32) for h in heads]
    vts = [lax.dot_general(wvt_ref[h], kvn, _NT, preferred_element_type=F32) for h in heads]
    for h in heads:
        k_ref[0, h] = (ks[h] + k_rope).astype(BF16)
        vt_ref[0, h, 0, 0:MLA_V] = vts[h].astype(BF16)
        vt_ref[0, h, 0, MLA_V:MLA_V + ONES_ROWS] = jnp.ones((ONES_ROWS, kvn.shape[0]), BF16)
    qr = [(_rope(q, cos, sin, MLA_ROPE // 4) * scale).astype(BF16) for q in qs]
    for h in heads:
        qt_ref[0, h] = lax.dot_general(eye_ref[...], qr[h], _NT, preferred_element_type=F32).astype(BF16)


def _mla_prep(p_mla, cos, sin, g_cq, g_ckv, wq, wk, wvt):
    b, lt, _ = p_mla.shape
    tm = ROW_TILE
    const2 = lambda bi, i: (0, 0)
    const3 = lambda bi, i: (0, 0, 0)
    return pl.pallas_call(
        _mla_prep_kernel,
        grid=(b, lt // tm),
        in_specs=[pl.BlockSpec((1, tm, 512), lambda bi, i: (bi, i, 0)),
                  pl.BlockSpec((tm, MLA_DK), lambda bi, i: (i, 0)),
                  pl.BlockSpec((tm, MLA_DK), lambda bi, i: (i, 0)),
                  pl.BlockSpec((1, 256), const2),
                  pl.BlockSpec((1, MLA_KV_RANK), const2),
                  pl.BlockSpec((MLA_HEADS, 256, MLA_DK), const3),
                  pl.BlockSpec((MLA_HEADS, MLA_KV_RANK, MLA_DK), const3),
                  pl.BlockSpec((MLA_HEADS, MLA_V, MLA_KV_RANK), const3),
                  pl.BlockSpec((MLA_DK, MLA_DK), const2)],
        out_specs=[pl.BlockSpec((1, MLA_HEADS, MLA_DK, tm), lambda bi, i: (bi, 0, 0, i)),
                   pl.BlockSpec((1, MLA_HEADS, tm, MLA_DK), lambda bi, i: (bi, 0, i, 0)),
                   pl.BlockSpec((1, MLA_HEADS, 1, MLA_V + ONES_ROWS, tm), lambda bi, i: (bi, 0, i, 0, 0))],
        out_shape=[jax.ShapeDtypeStruct((b, MLA_HEADS, MLA_DK, lt), BF16),
                   jax.ShapeDtypeStruct((b, MLA_HEADS, lt, MLA_DK), BF16),
                   jax.ShapeDtypeStruct((b, MLA_HEADS, lt // tm, MLA_V + ONES_ROWS, tm), BF16)],
        compiler_params=_cparams("parallel", "parallel"),
        name="mla_prep",
    )(p_mla, cos, sin, g_cq, g_ckv, wq, wk, wvt, jnp.eye(MLA_DK, dtype=BF16))


def _head_mean_sq(x, ones_bd):
    sq = x * x
    hi = sq.astype(BF16)
    lo = (sq - hi.astype(F32)).astype(BF16)
    s = jnp.dot(hi, ones_bd, preferred_element_type=F32) + jnp.dot(lo, ones_bd, preferred_element_type=F32)
    return s * (1.0 / GQA_DIM)


def _gqa_prep_kernel(p_ref, cos_ref, sin_ref, gq_ref, gk_ref, ones_ref, eye_ref, qt_ref, k_ref, vt_ref):
    p = p_ref[0].astype(F32)
    cos, sin = cos_ref[...], sin_ref[...]
    nq = GQA_HEADS * GQA_DIM
    nk = GQA_KV_HEADS * GQA_DIM
    q = p[:, 0:nq]
    k = p[:, nq:nq + nk]
    v = p_ref[0, :, nq + nk:nq + 2 * nk]
    q_ms = _head_mean_sq(q, ones_ref[...])
    k_ms = _head_mean_sq(k, ones_ref[0:nk, 0:nk])
    for h in range(GQA_KV_HEADS):
        sel = eye_ref[h * GQA_DIM:(h + 1) * GQA_DIM, 0:nk]
        vt_ref[0, h, 0, 0:GQA_DIM] = lax.dot_general(sel, v, _NT, preferred_element_type=F32).astype(BF16)
        vt_ref[0, h, 0, GQA_DIM:GQA_DIM + ONES_ROWS] = jnp.ones((ONES_ROWS, v.shape[0]), BF16)
    qn = q * lax.rsqrt(q_ms + EPS) * gq_ref[...]
    qr = (_rope(qn, cos, sin, GQA_DIM // 4) * (GQA_DIM ** -0.5 * LOG2E)).astype(BF16)
    kn = k * lax.rsqrt(k_ms + EPS) * gk_ref[...]
    kr = _rope(kn, cos[:, 0:nk], sin[:, 0:nk], GQA_DIM // 4)
    for h in range(GQA_HEADS):
        sel = eye_ref[h * GQA_DIM:(h + 1) * GQA_DIM, :]
        qt_ref[0, h] = lax.dot_general(sel, qr, _NT, preferred_element_type=F32).astype(BF16)
    for h in range(GQA_KV_HEADS):
        k_ref[0, h] = kr[:, h * GQA_DIM:(h + 1) * GQA_DIM].astype(BF16)


def _gqa_prep(p_gqa, cos, sin, g_q, g_k, ones_bd):
    b, lt, _ = p_gqa.shape
    tm = ROW_TILE
    nq = GQA_HEADS * GQA_DIM
    nk = GQA_KV_HEADS * GQA_DIM
    const2 = lambda bi, i: (0, 0)
    return pl.pallas_call(
        _gqa_prep_kernel,
        grid=(b, lt // tm),
        in_specs=[pl.BlockSpec((1, tm, 512), lambda bi, i: (bi, i, 0)),
                  pl.BlockSpec((tm, nq), lambda bi, i: (i, 0)),
                  pl.BlockSpec((tm, nq), lambda bi, i: (i, 0)),
                  pl.BlockSpec((1, nq), const2),
                  pl.BlockSpec((1, nk), const2),
                  pl.BlockSpec((nq, nq), const2),
                  pl.BlockSpec((nq, nq), const2)],
        out_specs=[pl.BlockSpec((1, GQA_HEADS, GQA_DIM, tm), lambda bi, i: (bi, 0, 0, i)),
                   pl.BlockSpec((1, GQA_KV_HEADS, tm, GQA_DIM), lambda bi, i: (bi, 0, i, 0)),
                   pl.BlockSpec((1, GQA_KV_HEADS, 1, GQA_DIM + ONES_ROWS, tm), lambda bi, i: (bi, 0, i, 0, 0))],
        out_shape=[jax.ShapeDtypeStruct((b, GQA_HEADS, GQA_DIM, lt), BF16),
                   jax.ShapeDtypeStruct((b, GQA_KV_HEADS, lt, GQA_DIM), BF16),
                   jax.ShapeDtypeStruct((b, GQA_KV_HEADS, lt // tm, GQA_DIM + ONES_ROWS, tm), BF16)],
        compiler_params=_cparams("parallel", "parallel"),
        name="gqa_prep",
    )(p_gqa, cos, sin, g_q, g_k, ones_bd, jnp.eye(nq, dtype=BF16))


def _attn_kernel(qt_ref, qn_ref, k_ref, vt_ref, gate_ref, eye_ref, o_ref, sa_ref, sb_ref, pa_ref, pb_ref,
                 cm_ref, acc_ref, *, heads, group, nct_q, ctx_chunks, all_chunks, unroll):
    i = pl.program_id(1)
    n_loops = jnp.where(i < nct_q, (ctx_chunks - 1) // unroll, (all_chunks - 1) // unroll)
    tq = qt_ref.shape[3]
    dva = vt_ref.shape[3]
    dv = dva - ONES_ROWS

    def score(j, h, dst, q_ref=qt_ref):
        rows = pl.ds(pl.multiple_of(j * ATT_TK, ATT_TK), ATT_TK)
        s = jnp.dot(k_ref[0, h // group, rows, :], q_ref[0, h], preferred_element_type=F32)
        dst[h] = s
        return jnp.max(s, axis=0, keepdims=True)

    def value(j, h, p_ref):
        return jnp.dot(vt_ref[0, h // group, j], p_ref[h], preferred_element_type=F32)

    def step(j, carry, s_cur, s_nxt, p_prev, p_cur):
        state, cmax = carry
        new, nmax = [], []
        for h in range(heads):
            pv = value(jnp.maximum(j - 1, 0), h, p_prev)
            if s_nxt is not None:
                nmax.append(score(j + 1, h, s_nxt))
            m = state[h]
            m_new = jnp.maximum(m, cmax[h])
            alpha = jnp.exp2(m - m_new)
            p_cur[h] = jnp.exp2(s_cur[h] - m_new).astype(BF16)
            acc_ref[h] = alpha * (acc_ref[h] + pv)
            new.append(m_new)
        return tuple(new), tuple(nmax)

    bufs = ((sa_ref, sb_ref, pb_ref, pa_ref), (sb_ref, sa_ref, pa_ref, pb_ref))

    def body(t, carry):
        for u in range(unroll):
            carry = step(unroll * t + u, carry, *bufs[u % 2])
        return carry

    @pl.when(i == 0)
    def _():
        for h in range(heads):
            cm_ref[h] = score(0, h, sa_ref)

    cmax0 = tuple(cm_ref[h] for h in range(heads))
    pb_ref[...] = jnp.zeros_like(pb_ref)
    acc_ref[...] = jnp.zeros_like(acc_ref)
    init = tuple(jnp.full((1, tq), -jnp.inf, F32) for _ in range(heads))
    carry = lax.fori_loop(0, n_loops, body, (init, cmax0))
    last = unroll * n_loops
    step(last, carry, sa_ref, None, pb_ref, pa_ref)
    for h in range(heads):
        cm_ref[h] = score(0, h, sa_ref, qn_ref)
    outs = []
    for h in range(heads):
        acc = acc_ref[h] + value(last, h, pa_ref)
        outs.append((acc[0:dv] * (1.0 / acc[dv:dv + 1])).astype(BF16))
    out = lax.dot_general(jnp.concatenate(outs, axis=0), eye_ref[...], _TN, preferred_element_type=F32)
    g = gate_ref[0].astype(F32)
    o_ref[0] = (out * _silu(g)).astype(BF16)


def _attention(qt, k, vt, gate_src, gate_block, n_ctx):
    b, heads, dk, lt = qt.shape
    hk, n_chunks, dva = k.shape[1], vt.shape[2], vt.shape[3]
    assert heads * (dva - ONES_ROWS) == GROUP_W
    n_tiles = lt // ATT_TQ
    ctx_chunks, all_chunks = n_ctx // ATT_TK, lt // ATT_TK
    unroll = next(u for u in ATT_UNROLLS if (ctx_chunks - 1) % u == 0 and (all_chunks - 1) % u == 0)
    kern = functools.partial(_attn_kernel, heads=heads, group=heads // hk, nct_q=n_ctx // ATT_TQ,
                             ctx_chunks=ctx_chunks, all_chunks=all_chunks, unroll=unroll)
    return pl.pallas_call(
        kern,
        grid=(b, n_tiles),
        in_specs=[pl.BlockSpec((1, heads, dk, ATT_TQ), lambda bi, i: (bi, 0, 0, i)),
                  pl.BlockSpec((1, heads, dk, ATT_TQ), lambda bi, i: (bi, 0, 0, jnp.minimum(i + 1, n_tiles - 1))),
                  pl.BlockSpec((1, hk, lt, dk), lambda bi, i: (bi, 0, 0, 0)),
                  pl.BlockSpec((1, hk, n_chunks, dva, ATT_TK), lambda bi, i: (bi, 0, 0, 0, 0)),
                  pl.BlockSpec((1, ATT_TQ, GROUP_W), lambda bi, i: (bi, i, gate_block)),
                  pl.BlockSpec((GROUP_W, GROUP_W), lambda bi, i: (0, 0))],
        out_specs=pl.BlockSpec((1, ATT_TQ, GROUP_W), lambda bi, i: (bi, i, 0)),
        out_shape=jax.ShapeDtypeStruct((b, lt, GROUP_W), BF16),
        scratch_shapes=[pltpu.VMEM((heads, ATT_TK, ATT_TQ), F32), pltpu.VMEM((heads, ATT_TK, ATT_TQ), F32),
                        pltpu.VMEM((heads, ATT_TK, ATT_TQ), BF16), pltpu.VMEM((heads, ATT_TK, ATT_TQ), BF16),
                        pltpu.VMEM((heads, 1, ATT_TQ), F32), pltpu.VMEM((heads, dva, ATT_TQ), F32)],
        compiler_params=_cparams("parallel", "arbitrary"),
        name="attention",
    )(qt, qt, k, vt, gate_src, jnp.eye(GROUP_W, dtype=BF16))


def _ssm_kernel(*refs, steps, nb, reverse, finish):
    if finish:
        u_ref, b_ref, a_ref, c_ref, yf_ref, d_ref, gw_ref, gb_ref, o_ref, st, car = refs
    else:
        u_ref, b_ref, a_ref, c_ref, o_ref, st, car = refs

    @pl.when(pl.program_id(0) == 0)
    def _():
        car[...] = jnp.zeros_like(car)

    half = st.shape[0] // 2
    halves = (slice(0, half), slice(half, 2 * half))
    for hs in halves:
        st[hs] = jnp.dot(u_ref[hs, 0:GROUP_W], b_ref[...], preferred_element_type=F32)
    n_state = st.shape[1] // 2
    lw = min(SSM_LANE_SPLIT, n_state)
    for c0 in range(0, n_state, lw):
        lr = slice(c0, c0 + lw)
        li = slice(n_state + c0, n_state + c0 + lw)
        ar = jnp.broadcast_to(a_ref[:, lr], (nb, lw))
        ai = jnp.broadcast_to(a_ref[:, li], (nb, lw))

        def body(tt, carry, lr=lr, li=li, ar=ar, ai=ai):
            sr, si = carry
            t = (steps - 1 - tt) if reverse else tt
            rows = pl.ds(pl.multiple_of(t * nb, nb), nb)
            nr = ar * sr - ai * si + st[rows, lr]
            ni = ar * si + ai * sr + st[rows, li]
            st[rows, lr] = nr
            st[rows, li] = ni
            return nr, ni

        sr, si = lax.fori_loop(0, steps, body, (car[:, lr], car[:, li]), unroll=2)
        car[:, lr] = sr
        car[:, li] = si
    ys = [jnp.dot(st[hs].astype(BF16), c_ref[...], preferred_element_type=F32) for hs in halves]
    if not finish:
        for hs, y in zip(halves, ys):
            o_ref[hs] = y
    else:
        zs = [jax.nn.gelu(y + yf_ref[hs] + d_ref[...] * u_ref[hs, 0:GROUP_W].astype(F32), approximate=True)
              for hs, y in zip(halves, ys)]
        gls = [jnp.dot(z.astype(BF16), gw_ref[...], preferred_element_type=F32) + gb_ref[...] for z in zs]
        for hs, z, gl in zip(halves, zs, gls):
            gate = u_ref[hs, GROUP_W:2 * GROUP_W].astype(F32)
            o_ref[hs] = (z * jax.nn.sigmoid(gl) * _silu(gate)).astype(BF16)


def _ssm_direction(u_tm, mats, nb, n_ctx_chunks, reverse, finish_args=None):
    rows, _ = u_tm.shape
    steps = SSM_T
    blk = steps * nb
    n_chunks = rows // blk
    n_state2 = mats[0].shape[1]
    if reverse:
        cidx = lambda i: (jnp.where(i < n_ctx_chunks, n_ctx_chunks - 1 - i, n_chunks - 1 - (i - n_ctx_chunks)), 0)
    else:
        cidx = lambda i: (i, 0)
    const = lambda i: (0, 0)
    in_specs = [pl.BlockSpec((blk, SSM_PACK), cidx)] + [pl.BlockSpec(m.shape, const) for m in mats]
    args = [u_tm, *mats]
    finish = finish_args is not None
    if finish:
        yf, d_skip, glu_w, glu_b = finish_args
        in_specs += [pl.BlockSpec((blk, GROUP_W), cidx), pl.BlockSpec(d_skip.shape, const),
                     pl.BlockSpec(glu_w.shape, const), pl.BlockSpec(glu_b.shape, const)]
        args += [yf, d_skip, glu_w, glu_b]
    return pl.pallas_call(
        functools.partial(_ssm_kernel, steps=steps, nb=nb, reverse=reverse, finish=finish),
        grid=(n_chunks,),
        in_specs=in_specs,
        out_specs=pl.BlockSpec((blk, GROUP_W), cidx),
        out_shape=jax.ShapeDtypeStruct((rows, GROUP_W), BF16 if finish else F32),
        scratch_shapes=[pltpu.VMEM((blk, n_state2), F32), pltpu.VMEM((nb, n_state2), F32)],
        compiler_params=_cparams("arbitrary"),
        name="ssm_rev" if reverse else "ssm_fwd",
    )(*args)


def _ssm_matrices(lam_re, lam_im, log_step, b_re, b_im, c_re, c_im):
    lr, li = lam_re.astype(F32), lam_im.astype(F32)
    step = jnp.exp(log_step.astype(F32))[:, None]
    mag = jnp.exp(lr * step)
    a_re, a_im = mag * jnp.cos(li * step), mag * jnp.sin(li * step)
    den = lr * lr + li * li
    q_re = ((a_re - 1.0) * lr + a_im * li) / den
    q_im = (a_im * lr - (a_re - 1.0) * li) / den
    bb_re, bb_im = b_re.astype(F32), b_im.astype(F32)
    bbar_re = q_re[..., None] * bb_re - q_im[..., None] * bb_im
    bbar_im = q_re[..., None] * bb_im + q_im[..., None] * bb_re
    eye = jnp.eye(SSM_GROUPS, dtype=F32)
    n_in = SSM_GROUPS * SSM_GROUP
    n_state = SSM_GROUPS * SSM_STATE

    def drive(m):
        return jnp.einsum('gph,gk->ghkp', m, eye).reshape(n_in, n_state).astype(BF16)

    def readout(m):
        return jnp.einsum('ghp,gk->gpkh', m, eye).reshape(n_state, n_in).astype(BF16)

    return (jnp.concatenate([drive(bbar_re), drive(bbar_im)], axis=1),
            jnp.concatenate([a_re.reshape(1, n_state), a_im.reshape(1, n_state)], axis=1),
            jnp.concatenate([readout(c_re.astype(F32)), readout(-c_im.astype(F32))], axis=0))


def _hy_pre_kernel(x_ref, prev_ref, next_ref, w_ref, b_ref, v_ref, x1_ref, x2_ref, sg_ref, *, n_tiles):
    i = pl.program_id(1)
    nconv = 3 * HY_W
    x = x_ref[0, :, 0:nconv].astype(F32)
    tm = x.shape[0]
    prev_row = jnp.where(i > 0, prev_ref[0, 7:8, 0:nconv].astype(F32), 0.0)
    next_row = jnp.where(i < n_tiles - 1, next_ref[0, 0:1, 0:nconv].astype(F32), 0.0)
    rid = lax.broadcasted_iota(jnp.int32, x.shape, 0)
    xm = jnp.where(rid == 0, prev_row, pltpu.roll(x, 1, 0))
    xp = jnp.where(rid == tm - 1, next_row, pltpu.roll(x, tm - 1, 0))
    proj = xm * w_ref[0:1] + x * w_ref[1:2] + xp * w_ref[2:3] + b_ref[...]
    gate = x_ref[0, :, nconv:nconv + HY_W].astype(F32)
    for o_ref, val in ((v_ref, proj[:, 0:HY_W]), (x1_ref, proj[:, HY_W:2 * HY_W]),
                       (x2_ref, proj[:, 2 * HY_W:3 * HY_W]), (sg_ref, _silu(gate))):
        o_ref[0] = val.astype(BF16)


def _hy_pre(p_hy, conv_w, conv_b, tile0, n_tiles):
    b = p_hy.shape[0]
    tm = ROW_TILE
    sub = tm // 8
    n = n_tiles * tm
    last_sub = p_hy.shape[1] // 8 - 1
    const2 = lambda bi, i: (0, 0)
    out_spec = pl.BlockSpec((1, tm, HY_W), lambda bi, i: (bi, i, 0))
    out_shape = jax.ShapeDtypeStruct((b, n, HY_W), BF16)
    return pl.pallas_call(
        functools.partial(_hy_pre_kernel, n_tiles=n_tiles),
        grid=(b, n_tiles),
        in_specs=[pl.BlockSpec((1, tm, HY_PACK), lambda bi, i: (bi, i + tile0, 0)),
                  pl.BlockSpec((1, 8, HY_PACK), lambda bi, i: (bi, jnp.maximum((i + tile0) * sub - 1, 0), 0)),
                  pl.BlockSpec((1, 8, HY_PACK), lambda bi, i: (bi, jnp.minimum((i + tile0 + 1) * sub, last_sub), 0)),
                  pl.BlockSpec((3, 3 * HY_W), const2),
                  pl.BlockSpec((1, 3 * HY_W), const2)],
        out_specs=[out_spec] * 4,
        out_shape=[out_shape] * 4,
        compiler_params=_cparams("parallel", "parallel"),
        name="hyena_pre",
    )(p_hy, p_hy, p_hy, conv_w, conv_b)


def _bitrev(p, bits):
    r = 0
    for i in range(bits):
        r = (r << 1) | ((p >> i) & 1)
    return r


@functools.lru_cache(maxsize=None)
def _fft_plan(n):
    r = FFT_R
    nb = n // r
    m = 2 * nb
    big = 2 * n
    bits = int(round(math.log2(nb)))
    assert nb >= 2 and (1 << bits) == nb
    jmap = np.zeros(m, np.int64)
    for p in range(nb):
        jmap[p] = 2 * _bitrev(p, bits)
        jmap[nb + p] = 2 * _bitrev(p, bits) + 1
    k2 = np.arange(r)
    n2 = np.arange(r)
    g = np.zeros((m, 2 * r, 2 * r), np.float32)
    for s in range(m):
        ang = -2.0 * np.pi * (np.outer(k2, n2) / r + np.outer(np.ones(r), n2) * jmap[s] / big)
        gr, gi = np.cos(ang), np.sin(ang)
        g[s] = np.block([[gr, -gi], [gi, gr]])
    half0 = nb // 2
    k = np.arange(half0)
    wnb = np.exp(-2j * np.pi * k / nb)
    wm0 = np.exp(-2j * np.pi * k / m)
    wm1 = np.exp(-2j * np.pi * (k + half0) / m)
    tw_edge = np.stack([wnb.real, wnb.imag, wm0.real, wm0.imag, wm1.real, wm1.imag]).astype(np.float32)
    mids = []
    h = half0 // 2
    while h >= 1:
        kk = np.arange(nb // 2) % h
        w = np.exp(-2j * np.pi * kk / (2 * h))
        mids.append(np.stack([w.real, w.imag]))
        h //= 2
    tw_mid = (np.concatenate(mids, 0) if mids else np.zeros((2, max(nb // 2, 1)))).astype(np.float32)
    freq = (jmap[:, None] + m * k2[None, :])
    return dict(nb=nb, m=m, g=g, tw_edge=tw_edge, tw_mid=tw_mid, n_mid=len(mids), freq=freq)


def _cmul(ar, ai, wr, wi):
    return ar * wr - ai * wi, ar * wi + ai * wr


def _slot_pair(bb, lg, nb):
    half0 = nb // 2
    branch = bb // half0
    bf = bb - branch * half0
    i0 = branch * nb + ((bf >> lg) << (lg + 1)) + (bf & ((1 << lg) - 1))
    return bf, i0, i0 + (1 << lg)


def _block_dft_forward(read, w, tw_edge, tw_mid, nb, n_mid):
    r = FFT_R
    half0 = nb // 2
    re, im = slice(0, r), slice(r, 2 * r)

    def first(k, c):
        wr, wi = tw_edge[0, k], tw_edge[1, k]
        ar, ai = read(k)
        br, bi = read(k + half0)
        w[k, re], w[k, im] = ar + br, ai + bi
        dr, di = _cmul(ar - br, ai - bi, wr, wi)
        w[k + half0, re], w[k + half0, im] = dr, di
        a2r, a2i = _cmul(ar, ai, tw_edge[2, k], tw_edge[3, k])
        b2r, b2i = _cmul(br, bi, tw_edge[4, k], tw_edge[5, k])
        w[nb + k, re], w[nb + k, im] = a2r + b2r, a2i + b2i
        dr, di = _cmul(a2r - b2r, a2i - b2i, wr, wi)
        w[nb + k + half0, re], w[nb + k + half0, im] = dr, di
        return c

    lax.fori_loop(0, half0, first, 0)

    for s in range(n_mid):
        lg = int(round(math.log2(half0))) - 1 - s

        def dif(bb, c, s=s, lg=lg):
            bf, i0, i1 = _slot_pair(bb, lg, nb)
            wr, wi = tw_mid[2 * s, bf], tw_mid[2 * s + 1, bf]
            ar, ai, br, bi = w[i0, re], w[i0, im], w[i1, re], w[i1, im]
            w[i0, re], w[i0, im] = ar + br, ai + bi
            dr, di = _cmul(ar - br, ai - bi, wr, wi)
            w[i1, re], w[i1, im] = dr, di
            return c

        lax.fori_loop(0, nb, dif, 0)


def _hy_conv_kernel(*refs, nb, n_mid, n_post):
    tw_edge, tw_mid, a_ref, bias_ref, kh_ref, g_ref, gi_ref = refs[:7]
    post_refs = refs[7:7 + n_post]
    o_ref, w = refs[7 + n_post], refs[8 + n_post]
    r = FFT_R
    m = 2 * nb
    half0 = nb // 2
    re, im = slice(0, r), slice(r, 2 * r)

    def rows(k):
        return pl.ds(pl.multiple_of(k * r, r), r)

    _block_dft_forward(lambda k: (a_ref[0, rows(k), :].astype(F32), a_ref[1, rows(k), :].astype(F32)),
                       w, tw_edge, tw_mid, nb, n_mid)

    def spectral(t, c):
        slots = [FFT_SLOTS * t + u for u in range(FFT_SLOTS)]
        xs = [jnp.dot(g_ref[s], w[s].astype(BF16), preferred_element_type=F32) for s in slots]
        outs = []
        for s, x in zip(slots, xs):
            yr, yi = _cmul(x[re], x[im], kh_ref[0, s, re, :], kh_ref[0, s, im, :])
            y = jnp.concatenate([yr, yi], axis=0).astype(BF16)
            outs.append(jnp.dot(gi_ref[s], y, preferred_element_type=F32))
        for s, o in zip(slots, outs):
            w[s] = o
        return c

    lax.fori_loop(0, m // FFT_SLOTS, spectral, 0)

    for s in reversed(range(n_mid)):
        lg = int(round(math.log2(half0))) - 1 - s

        def dit(bb, c, s=s, lg=lg):
            bf, i0, i1 = _slot_pair(bb, lg, nb)
            wr, wi = tw_mid[2 * s, bf], -tw_mid[2 * s + 1, bf]
            ar, ai = w[i0, re], w[i0, im]
            br, bi = _cmul(w[i1, re], w[i1, im], wr, wi)
            w[i0, re], w[i0, im] = ar + br, ai + bi
            w[i1, re], w[i1, im] = ar - br, ai - bi
            return c

        lax.fori_loop(0, nb, dit, 0)

    bias = bias_ref[0]

    def emit(k, yr, yi):
        for bsel, y in ((0, yr), (1, yi)):
            a = a_ref[bsel, rows(k), :].astype(F32)
            val = y + bias * a
            for p_ref in post_refs:
                val = val * p_ref[bsel, rows(k), :].astype(F32)
            o_ref[bsel, rows(k), :] = val.astype(o_ref.dtype)

    def last(k, c):
        wr, wi = tw_edge[0, k], -tw_edge[1, k]
        ar, ai = w[k, re], w[k, im]
        br, bi = _cmul(w[k + half0, re], w[k + half0, im], wr, wi)
        cr, ci = w[nb + k, re], w[nb + k, im]
        dr, di = _cmul(w[nb + k + half0, re], w[nb + k + half0, im], wr, wi)
        o0r, o0i = _cmul(cr + dr, ci + di, tw_edge[2, k], -tw_edge[3, k])
        o1r, o1i = _cmul(cr - dr, ci - di, tw_edge[4, k], -tw_edge[5, k])
        emit(k, ar + br + o0r, ai + bi + o0i)
        emit(k + half0, ar - br + o1r, ai - bi + o1i)
        return c

    lax.fori_loop(0, half0, last, 0)


def _hy_conv(a, khat, bias, posts, out_dtype):
    b, n, width = a.shape
    lanes = LANES
    halves = width // lanes
    plan = _fft_plan(n)
    nb, m = plan['nb'], plan['m']
    r = FFT_R
    sig = pl.BlockSpec((2, n, lanes), lambda hf, pr: (pr, 0, hf))
    smem = pl.BlockSpec(memory_space=pltpu.SMEM)
    once = pl.Buffered(1)
    in_specs = [smem, smem, sig,
                pl.BlockSpec((1, 1, lanes), lambda hf, pr: (hf, 0, 0)),
                pl.BlockSpec((1, m, 2 * r, lanes), lambda hf, pr: (hf, 0, 0, 0), pipeline_mode=once),
                pl.BlockSpec((m, 2 * r, 2 * r), lambda hf, pr: (0, 0, 0), pipeline_mode=once),
                pl.BlockSpec((m, 2 * r, 2 * r), lambda hf, pr: (0, 0, 0), pipeline_mode=once)]
    in_specs += [sig] * len(posts)
    g_fwd = jnp.asarray(plan['g'], BF16)
    g_inv = jnp.asarray(np.swapaxes(plan['g'], 1, 2), BF16)
    return pl.pallas_call(
        functools.partial(_hy_conv_kernel, nb=nb, n_mid=plan['n_mid'], n_post=len(posts)),
        grid=(halves, b // 2),
        in_specs=in_specs,
        out_specs=sig,
        out_shape=jax.ShapeDtypeStruct(a.shape, out_dtype),
        scratch_shapes=[pltpu.VMEM((m, 2 * r, lanes), F32)],
        compiler_params=pltpu.CompilerParams(dimension_semantics=("arbitrary", "arbitrary"),
                                             vmem_limit_bytes=HY_VMEM_LIMIT),
        name="hyena_conv",
    )(jnp.asarray(plan['tw_edge']), jnp.asarray(plan['tw_mid']), a, bias, khat, g_fwd, g_inv, *posts)


_HP = lax.Precision.HIGHEST
HY_TAPS_W = HY_ORDER * 2 * HY_W
HY_FEAT = 128


def _hy_taps_kernel(z_ref, dec_ref, w1_ref, b1_ref, f1_ref, w2_ref, b2_ref, f2_ref, w3_ref, o_ref,
                    h_scr, ss_scr):
    phase, i = pl.program_id(0), pl.program_id(1)
    tm = z_ref.shape[0]
    rows = pl.ds(pl.multiple_of(i * tm, tm), tm)

    @pl.when(phase == 0)
    def _():
        @pl.when(i == 0)
        def _():
            ss_scr[...] = jnp.zeros_like(ss_scr)

        h = jnp.dot(z_ref[...], w1_ref[...], precision=_HP, preferred_element_type=F32)
        h = jnp.sin(f1_ref[...] * (h + b1_ref[...]))
        h = jnp.dot(h, w2_ref[...], precision=_HP, preferred_element_type=F32)
        h = jnp.sin(f2_ref[...] * (h + b2_ref[...]))
        dec = dec_ref[...]
        for c0 in range(0, HY_TAPS_W, HY_W):
            t = jnp.dot(h, w3_ref[:, c0:c0 + HY_W], precision=_HP, preferred_element_type=F32) * dec
            h_scr[rows, c0:c0 + HY_W] = t
            ss_scr[:, c0:c0 + HY_W] += jnp.sum(t * t, axis=0, keepdims=True)

    @pl.when(phase == 1)
    def _():
        first_row = lax.broadcasted_iota(jnp.int32, (tm, HY_W), 0) == jnp.where(i == 0, 0, -1)
        for o in range(HY_ORDER):
            c0 = o * 2 * HY_W
            tot = ss_scr[:, c0:c0 + HY_W] + ss_scr[:, c0 + HY_W:c0 + 2 * HY_W]
            inv = lax.rsqrt(tot + EPS)
            o_ref[:, c0:c0 + HY_W] = h_scr[rows, c0:c0 + HY_W] * inv
            o_ref[:, c0 + HY_W:c0 + 2 * HY_W] = jnp.where(first_row, 0.0, h_scr[rows, c0 + HY_W:c0 + 2 * HY_W] * inv)


@functools.lru_cache(maxsize=None)
def _filter_features(n):
    t = np.linspace(0.0, 1.0, n, dtype=np.float32)[:, None]
    omega = (2.0 * np.pi * np.arange(n, dtype=np.float32)[:, None] / n).astype(np.float32)
    bands = np.linspace(1e-4, HY_BANDS - 1, HY_BANDS, dtype=np.float32)[None, :]
    z = np.zeros((n, HY_FEAT), np.float32)
    z[:, 0:1] = t
    z[:, 1:1 + HY_BANDS] = np.cos(bands * omega)
    z[:, 1 + HY_BANDS:HY_EMB] = -np.sin(bands * omega)
    max_decay = math.log(HY_DECAY_TARGET) / HY_FAST_DECAY
    min_decay = math.log(HY_DECAY_TARGET) / HY_SLOW_DECAY
    deltas = np.linspace(min_decay, max_decay, HY_W, dtype=np.float32)
    dec = np.exp(-t * np.abs(deltas)).astype(np.float32)
    return z, dec


def _hy_taps(n, w1, b1, fr1, w2, b2, fr2, w3):
    z, dec = _filter_features(n)
    tm = min(n, 512)
    pad = HY_FEAT - HY_HIDDEN
    w1p = jnp.pad(w1.astype(F32), ((0, HY_FEAT - HY_EMB), (0, pad)))
    w2p = jnp.pad(w2.astype(F32), ((0, pad), (0, pad)))
    w3p = jnp.pad(w3.astype(F32), ((0, pad), (0, 0)))
    vec = lambda v: jnp.pad(v.astype(F32), (0, pad)).reshape(1, HY_FEAT)
    const = lambda ph, i: (0, 0)
    return pl.pallas_call(
        _hy_taps_kernel,
        grid=(2, n // tm),
        in_specs=[pl.BlockSpec((tm, HY_FEAT), lambda ph, i: (i * (1 - ph), 0)),
                  pl.BlockSpec((tm, HY_W), lambda ph, i: (i * (1 - ph), 0)),
                  pl.BlockSpec((HY_FEAT, HY_FEAT), const), pl.BlockSpec((1, HY_FEAT), const),
                  pl.BlockSpec((1, HY_FEAT), const),
                  pl.BlockSpec((HY_FEAT, HY_FEAT), const), pl.BlockSpec((1, HY_FEAT), const),
                  pl.BlockSpec((1, HY_FEAT), const),
                  pl.BlockSpec((HY_FEAT, HY_TAPS_W), const)],
        out_specs=pl.BlockSpec((tm, HY_TAPS_W), lambda ph, i: (i * ph, 0)),
        out_shape=jax.ShapeDtypeStruct((n, HY_TAPS_W), F32),
        scratch_shapes=[pltpu.VMEM((n, HY_TAPS_W), F32), pltpu.VMEM((1, HY_TAPS_W), F32)],
        compiler_params=_cparams("arbitrary", "arbitrary"),
        name="hyena_taps",
    )(jnp.asarray(z), jnp.asarray(dec), w1p, vec(b1), vec(fr1), w2p, vec(b2), vec(fr2), w3p)


def _hy_spec_kernel(tw_edge, tw_mid, f_ref, b_ref, ghi_ref, glo_ref, o_ref, w, *, nb, n_mid, inv_n):
    r = FFT_R
    m = 2 * nb
    re, im = slice(0, r), slice(r, 2 * r)
    zeros = jnp.zeros((r, LANES), F32)

    def rows(k):
        return pl.ds(pl.multiple_of(k * r, r), r)

    def in_block_dft(s):
        x = w[s]
        xh = x.astype(BF16)
        xl = (x - xh.astype(F32)).astype(BF16)
        gh = ghi_ref[s]
        y = jnp.dot(gh, xh, preferred_element_type=F32)
        y += jnp.dot(gh, xl, preferred_element_type=F32)
        y += jnp.dot(glo_ref[s], xh, preferred_element_type=F32)
        return y * inv_n

    _block_dft_forward(lambda k: (f_ref[rows(k), :], zeros), w, tw_edge, tw_mid, nb, n_mid)

    def fwd_part(s, c):
        o_ref[0, 0, s] = in_block_dft(s)
        return c

    lax.fori_loop(0, m, fwd_part, 0)
    _block_dft_forward(lambda k: (b_ref[rows(k), :], zeros), w, tw_edge, tw_mid, nb, n_mid)

    def bwd_part(s, c):
        y = in_block_dft(s)
        o_ref[0, 0, s, re] += y[re]
        o_ref[0, 0, s, im] -= y[im]
        return c

    lax.fori_loop(0, m, bwd_part, 0)


def _hy_filter_spectrum(taps):
    n = taps.shape[0]
    plan = _fft_plan(n)
    nb, m = plan['nb'], plan['m']
    r = FFT_R
    halves = HY_W // LANES
    g = plan['g']
    g_hi = g.astype(jnp.bfloat16)
    g_lo = (g - np.asarray(g_hi, np.float32)).astype(jnp.bfloat16)
    smem = pl.BlockSpec(memory_space=pltpu.SMEM)
    once = pl.Buffered(1)
    table = lambda o, hf: (0, 0, 0)
    return pl.pallas_call(
        functools.partial(_hy_spec_kernel, nb=nb, n_mid=plan['n_mid'], inv_n=1.0 / (2 * n)),
        grid=(HY_ORDER, halves),
        in_specs=[smem, smem,
                  pl.BlockSpec((n, LANES), lambda o, hf: (0, o * 2 * halves + hf)),
                  pl.BlockSpec((n, LANES), lambda o, hf: (0, o * 2 * halves + halves + hf)),
                  pl.BlockSpec((m, 2 * r, 2 * r), table, pipeline_mode=once),
                  pl.BlockSpec((m, 2 * r, 2 * r), table, pipeline_mode=once)],
        out_specs=pl.BlockSpec((1, 1, m, 2 * r, LANES), lambda o, hf: (o, hf, 0, 0, 0)),
        out_shape=jax.ShapeDtypeStruct((HY_ORDER, halves, m, 2 * r, LANES), F32),
        scratch_shapes=[pltpu.VMEM((m, 2 * r, LANES), F32)],
        compiler_params=pltpu.CompilerParams(dimension_semantics=("arbitrary", "arbitrary"),
                                             vmem_limit_bytes=HY_VMEM_LIMIT),
        name="hyena_spectrum",
    )(jnp.asarray(plan['tw_edge']), jnp.asarray(plan['tw_mid']), taps, taps,
      jnp.asarray(g_hi), jnp.asarray(g_lo))


def _pack_w_in(w_in):
    d = w_in.shape[0]
    z = lambda n: jnp.zeros((d, n), w_in.dtype)
    o = 0
    q_lat = w_in[:, o:o + MLA_Q_RANK]; o += MLA_Q_RANK
    kv_lat = w_in[:, o:o + MLA_KV_RANK]; o += MLA_KV_RANK
    k_rope = w_in[:, o:o + MLA_ROPE]; o += MLA_ROPE
    gate = w_in[:, o:o + GROUP_W]; o += GROUP_W
    mla = jnp.concatenate([q_lat, z(256 - MLA_Q_RANK), kv_lat, z(MLA_NOPE), k_rope,
                           z(MLA_DK - MLA_NOPE - MLA_ROPE), gate], axis=1)
    return jnp.concatenate([mla, w_in[:, o:]], axis=1).astype(BF16)


def _pack_mla_up(w_uq, w_ukv):
    dq = MLA_NOPE + MLA_ROPE
    wq = w_uq.reshape(MLA_Q_RANK, MLA_HEADS, dq).transpose(1, 0, 2)
    wq = jnp.pad(wq, ((0, 0), (0, 256 - MLA_Q_RANK), (0, MLA_DK - dq))).astype(BF16)
    wkv = w_ukv.reshape(MLA_KV_RANK, MLA_HEADS, MLA_NOPE + MLA_V).transpose(1, 0, 2)
    wk = jnp.pad(wkv[:, :, :MLA_NOPE], ((0, 0), (0, 0), (0, MLA_DK - MLA_NOPE))).astype(BF16)
    wvt = jnp.swapaxes(wkv[:, :, MLA_NOPE:], 1, 2).astype(BF16)
    return wq, wk, wvt


def kernel(x, c, ctx, c_ctx, w_mod, b_mod, g_pre, g_post, w_in, w_out, mla_g_cq, mla_w_uq, mla_g_ckv, mla_w_ukv, gqa_g_q, gqa_g_k, ssm_lambda_re, ssm_lambda_im, ssm_log_step, ssm_b_re, ssm_b_im, ssm_c_re, ssm_c_im, ssm_d, ssm_glu_w, ssm_glu_b, hy_conv_w, hy_conv_b, hy_f_w1, hy_f_b1, hy_f_freq1, hy_f_w2, hy_f_b2, hy_f_freq2, hy_f_w3, hy_bias):
    b, n_lat, d = x.shape
    n_ctx = ctx.shape[1]
    depth = w_in.shape[0]
    lt = n_ctx + n_lat
    nct = n_ctx // ROW_TILE
    assert n_ctx % ROW_TILE == 0 and n_lat % ROW_TILE == 0 and b % 2 == 0 and b % 8 == 0

    n_cond = -(-(b + 1) // 8) * 8
    cond = jnp.zeros((n_cond, d), F32).at[:b].set(c).at[b].set(c_ctx)
    mod = _modulation(cond, w_mod, b_mod)[:, :b + 1]
    shift, scale, gate = mod[..., :d], mod[..., d:2 * d], mod[..., 2 * d:]

    mla_cos, mla_sin, gqa_cos, gqa_sin = _rope_tables(n_ctx, n_lat)
    nq = GQA_HEADS * GQA_DIM
    ones_bd = jnp.asarray(np.kron(np.eye(GQA_HEADS), np.ones((GQA_DIM, GQA_DIM))), BF16)

    xa = jnp.concatenate([ctx, x], axis=1)
    for l in range(depth):
        sc = (g_pre[l][None, :] * (1.0 + scale[l]))[:, None, :]
        sh = shift[l][:, None, :]
        p_mla, p_gqa, p_ssm, p_hy = _inproj(xa, sc, sh, _pack_w_in(w_in[l]), nct)

        wq, wk, wvt = _pack_mla_up(mla_w_uq[l], mla_w_ukv[l])
        g_cq = jnp.pad(mla_g_cq[l], (0, 256 - MLA_Q_RANK)).reshape(1, 256)
        qt, k, vt = _mla_prep(p_mla, mla_cos, mla_sin, g_cq, mla_g_ckv[l].reshape(1, -1), wq, wk, wvt)
        a_out = _attention(qt, k, vt, p_mla, 2, n_ctx)

        qt, k, vt = _gqa_prep(p_gqa, gqa_cos, gqa_sin, jnp.tile(gqa_g_q[l], GQA_HEADS).reshape(1, nq),
                              jnp.tile(gqa_g_k[l], GQA_KV_HEADS).reshape(1, -1), ones_bd)
        g_out = _attention(qt, k, vt, p_gqa, 2, n_ctx)

        u_tm = jnp.transpose(p_ssm, (1, 0, 2)).reshape(lt * b, SSM_PACK)
        mats = [_ssm_matrices(ssm_lambda_re[l, di], ssm_lambda_im[l, di], ssm_log_step[l, di],
                              ssm_b_re[l, di], ssm_b_im[l, di], ssm_c_re[l, di], ssm_c_im[l, di])
                for di in range(2)]
        n_ctx_chunks = n_ctx // SSM_T
        y_f = _ssm_direction(u_tm, mats[0], b, n_ctx_chunks, False)
        fin = (y_f, ssm_d[l].reshape(1, -1), ssm_glu_w[l].astype(BF16), ssm_glu_b[l].reshape(1, -1))
        s_tm = _ssm_direction(u_tm, mats[1], b, n_ctx_chunks, True, fin)
        s_out = jnp.transpose(s_tm.reshape(lt, b, GROUP_W), (1, 0, 2))

        filt = (hy_f_w1[l], hy_f_b1[l], hy_f_freq1[l], hy_f_w2[l], hy_f_b2[l], hy_f_freq2[l], hy_f_w3[l])
        conv_b = hy_conv_b[l].reshape(1, -1)
        parts = []
        for tile0, n_tiles in ((0, nct), (nct, n_lat // ROW_TILE)):
            n = n_tiles * ROW_TILE
            hv, hx1, hx2, hsg = _hy_pre(p_hy, hy_conv_w[l], conv_b, tile0, n_tiles)
            khat = _hy_filter_spectrum(_hy_taps(n, *filt))
            bias = hy_bias[l].astype(F32).reshape(HY_ORDER, HY_W // LANES, 1, LANES)
            z1 = _hy_conv(hv, khat[0], bias[0], (hx1,), BF16)
            parts.append(_hy_conv(z1, khat[1], bias[1], (hx2, hsg), BF16))
        y_out = jnp.concatenate(parts, axis=1)

        xa = _outproj(a_out, g_out, s_out, y_out, xa, gate[l][:, None, :], g_post[l].reshape(1, d),
                      w_out[l].astype(BF16), nct, tile0=nct if l == depth - 1 else 0)
    return xa
```

```python
import functools
import math

import numpy as np
import jax
import jax.numpy as jnp
from jax import lax
from jax.experimental import pallas as pl
from jax.experimental.pallas import tpu as pltpu

F32 = jnp.float32
BF16 = jnp.bfloat16

GRID_W = 64
ROPE_BASE = 10000.0
EPS = 1e-6
GROUP_W = 256
MLA_HEADS, MLA_NOPE, MLA_ROPE, MLA_V = 4, 64, 32, 64
MLA_Q_RANK, MLA_KV_RANK = 192, 128
GQA_HEADS, GQA_KV_HEADS, GQA_DIM = 4, 2, 64
SSM_GROUPS, SSM_GROUP, SSM_STATE = 16, 16, 64
HY_W, HY_ORDER, HY_EMB, HY_HIDDEN = 256, 2, 33, 64
HY_BANDS = (HY_EMB - 1) // 2
HY_FAST_DECAY, HY_SLOW_DECAY, HY_DECAY_TARGET = 0.3, 1.5, 1e-2

MLA_PACK = 768
GQA_PACK = 768
SSM_PACK = 512
HY_PACK = 1024
IN_PACK = MLA_PACK + GQA_PACK + SSM_PACK + HY_PACK
MLA_DK = 128

LANES = 128
ROW_TILE = 256
ATT_TQ = 256
ATT_TK = 256
ATT_UNROLLS = (16, 8, 4, 2)
ONES_ROWS = 16
SSM_T = 128
SSM_LANE_SPLIT = 512
FFT_R = 128
FFT_SLOTS = 4
VMEM_LIMIT = 52 * 1024 * 1024
HY_VMEM_LIMIT = 58 * 1024 * 1024


def _cparams(*sem):
    return pltpu.CompilerParams(dimension_semantics=sem, vmem_limit_bytes=VMEM_LIMIT)


def _silu(x):
    return x * jax.nn.sigmoid(x)


def _mod_kernel(c_ref, w_ref, b_ref, o_ref):
    c = c_ref[...]
    s = _silu(c).astype(BF16)
    o_ref[0] = jnp.dot(s, w_ref[0].astype(BF16), preferred_element_type=F32) + b_ref[0]


def _modulation(cond, w_mod, b_mod):
    depth, d, n3 = w_mod.shape
    r = cond.shape[0]
    tn = 512
    return pl.pallas_call(
        _mod_kernel,
        grid=(depth, n3 // tn),
        in_specs=[pl.BlockSpec((r, d), lambda l, j: (0, 0)),
                  pl.BlockSpec((1, d, tn), lambda l, j: (l, 0, j)),
                  pl.BlockSpec((1, 1, tn), lambda l, j: (l, 0, j))],
        out_specs=pl.BlockSpec((1, r, tn), lambda l, j: (l, 0, j)),
        out_shape=jax.ShapeDtypeStruct((depth, r, n3), F32),
        compiler_params=_cparams("parallel", "parallel"),
        name="modulation",
    )(cond, w_mod, b_mod.reshape(depth, 1, n3))


def _outproj_kernel(a_ref, g_ref, s_ref, y_ref, x_ref, gt_ref, gp_ref, w_ref, o_ref):
    half = a_ref.shape[1] // 2
    for hs in (slice(0, half), slice(half, 2 * half)):
        acc = jnp.dot(a_ref[0, hs], w_ref[0:GROUP_W], preferred_element_type=F32)
        acc += jnp.dot(g_ref[0, hs], w_ref[GROUP_W:2 * GROUP_W], preferred_element_type=F32)
        acc += jnp.dot(s_ref[0, hs], w_ref[2 * GROUP_W:3 * GROUP_W], preferred_element_type=F32)
        acc += jnp.dot(y_ref[0, hs], w_ref[3 * GROUP_W:4 * GROUP_W], preferred_element_type=F32)
        ms = jnp.mean(acc * acc, axis=-1, keepdims=True)
        o_ref[0, hs] = x_ref[0, hs] + gt_ref[0] * (acc * lax.rsqrt(ms + EPS) * gp_ref[...])


def _outproj(a, g, s, y, xa, gate, g_post, w, nct, tile0=0):
    b, lt, d = xa.shape
    tm = ROW_TILE

    def mod_idx(bi, i):
        return (jnp.where(i + tile0 < nct, b, bi), 0, 0)

    row = lambda bi, i: (bi, i + tile0, 0)
    return pl.pallas_call(
        _outproj_kernel,
        grid=(b, lt // tm - tile0),
        in_specs=[pl.BlockSpec((1, tm, GROUP_W), row)] * 4 + [
            pl.BlockSpec((1, tm, d), row),
            pl.BlockSpec((1, 1, d), mod_idx),
            pl.BlockSpec((1, d), lambda bi, i: (0, 0)),
            pl.BlockSpec((4 * GROUP_W, d), lambda bi, i: (0, 0))],
        out_specs=pl.BlockSpec((1, tm, d), lambda bi, i: (bi, i, 0)),
        out_shape=jax.ShapeDtypeStruct((b, lt - tile0 * tm, d), F32),
        compiler_params=_cparams("parallel", "parallel"),
        name="outproj",
    )(a, g, s, y, xa, gate, g_post, w)


def _rope(x, cos, sin, shift):
    w = x.shape[-1]
    lane = lax.broadcasted_iota(jnp.int32, x.shape, 1)
    first = (lane & shift) == 0
    swapped = jnp.where(first, -pltpu.roll(x, w - shift, 1), pltpu.roll(x, shift, 1))
    return x * cos + swapped * sin


def _rope_tables(n_ctx, n_lat):
    t = np.arange(n_lat)
    row = (t // GRID_W).astype(np.float64)
    col = (t % GRID_W).astype(np.float64)

    def block(pos, h):
        inv = ROPE_BASE ** (-np.arange(0, h, 2, dtype=np.float64) / h)
        ang = (pos[:, None].astype(np.float32) * inv[None, :].astype(np.float32)).astype(np.float32)
        c, s = np.cos(ang), np.sin(ang)
        return np.concatenate([c, c], -1), np.concatenate([s, s], -1)

    def full(h, lead, width):
        cr, sr = block(row, h)
        cc, sc = block(col, h)
        cos = np.ones((n_ctx + n_lat, width), np.float32)
        sin = np.zeros((n_ctx + n_lat, width), np.float32)
        cos[n_ctx:, lead:lead + 2 * h] = np.concatenate([cr, cc], -1)
        sin[n_ctx:, lead:lead + 2 * h] = np.concatenate([sr, sc], -1)
        return cos, sin

    mc, ms = full(MLA_ROPE // 2, MLA_NOPE, MLA_DK)
    gc, gs = full(GQA_DIM // 2, 0, GQA_DIM)
    reps = GQA_HEADS
    return (jnp.asarray(mc), jnp.asarray(ms),
            jnp.asarray(np.tile(gc, (1, reps))), jnp.asarray(np.tile(gs, (1, reps))))


_NT = (((1,), (1,)), ((), ()))
_TN = (((0,), (0,)), ((), ()))
LOG2E = math.log2(math.e)


def _mla_prep_body(p, cos_ref, sin_ref, gq_ref, gkv_ref, wq_ref, wk_ref, wvt_ref, eye_ref,
                   qt_ref, k_ref, vt_ref):
    cos, sin = cos_ref[...], sin_ref[...]
    ql = p[:, 0:256]
    rq = lax.rsqrt(jnp.sum(ql * ql, axis=-1, keepdims=True) * (1.0 / MLA_Q_RANK) + EPS)
    qn = (ql * rq * gq_ref[...]).astype(BF16)
    kvl = p[:, 256:384]
    rk = lax.rsqrt(jnp.mean(kvl * kvl, axis=-1, keepdims=True) + EPS)
    kvn = (kvl * rk * gkv_ref[...]).astype(BF16)
    k_rope = _rope(p[:, 384:512], cos, sin, MLA_ROPE // 4)
    scale = (MLA_NOPE + MLA_ROPE) ** -0.5 * LOG2E
    heads = range(MLA_HEADS)
    qs = [jnp.dot(qn, wq_ref[h], preferred_element_type=F32) for h in heads]
    ks = [jnp.dot(kvn, wk_ref[h], preferred_element_type=F32) for h in heads]
    vts = [lax.dot_general(wvt_ref[h], kvn, _NT, preferred_element_type=F32) for h in heads]
    for h in heads:
        k_ref[0, h] = (ks[h] + k_rope).astype(BF16)
        vt_ref[0, h, 0, 0:MLA_V] = vts[h].astype(BF16)
        vt_ref[0, h, 0, MLA_V:MLA_V + ONES_ROWS] = jnp.ones((ONES_ROWS, kvn.shape[0]), BF16)
    qr = [(_rope(q, cos, sin, MLA_ROPE // 4) * scale).astype(BF16) for q in qs]
    for h in heads:
        qt_ref[0, h] = lax.dot_general(eye_ref[...], qr[h], _NT, preferred_element_type=F32).astype(BF16)


def _head_mean_sq(x, ones_bd):
    sq = x * x
    hi = sq.astype(BF16)
    lo = (sq - hi.astype(F32)).astype(BF16)
    s = jnp.dot(hi, ones_bd, preferred_element_type=F32) + jnp.dot(lo, ones_bd, preferred_element_type=F32)
    return s * (1.0 / GQA_DIM)


def _gqa_prep_body(p, cos_ref, sin_ref, gq_ref, gk_ref, ones_ref, eye_ref, qt_ref, k_ref, vt_ref):
    cos, sin = cos_ref[...], sin_ref[...]
    nq = GQA_HEADS * GQA_DIM
    nk = GQA_KV_HEADS * GQA_DIM
    q = p[:, 0:nq]
    k = p[:, nq:nq + nk]
    v = p[:, nq + nk:nq + 2 * nk].astype(BF16)
    q_ms = _head_mean_sq(q, ones_ref[...])
    k_ms = _head_mean_sq(k, ones_ref[0:nk, 0:nk])
    for h in range(GQA_KV_HEADS):
        sel = eye_ref[h * GQA_DIM:(h + 1) * GQA_DIM, 0:nk]
        vt_ref[0, h, 0, 0:GQA_DIM] = lax.dot_general(sel, v, _NT, preferred_element_type=F32).astype(BF16)
        vt_ref[0, h, 0, GQA_DIM:GQA_DIM + ONES_ROWS] = jnp.ones((ONES_ROWS, v.shape[0]), BF16)
    qn = q * lax.rsqrt(q_ms + EPS) * gq_ref[...]
    qr = (_rope(qn, cos, sin, GQA_DIM // 4) * (GQA_DIM ** -0.5 * LOG2E)).astype(BF16)
    kn = k * lax.rsqrt(k_ms + EPS) * gk_ref[...]
    kr = _rope(kn, cos[:, 0:nk], sin[:, 0:nk], GQA_DIM // 4)
    for h in range(GQA_HEADS):
        sel = eye_ref[h * GQA_DIM:(h + 1) * GQA_DIM, :]
        qt_ref[0, h] = lax.dot_general(sel, qr, _NT, preferred_element_type=F32).astype(BF16)
    for h in range(GQA_KV_HEADS):
        k_ref[0, h] = kr[:, h * GQA_DIM:(h + 1) * GQA_DIM].astype(BF16)


def _inproj_kernel(x_ref, sc_ref, sh_ref, w_ref,
                   mcos_ref, msin_ref, gcq_ref, gckv_ref, wq_ref, wk_ref, wvt_ref, eye_m_ref,
                   gcos_ref, gsin_ref, gq_ref, gk_ref, ones_ref, eye_g_ref,
                   gate_m_ref, gate_g_ref, ssm_ref, hy_ref,
                   mqt_ref, mk_ref, mvt_ref, gqt_ref, gk_out_ref, gvt_ref):
    x = x_ref[0]
    ms = jnp.mean(x * x, axis=-1, keepdims=True)
    h = x * lax.rsqrt(ms + EPS) * sc_ref[0] + sh_ref[0]
    hb = h.astype(BF16)

    def proj(c0, width):
        return jnp.dot(hb, w_ref[:, c0:c0 + width], preferred_element_type=F32)

    qkv = 2 * GROUP_W
    p_mla = proj(0, qkv)
    p_gqa = proj(MLA_PACK, qkv)
    gate_m_ref[0] = proj(qkv, GROUP_W).astype(BF16)
    gate_g_ref[0] = proj(MLA_PACK + qkv, GROUP_W).astype(BF16)
    ssm_ref[0] = proj(MLA_PACK + GQA_PACK, SSM_PACK).astype(BF16)
    hy_ref[0] = proj(MLA_PACK + GQA_PACK + SSM_PACK, HY_PACK).astype(BF16)
    _mla_prep_body(p_mla, mcos_ref, msin_ref, gcq_ref, gckv_ref, wq_ref, wk_ref, wvt_ref, eye_m_ref,
                   mqt_ref, mk_ref, mvt_ref)
    _gqa_prep_body(p_gqa, gcos_ref, gsin_ref, gq_ref, gk_ref, ones_ref, eye_g_ref,
                   gqt_ref, gk_out_ref, gvt_ref)


def _inproj(xa, scale, shift, w, nct, mla_args, gqa_args):
    b, lt, d = xa.shape
    tm = ROW_TILE
    nq = GQA_HEADS * GQA_DIM
    nk = GQA_KV_HEADS * GQA_DIM
    n_tiles = lt // tm

    def mod_idx(bi, i):
        return (jnp.where(i < nct, b, bi), 0, 0)

    const2 = lambda bi, i: (0, 0)
    const3 = lambda bi, i: (0, 0, 0)
    row = lambda bi, i: (bi, i, 0)
    pos = lambda bi, i: (i, 0)
    qt_idx = lambda bi, i: (bi, 0, 0, i)
    k_idx = lambda bi, i: (bi, 0, i, 0)
    vt_idx = lambda bi, i: (bi, 0, i, 0, 0)
    in_specs = [pl.BlockSpec((1, tm, d), row), pl.BlockSpec((1, 1, d), mod_idx), pl.BlockSpec((1, 1, d), mod_idx),
                pl.BlockSpec((d, IN_PACK), const2),
                pl.BlockSpec((tm, MLA_DK), pos), pl.BlockSpec((tm, MLA_DK), pos),
                pl.BlockSpec((1, 256), const2), pl.BlockSpec((1, MLA_KV_RANK), const2),
                pl.BlockSpec((MLA_HEADS, 256, MLA_DK), const3),
                pl.BlockSpec((MLA_HEADS, MLA_KV_RANK, MLA_DK), const3),
                pl.BlockSpec((MLA_HEADS, MLA_V, MLA_KV_RANK), const3),
                pl.BlockSpec((MLA_DK, MLA_DK), const2),
                pl.BlockSpec((tm, nq), pos), pl.BlockSpec((tm, nq), pos),
                pl.BlockSpec((1, nq), const2), pl.BlockSpec((1, nk), const2),
                pl.BlockSpec((nq, nq), const2), pl.BlockSpec((nq, nq), const2)]
    out_specs = [pl.BlockSpec((1, tm, GROUP_W), row), pl.BlockSpec((1, tm, GROUP_W), row),
                 pl.BlockSpec((1, tm, SSM_PACK), row), pl.BlockSpec((1, tm, HY_PACK), row),
                 pl.BlockSpec((1, MLA_HEADS, MLA_DK, tm), qt_idx),
                 pl.BlockSpec((1, MLA_HEADS, tm, MLA_DK), k_idx),
                 pl.BlockSpec((1, MLA_HEADS, 1, MLA_V + ONES_ROWS, tm), vt_idx),
                 pl.BlockSpec((1, GQA_HEADS, GQA_DIM, tm), qt_idx),
                 pl.BlockSpec((1, GQA_KV_HEADS, tm, GQA_DIM), k_idx),
                 pl.BlockSpec((1, GQA_KV_HEADS, 1, GQA_DIM + ONES_ROWS, tm), vt_idx)]
    out_shape = [jax.ShapeDtypeStruct((b, lt, GROUP_W), BF16), jax.ShapeDtypeStruct((b, lt, GROUP_W), BF16),
                 jax.ShapeDtypeStruct((b, lt, SSM_PACK), BF16), jax.ShapeDtypeStruct((b, lt, HY_PACK), BF16),
                 jax.ShapeDtypeStruct((b, MLA_HEADS, MLA_DK, lt), BF16),
                 jax.ShapeDtypeStruct((b, MLA_HEADS, lt, MLA_DK), BF16),
                 jax.ShapeDtypeStruct((b, MLA_HEADS, n_tiles, MLA_V + ONES_ROWS, tm), BF16),
                 jax.ShapeDtypeStruct((b, GQA_HEADS, GQA_DIM, lt), BF16),
                 jax.ShapeDtypeStruct((b, GQA_KV_HEADS, lt, GQA_DIM), BF16),
                 jax.ShapeDtypeStruct((b, GQA_KV_HEADS, n_tiles, GQA_DIM + ONES_ROWS, tm), BF16)]
    outs = pl.pallas_call(
        _inproj_kernel,
        grid=(b, n_tiles),
        in_specs=in_specs,
        out_specs=out_specs,
        out_shape=out_shape,
        compiler_params=_cparams("parallel", "parallel"),
        name="inproj",
    )(xa, scale, shift, w, *mla_args, jnp.eye(MLA_DK, dtype=BF16), *gqa_args, jnp.eye(nq, dtype=BF16))
    return outs[0], outs[1], outs[2], outs[3], tuple(outs[4:7]), tuple(outs[7:10])


def _attn_kernel(qt_ref, qn_ref, k_ref, vt_ref, gate_ref, eye_ref, o_ref, sa_ref, sb_ref, pa_ref, pb_ref,
                 cm_ref, acc_ref, *, heads, group, nct_q, ctx_chunks, all_chunks, unroll):
    i = pl.program_id(1)
    n_loops = jnp.where(i < nct_q, (ctx_chunks - 1) // unroll, (all_chunks - 1) // unroll)
    tq = qt_ref.shape[3]
    dva = vt_ref.shape[3]
    dv = dva - ONES_ROWS

    def score(j, h, dst, q_ref=qt_ref):
        rows = pl.ds(pl.multiple_of(j * ATT_TK, ATT_TK), ATT_TK)
        s = jnp.dot(k_ref[0, h // group, rows, :], q_ref[0, h], preferred_element_type=F32)
        dst[h] = s
        return jnp.max(s, axis=0, keepdims=True)

    def value(j, h, p_ref):
        return jnp.dot(vt_ref[0, h // group, j], p_ref[h], preferred_element_type=F32)

    def step(j, carry, s_cur, s_nxt, p_prev, p_cur):
        state, cmax = carry
        new, nmax = [], []
        for h in range(heads):
            pv = value(jnp.maximum(j - 1, 0), h, p_prev)
            if s_nxt is not None:
                nmax.append(score(j + 1, h, s_nxt))
            m = state[h]
            m_new = jnp.maximum(m, cmax[h])
            alpha = jnp.exp2(m - m_new)
            p_cur[h] = jnp.exp2(s_cur[h] - m_new).astype(BF16)
            acc_ref[h] = alpha * (acc_ref[h] + pv)
            new.append(m_new)
        return tuple(new), tuple(nmax)

    bufs = ((sa_ref, sb_ref, pb_ref, pa_ref), (sb_ref, sa_ref, pa_ref, pb_ref))

    def body(t, carry):
        for u in range(unroll):
            carry = step(unroll * t + u, carry, *bufs[u % 2])
        return carry

    @pl.when(i == 0)
    def _():
        for h in range(heads):
            cm_ref[h] = score(0, h, sa_ref)

    cmax0 = tuple(cm_ref[h] for h in range(heads))
    pb_ref[...] = jnp.zeros_like(pb_ref)
    acc_ref[...] = jnp.zeros_like(acc_ref)
    init = tuple(jnp.full((1, tq), -jnp.inf, F32) for _ in range(heads))
    carry = lax.fori_loop(0, n_loops, body, (init, cmax0))
    last = unroll * n_loops
    step(last, carry, sa_ref, None, pb_ref, pa_ref)
    for h in range(heads):
        cm_ref[h] = score(0, h, sa_ref, qn_ref)
    outs = []
    for h in range(heads):
        acc = acc_ref[h] + value(last, h, pa_ref)
        outs.append((acc[0:dv] * (1.0 / acc[dv:dv + 1])).astype(BF16))
    out = lax.dot_general(jnp.concatenate(outs, axis=0), eye_ref[...], _TN, preferred_element_type=F32)
    g = gate_ref[0].astype(F32)
    o_ref[0] = (out * _silu(g)).astype(BF16)


def _attention(qt, k, vt, gate_src, gate_block, n_ctx):
    b, heads, dk, lt = qt.shape
    hk, n_chunks, dva = k.shape[1], vt.shape[2], vt.shape[3]
    assert heads * (dva - ONES_ROWS) == GROUP_W
    n_tiles = lt // ATT_TQ
    ctx_chunks, all_chunks = n_ctx // ATT_TK, lt // ATT_TK
    unroll = next(u for u in ATT_UNROLLS if (ctx_chunks - 1) % u == 0 and (all_chunks - 1) % u == 0)
    kern = functools.partial(_attn_kernel, heads=heads, group=heads // hk, nct_q=n_ctx // ATT_TQ,
                             ctx_chunks=ctx_chunks, all_chunks=all_chunks, unroll=unroll)
    return pl.pallas_call(
        kern,
        grid=(b, n_tiles),
        in_specs=[pl.BlockSpec((1, heads, dk, ATT_TQ), lambda bi, i: (bi, 0, 0, i)),
                  pl.BlockSpec((1, heads, dk, ATT_TQ), lambda bi, i: (bi, 0, 0, jnp.minimum(i + 1, n_tiles - 1))),
                  pl.BlockSpec((1, hk, lt, dk), lambda bi, i: (bi, 0, 0, 0)),
                  pl.BlockSpec((1, hk, n_chunks, dva, ATT_TK), lambda bi, i: (bi, 0, 0, 0, 0)),
                  pl.BlockSpec((1, ATT_TQ, GROUP_W), lambda bi, i: (bi, i, gate_block)),
                  pl.BlockSpec((GROUP_W, GROUP_W), lambda bi, i: (0, 0))],
        out_specs=pl.BlockSpec((1, ATT_TQ, GROUP_W), lambda bi, i: (bi, i, 0)),
        out_shape=jax.ShapeDtypeStruct((b, lt, GROUP_W), BF16),
        scratch_shapes=[pltpu.VMEM((heads, ATT_TK, ATT_TQ), F32), pltpu.VMEM((heads, ATT_TK, ATT_TQ), F32),
                        pltpu.VMEM((heads, ATT_TK, ATT_TQ), BF16), pltpu.VMEM((heads, ATT_TK, ATT_TQ), BF16),
                        pltpu.VMEM((heads, 1, ATT_TQ), F32), pltpu.VMEM((heads, dva, ATT_TQ), F32)],
        compiler_params=_cparams("parallel", "arbitrary"),
        name="attention",
    )(qt, qt, k, vt, gate_src, jnp.eye(GROUP_W, dtype=BF16))


def _ssm_kernel(*refs, steps, nb, reverse, finish):
    if finish:
        u_ref, b_ref, a_ref, c_ref, yf_ref, d_ref, gw_ref, gb_ref, o_ref, st, car = refs
    else:
        u_ref, b_ref, a_ref, c_ref, o_ref, st, car = refs

    @pl.when(pl.program_id(0) == 0)
    def _():
        car[...] = jnp.zeros_like(car)

    half = st.shape[0] // 2
    halves = (slice(0, half), slice(half, 2 * half))
    for hs in halves:
        st[hs] = jnp.dot(u_ref[hs, 0:GROUP_W], b_ref[...], preferred_element_type=F32)
    n_state = st.shape[1] // 2
    lw = min(SSM_LANE_SPLIT, n_state)
    for c0 in range(0, n_state, lw):
        lr = slice(c0, c0 + lw)
        li = slice(n_state + c0, n_state + c0 + lw)
        ar = jnp.broadcast_to(a_ref[:, lr], (nb, lw))
        ai = jnp.broadcast_to(a_ref[:, li], (nb, lw))

        def body(tt, carry, lr=lr, li=li, ar=ar, ai=ai):
            sr, si = carry
            t = (steps - 1 - tt) if reverse else tt
            rows = pl.ds(pl.multiple_of(t * nb, nb), nb)
            nr = ar * sr - ai * si + st[rows, lr]
            ni = ar * si + ai * sr + st[rows, li]
            st[rows, lr] = nr
            st[rows, li] = ni
            return nr, ni

        sr, si = lax.fori_loop(0, steps, body, (car[:, lr], car[:, li]), unroll=2)
        car[:, lr] = sr
        car[:, li] = si
    ys = [jnp.dot(st[hs].astype(BF16), c_ref[...], preferred_element_type=F32) for hs in halves]
    if not finish:
        for hs, y in zip(halves, ys):
            o_ref[hs] = y
    else:
        zs = [jax.nn.gelu(y + yf_ref[hs] + d_ref[...] * u_ref[hs, 0:GROUP_W].astype(F32), approximate=True)
              for hs, y in zip(halves, ys)]
        gls = [jnp.dot(z.astype(BF16), gw_ref[...], preferred_element_type=F32) + gb_ref[...] for z in zs]
        for hs, z, gl in zip(halves, zs, gls):
            gate = u_ref[hs, GROUP_W:2 * GROUP_W].astype(F32)
            o_ref[hs] = (z * jax.nn.sigmoid(gl) * _silu(gate)).astype(BF16)


def _ssm_direction(u_tm, mats, nb, n_ctx_chunks, reverse, finish_args=None):
    rows, _ = u_tm.shape
    steps = SSM_T
    blk = steps * nb
    n_chunks = rows // blk
    n_state2 = mats[0].shape[1]
    if reverse:
        cidx = lambda i: (jnp.where(i < n_ctx_chunks, n_ctx_chunks - 1 - i, n_chunks - 1 - (i - n_ctx_chunks)), 0)
    else:
        cidx = lambda i: (i, 0)
    const = lambda i: (0, 0)
    in_specs = [pl.BlockSpec((blk, SSM_PACK), cidx)] + [pl.BlockSpec(m.shape, const) for m in mats]
    args = [u_tm, *mats]
    finish = finish_args is not None
    if finish:
        yf, d_skip, glu_w, glu_b = finish_args
        in_specs += [pl.BlockSpec((blk, GROUP_W), cidx), pl.BlockSpec(d_skip.shape, const),
                     pl.BlockSpec(glu_w.shape, const), pl.BlockSpec(glu_b.shape, const)]
        args += [yf, d_skip, glu_w, glu_b]
    return pl.pallas_call(
        functools.partial(_ssm_kernel, steps=steps, nb=nb, reverse=reverse, finish=finish),
        grid=(n_chunks,),
        in_specs=in_specs,
        out_specs=pl.BlockSpec((blk, GROUP_W), cidx),
        out_shape=jax.ShapeDtypeStruct((rows, GROUP_W), BF16 if finish else F32),
        scratch_shapes=[pltpu.VMEM((blk, n_state2), F32), pltpu.VMEM((nb, n_state2), F32)],
        compiler_params=_cparams("arbitrary"),
        name="ssm_rev" if reverse else "ssm_fwd",
    )(*args)


def _ssm_matrices(lam_re, lam_im, log_step, b_re, b_im, c_re, c_im):
    lr, li = lam_re.astype(F32), lam_im.astype(F32)
    step = jnp.exp(log_step.astype(F32))[:, None]
    mag = jnp.exp(lr * step)
    a_re, a_im = mag * jnp.cos(li * step), mag * jnp.sin(li * step)
    den = lr * lr + li * li
    q_re = ((a_re - 1.0) * lr + a_im * li) / den
    q_im = (a_im * lr - (a_re - 1.0) * li) / den
    bb_re, bb_im = b_re.astype(F32), b_im.astype(F32)
    bbar_re = q_re[..., None] * bb_re - q_im[..., None] * bb_im
    bbar_im = q_re[..., None] * bb_im + q_im[..., None] * bb_re
    eye = jnp.eye(SSM_GROUPS, dtype=F32)
    n_in = SSM_GROUPS * SSM_GROUP
    n_state = SSM_GROUPS * SSM_STATE

    def drive(m):
        return jnp.einsum('gph,gk->ghkp', m, eye).reshape(n_in, n_state).astype(BF16)

    def readout(m):
        return jnp.einsum('ghp,gk->gpkh', m, eye).reshape(n_state, n_in).astype(BF16)

    return (jnp.concatenate([drive(bbar_re), drive(bbar_im)], axis=1),
            jnp.concatenate([a_re.reshape(1, n_state), a_im.reshape(1, n_state)], axis=1),
            jnp.concatenate([readout(c_re.astype(F32)), readout(-c_im.astype(F32))], axis=0))


def _hy_pre_kernel(x_ref, prev_ref, next_ref, w_ref, b_ref, v_ref, x1_ref, x2_ref, sg_ref, *, n_tiles):
    i = pl.program_id(1)
    nconv = 3 * HY_W
    x = x_ref[0, :, 0:nconv].astype(F32)
    tm = x.shape[0]
    prev_row = jnp.where(i > 0, prev_ref[0, 7:8, 0:nconv].astype(F32), 0.0)
    next_row = jnp.where(i < n_tiles - 1, next_ref[0, 0:1, 0:nconv].astype(F32), 0.0)
    rid = lax.broadcasted_iota(jnp.int32, x.shape, 0)
    xm = jnp.where(rid == 0, prev_row, pltpu.roll(x, 1, 0))
    xp = jnp.where(rid == tm - 1, next_row, pltpu.roll(x, tm - 1, 0))
    proj = xm * w_ref[0:1] + x * w_ref[1:2] + xp * w_ref[2:3] + b_ref[...]
    gate = x_ref[0, :, nconv:nconv + HY_W].astype(F32)
    for o_ref, val in ((v_ref, proj[:, 0:HY_W]), (x1_ref, proj[:, HY_W:2 * HY_W]),
                       (x2_ref, proj[:, 2 * HY_W:3 * HY_W]), (sg_ref, _silu(gate))):
        o_ref[0] = val.astype(BF16)


def _hy_pre(p_hy, conv_w, conv_b, tile0, n_tiles):
    b = p_hy.shape[0]
    tm = ROW_TILE
    sub = tm // 8
    n = n_tiles * tm
    last_sub = p_hy.shape[1] // 8 - 1
    const2 = lambda bi, i: (0, 0)
    out_spec = pl.BlockSpec((1, tm, HY_W), lambda bi, i: (bi, i, 0))
    out_shape = jax.ShapeDtypeStruct((b, n, HY_W), BF16)
    return pl.pallas_call(
        functools.partial(_hy_pre_kernel, n_tiles=n_tiles),
        grid=(b, n_tiles),
        in_specs=[pl.BlockSpec((1, tm, HY_PACK), lambda bi, i: (bi, i + tile0, 0)),
                  pl.BlockSpec((1, 8, HY_PACK), lambda bi, i: (bi, jnp.maximum((i + tile0) * sub - 1, 0), 0)),
                  pl.BlockSpec((1, 8, HY_PACK), lambda bi, i: (bi, jnp.minimum((i + tile0 + 1) * sub, last_sub), 0)),
                  pl.BlockSpec((3, 3 * HY_W), const2),
                  pl.BlockSpec((1, 3 * HY_W), const2)],
        out_specs=[out_spec] * 4,
        out_shape=[out_shape] * 4,
        compiler_params=_cparams("parallel", "parallel"),
        name="hyena_pre",
    )(p_hy, p_hy, p_hy, conv_w, conv_b)


def _bitrev(p, bits):
    r = 0
    for i in range(bits):
        r = (r << 1) | ((p >> i) & 1)
    return r


@functools.lru_cache(maxsize=None)
def _fft_plan(n):
    r = FFT_R
    nb = n // r
    m = 2 * nb
    big = 2 * n
    bits = int(round(math.log2(nb)))
    assert nb >= 2 and (1 << bits) == nb
    jmap = np.zeros(m, np.int64)
    for p in range(nb):
        jmap[p] = 2 * _bitrev(p, bits)
        jmap[nb + p] = 2 * _bitrev(p, bits) + 1
    k2 = np.arange(r)
    n2 = np.arange(r)
    g = np.zeros((m, 2 * r, 2 * r), np.float32)
    for s in range(m):
        ang = -2.0 * np.pi * (np.outer(k2, n2) / r + np.outer(np.ones(r), n2) * jmap[s] / big)
        gr, gi = np.cos(ang), np.sin(ang)
        g[s] = np.block([[gr, -gi], [gi, gr]])
    half0 = nb // 2
    k = np.arange(half0)
    wnb = np.exp(-2j * np.pi * k / nb)
    wm0 = np.exp(-2j * np.pi * k / m)
    wm1 = np.exp(-2j * np.pi * (k + half0) / m)
    tw_edge = np.stack([wnb.real, wnb.imag, wm0.real, wm0.imag, wm1.real, wm1.imag]).astype(np.float32)
    mids = []
    h = half0 // 2
    while h >= 1:
        kk = np.arange(nb // 2) % h
        w = np.exp(-2j * np.pi * kk / (2 * h))
        mids.append(np.stack([w.real, w.imag]))
        h //= 2
    tw_mid = (np.concatenate(mids, 0) if mids else np.zeros((2, max(nb // 2, 1)))).astype(np.float32)
    freq = (jmap[:, None] + m * k2[None, :])
    return dict(nb=nb, m=m, g=g, tw_edge=tw_edge, tw_mid=tw_mid, n_mid=len(mids), freq=freq)


def _cmul(ar, ai, wr, wi):
    return ar * wr - ai * wi, ar * wi + ai * wr


def _slot_pair(bb, lg, nb):
    half0 = nb // 2
    branch = bb // half0
    bf = bb - branch * half0
    i0 = branch * nb + ((bf >> lg) << (lg + 1)) + (bf & ((1 << lg) - 1))
    return bf, i0, i0 + (1 << lg)


def _block_dft_forward(read, w, tw_edge, tw_mid, nb, n_mid):
    r = FFT_R
    half0 = nb // 2
    re, im = slice(0, r), slice(r, 2 * r)

    def first(k, c):
        wr, wi = tw_edge[0, k], tw_edge[1, k]
        ar, ai = read(k)
        br, bi = read(k + half0)
        w[k, re], w[k, im] = ar + br, ai + bi
        dr, di = _cmul(ar - br, ai - bi, wr, wi)
        w[k + half0, re], w[k + half0, im] = dr, di
        a2r, a2i = _cmul(ar, ai, tw_edge[2, k], tw_edge[3, k])
        b2r, b2i = _cmul(br, bi, tw_edge[4, k], tw_edge[5, k])
        w[nb + k, re], w[nb + k, im] = a2r + b2r, a2i + b2i
        dr, di = _cmul(a2r - b2r, a2i - b2i, wr, wi)
        w[nb + k + half0, re], w[nb + k + half0, im] = dr, di
        return c

    lax.fori_loop(0, half0, first, 0)

    for s in range(n_mid):
        lg = int(round(math.log2(half0))) - 1 - s

        def dif(bb, c, s=s, lg=lg):
            bf, i0, i1 = _slot_pair(bb, lg, nb)
            wr, wi = tw_mid[2 * s, bf], tw_mid[2 * s + 1, bf]
            ar, ai, br, bi = w[i0, re], w[i0, im], w[i1, re], w[i1, im]
            w[i0, re], w[i0, im] = ar + br, ai + bi
            dr, di = _cmul(ar - br, ai - bi, wr, wi)
            w[i1, re], w[i1, im] = dr, di
            return c

        lax.fori_loop(0, nb, dif, 0)


def _hy_conv_kernel(*refs, nb, n_mid, n_post):
    tw_edge, tw_mid, a_ref, bias_ref, kh_ref, g_ref, gi_ref = refs[:7]
    post_refs = refs[7:7 + n_post]
    o_ref, w = refs[7 + n_post], refs[8 + n_post]
    r = FFT_R
    m = 2 * nb
    half0 = nb // 2
    re, im = slice(0, r), slice(r, 2 * r)

    def rows(k):
        return pl.ds(pl.multiple_of(k * r, r), r)

    _block_dft_forward(lambda k: (a_ref[0, rows(k), :].astype(F32), a_ref[1, rows(k), :].astype(F32)),
                       w, tw_edge, tw_mid, nb, n_mid)

    def spectral(t, c):
        slots = [FFT_SLOTS * t + u for u in range(FFT_SLOTS)]
        xs = [jnp.dot(g_ref[s], w[s].astype(BF16), preferred_element_type=F32) for s in slots]
        outs = []
        for s, x in zip(slots, xs):
            yr, yi = _cmul(x[re], x[im], kh_ref[0, s, re, :], kh_ref[0, s, im, :])
            y = jnp.concatenate([yr, yi], axis=0).astype(BF16)
            outs.append(jnp.dot(gi_ref[s], y, preferred_element_type=F32))
        for s, o in zip(slots, outs):
            w[s] = o
        return c

    lax.fori_loop(0, m // FFT_SLOTS, spectral, 0)

    for s in reversed(range(n_mid)):
        lg = int(round(math.log2(half0))) - 1 - s

        def dit(bb, c, s=s, lg=lg):
            bf, i0, i1 = _slot_pair(bb, lg, nb)
            wr, wi = tw_mid[2 * s, bf], -tw_mid[2 * s + 1, bf]
            ar, ai = w[i0, re], w[i0, im]
            br, bi = _cmul(w[i1, re], w[i1, im], wr, wi)
            w[i0, re], w[i0, im] = ar + br, ai + bi
            w[i1, re], w[i1, im] = ar - br, ai - bi
            return c

        lax.fori_loop(0, nb, dit, 0)

    bias = bias_ref[0]

    def emit(k, yr, yi):
        for bsel, y in ((0, yr), (1, yi)):
            a = a_ref[bsel, rows(k), :].astype(F32)
            val = y + bias * a
            for p_ref in post_refs:
                val = val * p_ref[bsel, rows(k), :].astype(F32)
            o_ref[bsel, rows(k), :] = val.astype(o_ref.dtype)

    def last(k, c):
        wr, wi = tw_edge[0, k], -tw_edge[1, k]
        ar, ai = w[k, re], w[k, im]
        br, bi = _cmul(w[k + half0, re], w[k + half0, im], wr, wi)
        cr, ci = w[nb + k, re], w[nb + k, im]
        dr, di = _cmul(w[nb + k + half0, re], w[nb + k + half0, im], wr, wi)
        o0r, o0i = _cmul(cr + dr, ci + di, tw_edge[2, k], -tw_edge[3, k])
        o1r, o1i = _cmul(cr - dr, ci - di, tw_edge[4, k], -tw_edge[5, k])
        emit(k, ar + br + o0r, ai + bi + o0i)
        emit(k + half0, ar - br + o1r, ai - bi + o1i)
        return c

    lax.fori_loop(0, half0, last, 0)


def _hy_conv(a, khat, bias, posts, out_dtype):
    b, n, width = a.shape
    lanes = LANES
    halves = width // lanes
    plan = _fft_plan(n)
    nb, m = plan['nb'], plan['m']
    r = FFT_R
    sig = pl.BlockSpec((2, n, lanes), lambda hf, pr: (pr, 0, hf))
    smem = pl.BlockSpec(memory_space=pltpu.SMEM)
    once = pl.Buffered(1)
    in_specs = [smem, smem, sig,
                pl.BlockSpec((1, 1, lanes), lambda hf, pr: (hf, 0, 0)),
                pl.BlockSpec((1, m, 2 * r, lanes), lambda hf, pr: (hf, 0, 0, 0), pipeline_mode=once),
                pl.BlockSpec((m, 2 * r, 2 * r), lambda hf, pr: (0, 0, 0), pipeline_mode=once),
                pl.BlockSpec((m, 2 * r, 2 * r), lambda hf, pr: (0, 0, 0), pipeline_mode=once)]
    in_specs += [sig] * len(posts)
    g_fwd = jnp.asarray(plan['g'], BF16)
    g_inv = jnp.asarray(np.swapaxes(plan['g'], 1, 2), BF16)
    return pl.pallas_call(
        functools.partial(_hy_conv_kernel, nb=nb, n_mid=plan['n_mid'], n_post=len(posts)),
        grid=(halves, b // 2),
        in_specs=in_specs,
        out_specs=sig,
        out_shape=jax.ShapeDtypeStruct(a.shape, out_dtype),
        scratch_shapes=[pltpu.VMEM((m, 2 * r, lanes), F32)],
        compiler_params=pltpu.CompilerParams(dimension_semantics=("arbitrary", "arbitrary"),
                                             vmem_limit_bytes=HY_VMEM_LIMIT),
        name="hyena_conv",
    )(jnp.asarray(plan['tw_edge']), jnp.asarray(plan['tw_mid']), a, bias, khat, g_fwd, g_inv, *posts)


_HP = lax.Precision.HIGHEST
HY_TAPS_W = HY_ORDER * 2 * HY_W
HY_FEAT = 128


def _hy_taps_kernel(z_ref, dec_ref, w1_ref, b1_ref, f1_ref, w2_ref, b2_ref, f2_ref, w3_ref, o_ref,
                    h_scr, ss_scr):
    phase, i = pl.program_id(0), pl.program_id(1)
    tm = z_ref.shape[0]
    rows = pl.ds(pl.multiple_of(i * tm, tm), tm)

    @pl.when(phase == 0)
    def _():
        @pl.when(i == 0)
        def _():
            ss_scr[...] = jnp.zeros_like(ss_scr)

        h = jnp.dot(z_ref[...], w1_ref[...], precision=_HP, preferred_element_type=F32)
        h = jnp.sin(f1_ref[...] * (h + b1_ref[...]))
        h = jnp.dot(h, w2_ref[...], precision=_HP, preferred_element_type=F32)
        h = jnp.sin(f2_ref[...] * (h + b2_ref[...]))
        dec = dec_ref[...]
        for c0 in range(0, HY_TAPS_W, HY_W):
            t = jnp.dot(h, w3_ref[:, c0:c0 + HY_W], precision=_HP, preferred_element_type=F32) * dec
            h_scr[rows, c0:c0 + HY_W] = t
            ss_scr[:, c0:c0 + HY_W] += jnp.sum(t * t, axis=0, keepdims=True)

    @pl.when(phase == 1)
    def _():
        first_row = lax.broadcasted_iota(jnp.int32, (tm, HY_W), 0) == jnp.where(i == 0, 0, -1)
        for o in range(HY_ORDER):
            c0 = o * 2 * HY_W
            tot = ss_scr[:, c0:c0 + HY_W] + ss_scr[:, c0 + HY_W:c0 + 2 * HY_W]
            inv = lax.rsqrt(tot + EPS)
            o_ref[:, c0:c0 + HY_W] = h_scr[rows, c0:c0 + HY_W] * inv
            o_ref[:, c0 + HY_W:c0 + 2 * HY_W] = jnp.where(first_row, 0.0, h_scr[rows, c0 + HY_W:c0 + 2 * HY_W] * inv)


@functools.lru_cache(maxsize=None)
def _filter_features(n):
    t = np.linspace(0.0, 1.0, n, dtype=np.float32)[:, None]
    omega = (2.0 * np.pi * np.arange(n, dtype=np.float32)[:, None] / n).astype(np.float32)
    bands = np.linspace(1e-4, HY_BANDS - 1, HY_BANDS, dtype=np.float32)[None, :]
    z = np.zeros((n, HY_FEAT), np.float32)
    z[:, 0:1] = t
    z[:, 1:1 + HY_BANDS] = np.cos(bands * omega)
    z[:, 1 + HY_BANDS:HY_EMB] = -np.sin(bands * omega)
    max_decay = math.log(HY_DECAY_TARGET) / HY_FAST_DECAY
    min_decay = math.log(HY_DECAY_TARGET) / HY_SLOW_DECAY
    deltas = np.linspace(min_decay, max_decay, HY_W, dtype=np.float32)
    dec = np.exp(-t * np.abs(deltas)).astype(np.float32)
    return z, dec


def _hy_taps(n, w1, b1, fr1, w2, b2, fr2, w3):
    z, dec = _filter_features(n)
    tm = min(n, 512)
    pad = HY_FEAT - HY_HIDDEN
    w1p = jnp.pad(w1.astype(F32), ((0, HY_FEAT - HY_EMB), (0, pad)))
    w2p = jnp.pad(w2.astype(F32), ((0, pad), (0, pad)))
    w3p = jnp.pad(w3.astype(F32), ((0, pad), (0, 0)))
    vec = lambda v: jnp.pad(v.astype(F32), (0, pad)).reshape(1, HY_FEAT)
    const = lambda ph, i: (0, 0)
    return pl.pallas_call(
        _hy_taps_kernel,
        grid=(2, n // tm),
        in_specs=[pl.BlockSpec((tm, HY_FEAT), lambda ph, i: (i * (1 - ph), 0)),
                  pl.BlockSpec((tm, HY_W), lambda ph, i: (i * (1 - ph), 0)),
                  pl.BlockSpec((HY_FEAT, HY_FEAT), const), pl.BlockSpec((1, HY_FEAT), const),
                  pl.BlockSpec((1, HY_FEAT), const),
                  pl.BlockSpec((HY_FEAT, HY_FEAT), const), pl.BlockSpec((1, HY_FEAT), const),
                  pl.BlockSpec((1, HY_FEAT), const),
                  pl.BlockSpec((HY_FEAT, HY_TAPS_W), const)],
        out_specs=pl.BlockSpec((tm, HY_TAPS_W), lambda ph, i: (i * ph, 0)),
        out_shape=jax.ShapeDtypeStruct((n, HY_TAPS_W), F32),
        scratch_shapes=[pltpu.VMEM((n, HY_TAPS_W), F32), pltpu.VMEM((1, HY_TAPS_W), F32)],
        compiler_params=_cparams("arbitrary", "arbitrary"),
        name="hyena_taps",
    )(jnp.asarray(z), jnp.asarray(dec), w1p, vec(b1), vec(fr1), w2p, vec(b2), vec(fr2), w3p)


def _hy_spec_kernel(tw_edge, tw_mid, f_ref, b_ref, ghi_ref, glo_ref, o_ref, w, *, nb, n_mid, inv_n):
    r = FFT_R
    m = 2 * nb
    re, im = slice(0, r), slice(r, 2 * r)
    zeros = jnp.zeros((r, LANES), F32)

    def rows(k):
        return pl.ds(pl.multiple_of(k * r, r), r)

    def in_block_dft(s):
        x = w[s]
        xh = x.astype(BF16)
        xl = (x - xh.astype(F32)).astype(BF16)
        gh = ghi_ref[s]
        y = jnp.dot(gh, xh, preferred_element_type=F32)
        y += jnp.dot(gh, xl, preferred_element_type=F32)
        y += jnp.dot(glo_ref[s], xh, preferred_element_type=F32)
        return y * inv_n

    _block_dft_forward(lambda k: (f_ref[rows(k), :], zeros), w, tw_edge, tw_mid, nb, n_mid)

    def fwd_part(s, c):
        o_ref[0, 0, s] = in_block_dft(s)
        return c

    lax.fori_loop(0, m, fwd_part, 0)
    _block_dft_forward(lambda k: (b_ref[rows(k), :], zeros), w, tw_edge, tw_mid, nb, n_mid)

    def bwd_part(s, c):
        y = in_block_dft(s)
        o_ref[0, 0, s, re] += y[re]
        o_ref[0, 0, s, im] -= y[im]
        return c

    lax.fori_loop(0, m, bwd_part, 0)


def _hy_filter_spectrum(taps):
    n = taps.shape[0]
    plan = _fft_plan(n)
    nb, m = plan['nb'], plan['m']
    r = FFT_R
    halves = HY_W // LANES
    g = plan['g']
    g_hi = g.astype(jnp.bfloat16)
    g_lo = (g - np.asarray(g_hi, np.float32)).astype(jnp.bfloat16)
    smem = pl.BlockSpec(memory_space=pltpu.SMEM)
    once = pl.Buffered(1)
    table = lambda o, hf: (0, 0, 0)
    return pl.pallas_call(
        functools.partial(_hy_spec_kernel, nb=nb, n_mid=plan['n_mid'], inv_n=1.0 / (2 * n)),
        grid=(HY_ORDER, halves),
        in_specs=[smem, smem,
                  pl.BlockSpec((n, LANES), lambda o, hf: (0, o * 2 * halves + hf)),
                  pl.BlockSpec((n, LANES), lambda o, hf: (0, o * 2 * halves + halves + hf)),
                  pl.BlockSpec((m, 2 * r, 2 * r), table, pipeline_mode=once),
                  pl.BlockSpec((m, 2 * r, 2 * r), table, pipeline_mode=once)],
        out_specs=pl.BlockSpec((1, 1, m, 2 * r, LANES), lambda o, hf: (o, hf, 0, 0, 0)),
        out_shape=jax.ShapeDtypeStruct((HY_ORDER, halves, m, 2 * r, LANES), F32),
        scratch_shapes=[pltpu.VMEM((m, 2 * r, LANES), F32)],
        compiler_params=pltpu.CompilerParams(dimension_semantics=("arbitrary", "arbitrary"),
                                             vmem_limit_bytes=HY_VMEM_LIMIT),
        name="hyena_spectrum",
    )(jnp.asarray(plan['tw_edge']), jnp.asarray(plan['tw_mid']), taps, taps,
      jnp.asarray(g_hi), jnp.asarray(g_lo))


def _pack_w_in(w_in):
    d = w_in.shape[0]
    z = lambda n: jnp.zeros((d, n), w_in.dtype)
    o = 0
    q_lat = w_in[:, o:o + MLA_Q_RANK]; o += MLA_Q_RANK
    kv_lat = w_in[:, o:o + MLA_KV_RANK]; o += MLA_KV_RANK
    k_rope = w_in[:, o:o + MLA_ROPE]; o += MLA_ROPE
    gate = w_in[:, o:o + GROUP_W]; o += GROUP_W
    mla = jnp.concatenate([q_lat, z(256 - MLA_Q_RANK), kv_lat, z(MLA_NOPE), k_rope,
                           z(MLA_DK - MLA_NOPE - MLA_ROPE), gate], axis=1)
    return jnp.concatenate([mla, w_in[:, o:]], axis=1).astype(BF16)


def _pack_mla_up(w_uq, w_ukv):
    dq = MLA_NOPE + MLA_ROPE
    wq = w_uq.reshape(MLA_Q_RANK, MLA_HEADS, dq).transpose(1, 0, 2)
    wq = jnp.pad(wq, ((0, 0), (0, 256 - MLA_Q_RANK), (0, MLA_DK - dq))).astype(BF16)
    wkv = w_ukv.reshape(MLA_KV_RANK, MLA_HEADS, MLA_NOPE + MLA_V).transpose(1, 0, 2)
    wk = jnp.pad(wkv[:, :, :MLA_NOPE], ((0, 0), (0, 0), (0, MLA_DK - MLA_NOPE))).astype(BF16)
    wvt = jnp.swapaxes(wkv[:, :, MLA_NOPE:], 1, 2).astype(BF16)
    return wq, wk, wvt


def kernel(x, c, ctx, c_ctx, w_mod, b_mod, g_pre, g_post, w_in, w_out, mla_g_cq, mla_w_uq, mla_g_ckv, mla_w_ukv, gqa_g_q, gqa_g_k, ssm_lambda_re, ssm_lambda_im, ssm_log_step, ssm_b_re, ssm_b_im, ssm_c_re, ssm_c_im, ssm_d, ssm_glu_w, ssm_glu_b, hy_conv_w, hy_conv_b, hy_f_w1, hy_f_b1, hy_f_freq1, hy_f_w2, hy_f_b2, hy_f_freq2, hy_f_w3, hy_bias):
    b, n_lat, d = x.shape
    n_ctx = ctx.shape[1]
    depth = w_in.shape[0]
    lt = n_ctx + n_lat
    nct = n_ctx // ROW_TILE
    assert n_ctx % ROW_TILE == 0 and n_lat % ROW_TILE == 0 and b % 2 == 0 and b % 8 == 0

    n_cond = -(-(b + 1) // 8) * 8
    cond = jnp.zeros((n_cond, d), F32).at[:b].set(c).at[b].set(c_ctx)
    mod = _modulation(cond, w_mod, b_mod)[:, :b + 1]
    shift, scale, gate = mod[..., :d], mod[..., d:2 * d], mod[..., 2 * d:]

    mla_cos, mla_sin, gqa_cos, gqa_sin = _rope_tables(n_ctx, n_lat)
    nq = GQA_HEADS * GQA_DIM
    ones_bd = jnp.asarray(np.kron(np.eye(GQA_HEADS), np.ones((GQA_DIM, GQA_DIM))), BF16)

    xa = jnp.concatenate([ctx, x], axis=1)
    for l in range(depth):
        sc = (g_pre[l][None, :] * (1.0 + scale[l]))[:, None, :]
        sh = shift[l][:, None, :]
        wq, wk, wvt = _pack_mla_up(mla_w_uq[l], mla_w_ukv[l])
        g_cq = jnp.pad(mla_g_cq[l], (0, 256 - MLA_Q_RANK)).reshape(1, 256)
        mla_args = (mla_cos, mla_sin, g_cq, mla_g_ckv[l].reshape(1, -1), wq, wk, wvt)
        gqa_args = (gqa_cos, gqa_sin, jnp.tile(gqa_g_q[l], GQA_HEADS).reshape(1, nq),
                    jnp.tile(gqa_g_k[l], GQA_KV_HEADS).reshape(1, -1), ones_bd)
        gate_m, gate_g, p_ssm, p_hy, mla_qkv, gqa_qkv = _inproj(xa, sc, sh, _pack_w_in(w_in[l]), nct,
                                                                mla_args, gqa_args)
        a_out = _attention(*mla_qkv, gate_m, 0, n_ctx)
        g_out = _attention(*gqa_qkv, gate_g, 0, n_ctx)

        u_tm = jnp.transpose(p_ssm, (1, 0, 2)).reshape(lt * b, SSM_PACK)
        mats = [_ssm_matrices(ssm_lambda_re[l, di], ssm_lambda_im[l, di], ssm_log_step[l, di],
                              ssm_b_re[l, di], ssm_b_im[l, di], ssm_c_re[l, di], ssm_c_im[l, di])
                for di in range(2)]
        n_ctx_chunks = n_ctx // SSM_T
        y_f = _ssm_direction(u_tm, mats[0], b, n_ctx_chunks, False)
        fin = (y_f, ssm_d[l].reshape(1, -1), ssm_glu_w[l].astype(BF16), ssm_glu_b[l].reshape(1, -1))
        s_tm = _ssm_direction(u_tm, mats[1], b, n_ctx_chunks, True, fin)
        s_out = jnp.transpose(s_tm.reshape(lt, b, GROUP_W), (1, 0, 2))

        filt = (hy_f_w1[l], hy_f_b1[l], hy_f_freq1[l], hy_f_w2[l], hy_f_b2[l], hy_f_freq2[l], hy_f_w3[l])
        conv_b = hy_conv_b[l].reshape(1, -1)
        parts = []
        for tile0, n_tiles in ((0, nct), (nct, n_lat // ROW_TILE)):
            n = n_tiles * ROW_TILE
            hv, hx1, hx2, hsg = _hy_pre(p_hy, hy_conv_w[l], conv_b, tile0, n_tiles)
            khat = _hy_filter_spectrum(_hy_taps(n, *filt))
            bias = hy_bias[l].astype(F32).reshape(HY_ORDER, HY_W // LANES, 1, LANES)
            z1 = _hy_conv(hv, khat[0], bias[0], (hx1,), BF16)
            parts.append(_hy_conv(z1, khat[1], bias[1], (hx2, hsg), BF16))
        y_out = jnp.concatenate(parts, axis=1)

        xa = _outproj(a_out, g_out, s_out, y_out, xa, gate[l][:, None, :], g_post[l].reshape(1, d),
                      w_out[l].astype(BF16), nct, tile0=nct if l == depth - 1 else 0)
    return xa
```

```python
import functools
import math

import numpy as np
import jax
import jax.numpy as jnp
from jax import lax
from jax.experimental import pallas as pl
from jax.experimental.pallas import tpu as pltpu

F32 = jnp.float32
BF16 = jnp.bfloat16

GRID_W = 64
ROPE_BASE = 10000.0
EPS = 1e-6
GROUP_W = 256
MLA_HEADS, MLA_NOPE, MLA_ROPE, MLA_V = 4, 64, 32, 64
MLA_Q_RANK, MLA_KV_RANK = 192, 128
GQA_HEADS, GQA_KV_HEADS, GQA_DIM = 4, 2, 64
SSM_GROUPS, SSM_GROUP, SSM_STATE = 16, 16, 64
HY_W, HY_ORDER, HY_EMB, HY_HIDDEN = 256, 2, 33, 64
HY_BANDS = (HY_EMB - 1) // 2
HY_FAST_DECAY, HY_SLOW_DECAY, HY_DECAY_TARGET = 0.3, 1.5, 1e-2

MLA_PACK = 768
GQA_PACK = 768
SSM_PACK = 512
HY_PACK = 1024
IN_PACK = MLA_PACK + GQA_PACK + SSM_PACK + HY_PACK
MLA_DK = 128

LANES = 128
ROW_TILE = 256
ATT_TQ = 256
ATT_TK = 256
ATT_UNROLLS = (16, 8, 4, 2)
ONES_ROWS = 16
SSM_T = 128
SSM_LANE_SPLIT = 512
FFT_R = 128
FFT_SLOTS = 4
VMEM_LIMIT = 52 * 1024 * 1024
HY_VMEM_LIMIT = 58 * 1024 * 1024


def _cparams(*sem):
    return pltpu.CompilerParams(dimension_semantics=sem, vmem_limit_bytes=VMEM_LIMIT)


def _silu(x):
    return x * jax.nn.sigmoid(x)


def _mod_kernel(c_ref, w_ref, b_ref, o_ref):
    c = c_ref[...]
    s = _silu(c).astype(BF16)
    o_ref[0] = jnp.dot(s, w_ref[0].astype(BF16), preferred_element_type=F32) + b_ref[0]


def _modulation(cond, w_mod, b_mod):
    depth, d, n3 = w_mod.shape
    r = cond.shape[0]
    tn = 512
    return pl.pallas_call(
        _mod_kernel,
        grid=(depth, n3 // tn),
        in_specs=[pl.BlockSpec((r, d), lambda l, j: (0, 0)),
                  pl.BlockSpec((1, d, tn), lambda l, j: (l, 0, j)),
                  pl.BlockSpec((1, 1, tn), lambda l, j: (l, 0, j))],
        out_specs=pl.BlockSpec((1, r, tn), lambda l, j: (l, 0, j)),
        out_shape=jax.ShapeDtypeStruct((depth, r, n3), F32),
        compiler_params=_cparams("parallel", "parallel"),
        name="modulation",
    )(cond, w_mod, b_mod.reshape(depth, 1, n3))


def _outproj_kernel(a_ref, g_ref, s_ref, y_ref, x_ref, gt_ref, gp_ref, w_ref, o_ref):
    half = a_ref.shape[1] // 2
    for hs in (slice(0, half), slice(half, 2 * half)):
        acc = jnp.dot(a_ref[0, hs], w_ref[0:GROUP_W], preferred_element_type=F32)
        acc += jnp.dot(g_ref[0, hs], w_ref[GROUP_W:2 * GROUP_W], preferred_element_type=F32)
        acc += jnp.dot(s_ref[0, hs], w_ref[2 * GROUP_W:3 * GROUP_W], preferred_element_type=F32)
        acc += jnp.dot(y_ref[0, hs], w_ref[3 * GROUP_W:4 * GROUP_W], preferred_element_type=F32)
        ms = jnp.mean(acc * acc, axis=-1, keepdims=True)
        o_ref[0, hs] = x_ref[0, hs] + gt_ref[0] * (acc * lax.rsqrt(ms + EPS) * gp_ref[...])


def _outproj(a, g, s, y, xa, gate, g_post, w, nct, tile0=0):
    b, lt, d = xa.shape
    tm = ROW_TILE

    def mod_idx(bi, i):
        return (jnp.where(i + tile0 < nct, b, bi), 0, 0)

    row = lambda bi, i: (bi, i + tile0, 0)
    return pl.pallas_call(
        _outproj_kernel,
        grid=(b, lt // tm - tile0),
        in_specs=[pl.BlockSpec((1, tm, GROUP_W), row)] * 4 + [
            pl.BlockSpec((1, tm, d), row),
            pl.BlockSpec((1, 1, d), mod_idx),
            pl.BlockSpec((1, d), lambda bi, i: (0, 0)),
            pl.BlockSpec((4 * GROUP_W, d), lambda bi, i: (0, 0))],
        out_specs=pl.BlockSpec((1, tm, d), lambda bi, i: (bi, i, 0)),
        out_shape=jax.ShapeDtypeStruct((b, lt - tile0 * tm, d), F32),
        compiler_params=_cparams("parallel", "parallel"),
        name="outproj",
    )(a, g, s, y, xa, gate, g_post, w)


def _rope(x, cos, sin, shift):
    w = x.shape[-1]
    lane = lax.broadcasted_iota(jnp.int32, x.shape, 1)
    first = (lane & shift) == 0
    swapped = jnp.where(first, -pltpu.roll(x, w - shift, 1), pltpu.roll(x, shift, 1))
    return x * cos + swapped * sin


def _rope_tables(n_ctx, n_lat):
    t = np.arange(n_lat)
    row = (t // GRID_W).astype(np.float64)
    col = (t % GRID_W).astype(np.float64)

    def block(pos, h):
        inv = ROPE_BASE ** (-np.arange(0, h, 2, dtype=np.float64) / h)
        ang = (pos[:, None].astype(np.float32) * inv[None, :].astype(np.float32)).astype(np.float32)
        c, s = np.cos(ang), np.sin(ang)
        return np.concatenate([c, c], -1), np.concatenate([s, s], -1)

    def full(h, lead, width):
        cr, sr = block(row, h)
        cc, sc = block(col, h)
        cos = np.ones((n_ctx + n_lat, width), np.float32)
        sin = np.zeros((n_ctx + n_lat, width), np.float32)
        cos[n_ctx:, lead:lead + 2 * h] = np.concatenate([cr, cc], -1)
        sin[n_ctx:, lead:lead + 2 * h] = np.concatenate([sr, sc], -1)
        return cos, sin

    mc, ms = full(MLA_ROPE // 2, MLA_NOPE, MLA_DK)
    gc, gs = full(GQA_DIM // 2, 0, GQA_DIM)
    reps = GQA_HEADS
    return (jnp.asarray(mc), jnp.asarray(ms),
            jnp.asarray(np.tile(gc, (1, reps))), jnp.asarray(np.tile(gs, (1, reps))))


_NT = (((1,), (1,)), ((), ()))
_TN = (((0,), (0,)), ((), ()))
LOG2E = math.log2(math.e)


def _mla_prep_body(p, cos_ref, sin_ref, gq_ref, gkv_ref, wq_ref, wk_ref, wvt_ref, eye_ref,
                   qt_ref, k_ref, vt_ref):
    cos, sin = cos_ref[...], sin_ref[...]
    ql = p[:, 0:256]
    rq = lax.rsqrt(jnp.sum(ql * ql, axis=-1, keepdims=True) * (1.0 / MLA_Q_RANK) + EPS)
    qn = (ql * rq * gq_ref[...]).astype(BF16)
    kvl = p[:, 256:384]
    rk = lax.rsqrt(jnp.mean(kvl * kvl, axis=-1, keepdims=True) + EPS)
    kvn = (kvl * rk * gkv_ref[...]).astype(BF16)
    k_rope = _rope(p[:, 384:512], cos, sin, MLA_ROPE // 4)
    scale = (MLA_NOPE + MLA_ROPE) ** -0.5 * LOG2E
    heads = range(MLA_HEADS)
    qs = [jnp.dot(qn, wq_ref[h], preferred_element_type=F32) for h in heads]
    ks = [jnp.dot(kvn, wk_ref[h], preferred_element_type=F32) for h in heads]
    vts = [lax.dot_general(wvt_ref[h], kvn, _NT, preferred_element_type=F32) for h in heads]
    for h in heads:
        k_ref[0, h] = (ks[h] + k_rope).astype(BF16)
        vt_ref[0, h, 0, 0:MLA_V] = vts[h].astype(BF16)
        vt_ref[0, h, 0, MLA_V:MLA_V + ONES_ROWS] = jnp.ones((ONES_ROWS, kvn.shape[0]), BF16)
    qr = [(_rope(q, cos, sin, MLA_ROPE // 4) * scale).astype(BF16) for q in qs]
    for h in heads:
        qt_ref[0, h] = lax.dot_general(eye_ref[...], qr[h], _NT, preferred_element_type=F32).astype(BF16)


def _head_mean_sq(x, ones_bd):
    sq = x * x
    hi = sq.astype(BF16)
    lo = (sq - hi.astype(F32)).astype(BF16)
    s = jnp.dot(hi, ones_bd, preferred_element_type=F32) + jnp.dot(lo, ones_bd, preferred_element_type=F32)
    return s * (1.0 / GQA_DIM)


def _gqa_prep_body(p, cos_ref, sin_ref, gq_ref, gk_ref, ones_ref, eye_ref, qt_ref, k_ref, vt_ref):
    cos, sin = cos_ref[...], sin_ref[...]
    nq = GQA_HEADS * GQA_DIM
    nk = GQA_KV_HEADS * GQA_DIM
    q = p[:, 0:nq]
    k = p[:, nq:nq + nk]
    v = p[:, nq + nk:nq + 2 * nk].astype(BF16)
    q_ms = _head_mean_sq(q, ones_ref[...])
    k_ms = _head_mean_sq(k, ones_ref[0:nk, 0:nk])
    for h in range(GQA_KV_HEADS):
        sel = eye_ref[h * GQA_DIM:(h + 1) * GQA_DIM, 0:nk]
        vt_ref[0, h, 0, 0:GQA_DIM] = lax.dot_general(sel, v, _NT, preferred_element_type=F32).astype(BF16)
        vt_ref[0, h, 0, GQA_DIM:GQA_DIM + ONES_ROWS] = jnp.ones((ONES_ROWS, v.shape[0]), BF16)
    qn = q * lax.rsqrt(q_ms + EPS) * gq_ref[...]
    qr = (_rope(qn, cos, sin, GQA_DIM // 4) * (GQA_DIM ** -0.5 * LOG2E)).astype(BF16)
    kn = k * lax.rsqrt(k_ms + EPS) * gk_ref[...]
    kr = _rope(kn, cos[:, 0:nk], sin[:, 0:nk], GQA_DIM // 4)
    for h in range(GQA_HEADS):
        sel = eye_ref[h * GQA_DIM:(h + 1) * GQA_DIM, :]
        qt_ref[0, h] = lax.dot_general(sel, qr, _NT, preferred_element_type=F32).astype(BF16)
    for h in range(GQA_KV_HEADS):
        k_ref[0, h] = kr[:, h * GQA_DIM:(h + 1) * GQA_DIM].astype(BF16)


def _inproj_kernel(x_ref, xp_ref, xn_ref, sc_ref, sh_ref, w_ref,
                   mcos_ref, msin_ref, gcq_ref, gckv_ref, wq_ref, wk_ref, wvt_ref, eye_m_ref,
                   gcos_ref, gsin_ref, gq_ref, gk_ref, ones_ref, eye_g_ref, cw_ref, cb_ref,
                   gate_m_ref, gate_g_ref, ssm_ref, hv_ref, hx1_ref, hx2_ref, hsg_ref,
                   mqt_ref, mk_ref, mvt_ref, gqt_ref, gk_out_ref, gvt_ref, *, nct, n_tiles):
    i = pl.program_id(1)

    tm = x_ref.shape[1]

    def normed(x):
        ms = jnp.mean(x * x, axis=-1, keepdims=True)
        return (x * lax.rsqrt(ms + EPS) * sc_ref[0] + sh_ref[0]).astype(BF16)

    halo = normed(jnp.concatenate([xp_ref[0], xn_ref[0]], axis=0))
    hb = normed(x_ref[0])

    def proj(c0, width, lhs=hb):
        return jnp.dot(lhs, w_ref[:, c0:c0 + width], preferred_element_type=F32)

    qkv = 2 * GROUP_W
    hy0 = MLA_PACK + GQA_PACK + SSM_PACK
    p_mla = proj(0, qkv)
    p_gqa = proj(MLA_PACK, qkv)
    p_hy_ext = proj(hy0, HY_PACK, jnp.concatenate([hb, halo], axis=0))
    p_hy = p_hy_ext[0:tm]
    halo_prev = p_hy_ext[tm + 7:tm + 8, 0:3 * HY_W]
    halo_next = p_hy_ext[tm + 8:tm + 9, 0:3 * HY_W]
    gate_m_ref[0] = proj(qkv, GROUP_W).astype(BF16)
    gate_g_ref[0] = proj(MLA_PACK + qkv, GROUP_W).astype(BF16)
    ssm_ref[0] = proj(MLA_PACK + GQA_PACK, SSM_PACK).astype(BF16)
    seq_start = (i == 0) | (i == nct)
    seq_end = (i == nct - 1) | (i == n_tiles - 1)
    _hy_pre_body(p_hy, jnp.where(seq_start, 0.0, halo_prev), jnp.where(seq_end, 0.0, halo_next),
                 cw_ref, cb_ref, hv_ref, hx1_ref, hx2_ref, hsg_ref)
    _mla_prep_body(p_mla, mcos_ref, msin_ref, gcq_ref, gckv_ref, wq_ref, wk_ref, wvt_ref, eye_m_ref,
                   mqt_ref, mk_ref, mvt_ref)
    _gqa_prep_body(p_gqa, gcos_ref, gsin_ref, gq_ref, gk_ref, ones_ref, eye_g_ref,
                   gqt_ref, gk_out_ref, gvt_ref)


def _inproj(xa, scale, shift, w, nct, mla_args, gqa_args, hy_args):
    b, lt, d = xa.shape
    tm = ROW_TILE
    nq = GQA_HEADS * GQA_DIM
    nk = GQA_KV_HEADS * GQA_DIM
    n_tiles = lt // tm
    sub = tm // 8

    def mod_idx(bi, i):
        return (jnp.where(i < nct, b, bi), 0, 0)

    const2 = lambda bi, i: (0, 0)
    const3 = lambda bi, i: (0, 0, 0)
    row = lambda bi, i: (bi, i, 0)
    pos = lambda bi, i: (i, 0)
    qt_idx = lambda bi, i: (bi, 0, 0, i)
    k_idx = lambda bi, i: (bi, 0, i, 0)
    vt_idx = lambda bi, i: (bi, 0, i, 0, 0)
    in_specs = [pl.BlockSpec((1, tm, d), row),
                pl.BlockSpec((1, 8, d), lambda bi, i: (bi, jnp.maximum(i * sub - 1, 0), 0)),
                pl.BlockSpec((1, 8, d), lambda bi, i: (bi, jnp.minimum((i + 1) * sub, lt // 8 - 1), 0)),
                pl.BlockSpec((1, 1, d), mod_idx), pl.BlockSpec((1, 1, d), mod_idx),
                pl.BlockSpec((d, IN_PACK), const2),
                pl.BlockSpec((tm, MLA_DK), pos), pl.BlockSpec((tm, MLA_DK), pos),
                pl.BlockSpec((1, 256), const2), pl.BlockSpec((1, MLA_KV_RANK), const2),
                pl.BlockSpec((MLA_HEADS, 256, MLA_DK), const3),
                pl.BlockSpec((MLA_HEADS, MLA_KV_RANK, MLA_DK), const3),
                pl.BlockSpec((MLA_HEADS, MLA_V, MLA_KV_RANK), const3),
                pl.BlockSpec((MLA_DK, MLA_DK), const2),
                pl.BlockSpec((tm, nq), pos), pl.BlockSpec((tm, nq), pos),
                pl.BlockSpec((1, nq), const2), pl.BlockSpec((1, nk), const2),
                pl.BlockSpec((nq, nq), const2), pl.BlockSpec((nq, nq), const2),
                pl.BlockSpec((3, 3 * HY_W), const2), pl.BlockSpec((1, 3 * HY_W), const2)]
    out_specs = [pl.BlockSpec((1, tm, GROUP_W), row), pl.BlockSpec((1, tm, GROUP_W), row),
                 pl.BlockSpec((1, tm, SSM_PACK), row)] + [pl.BlockSpec((1, tm, HY_W), row)] * 4 + [
                 pl.BlockSpec((1, MLA_HEADS, MLA_DK, tm), qt_idx),
                 pl.BlockSpec((1, MLA_HEADS, tm, MLA_DK), k_idx),
                 pl.BlockSpec((1, MLA_HEADS, 1, MLA_V + ONES_ROWS, tm), vt_idx),
                 pl.BlockSpec((1, GQA_HEADS, GQA_DIM, tm), qt_idx),
                 pl.BlockSpec((1, GQA_KV_HEADS, tm, GQA_DIM), k_idx),
                 pl.BlockSpec((1, GQA_KV_HEADS, 1, GQA_DIM + ONES_ROWS, tm), vt_idx)]
    out_shape = [jax.ShapeDtypeStruct((b, lt, GROUP_W), BF16), jax.ShapeDtypeStruct((b, lt, GROUP_W), BF16),
                 jax.ShapeDtypeStruct((b, lt, SSM_PACK), BF16)] + [
                 jax.ShapeDtypeStruct((b, lt, HY_W), BF16)] * 4 + [
                 jax.ShapeDtypeStruct((b, MLA_HEADS, MLA_DK, lt), BF16),
                 jax.ShapeDtypeStruct((b, MLA_HEADS, lt, MLA_DK), BF16),
                 jax.ShapeDtypeStruct((b, MLA_HEADS, n_tiles, MLA_V + ONES_ROWS, tm), BF16),
                 jax.ShapeDtypeStruct((b, GQA_HEADS, GQA_DIM, lt), BF16),
                 jax.ShapeDtypeStruct((b, GQA_KV_HEADS, lt, GQA_DIM), BF16),
                 jax.ShapeDtypeStruct((b, GQA_KV_HEADS, n_tiles, GQA_DIM + ONES_ROWS, tm), BF16)]
    outs = pl.pallas_call(
        functools.partial(_inproj_kernel, nct=nct, n_tiles=n_tiles),
        grid=(b, n_tiles),
        in_specs=in_specs,
        out_specs=out_specs,
        out_shape=out_shape,
        compiler_params=_cparams("parallel", "parallel"),
        name="inproj",
    )(xa, xa, xa, scale, shift, w, *mla_args, jnp.eye(MLA_DK, dtype=BF16), *gqa_args,
      jnp.eye(nq, dtype=BF16), *hy_args)
    return outs[0], outs[1], outs[2], tuple(outs[3:7]), tuple(outs[7:10]), tuple(outs[10:13])


def _attn_kernel(qt_ref, qn_ref, k_ref, vt_ref, gate_ref, eye_ref, o_ref, sa_ref, sb_ref, pa_ref, pb_ref,
                 cm_ref, acc_ref, *, heads, group, nct_q, ctx_chunks, all_chunks, unroll):
    i = pl.program_id(1)
    n_loops = jnp.where(i < nct_q, (ctx_chunks - 1) // unroll, (all_chunks - 1) // unroll)
    tq = qt_ref.shape[3]
    dva = vt_ref.shape[3]
    dv = dva - ONES_ROWS

    def score(j, h, dst, q_ref=qt_ref):
        rows = pl.ds(pl.multiple_of(j * ATT_TK, ATT_TK), ATT_TK)
        s = jnp.dot(k_ref[0, h // group, rows, :], q_ref[0, h], preferred_element_type=F32)
        dst[h] = s
        return jnp.max(s, axis=0, keepdims=True)

    def value(j, h, p_ref):
        return jnp.dot(vt_ref[0, h // group, j], p_ref[h], preferred_element_type=F32)

    def step(j, carry, s_cur, s_nxt, p_prev, p_cur):
        state, cmax = carry
        new, nmax = [], []
        for h in range(heads):
            pv = value(jnp.maximum(j - 1, 0), h, p_prev)
            if s_nxt is not None:
                nmax.append(score(j + 1, h, s_nxt))
            m = state[h]
            m_new = jnp.maximum(m, cmax[h])
            alpha = jnp.exp2(m - m_new)
            p_cur[h] = jnp.exp2(s_cur[h] - m_new).astype(BF16)
            acc_ref[h] = alpha * (acc_ref[h] + pv)
            new.append(m_new)
        return tuple(new), tuple(nmax)

    bufs = ((sa_ref, sb_ref, pb_ref, pa_ref), (sb_ref, sa_ref, pa_ref, pb_ref))

    def body(t, carry):
        for u in range(unroll):
            carry = step(unroll * t + u, carry, *bufs[u % 2])
        return carry

    @pl.when(i == 0)
    def _():
        for h in range(heads):
            cm_ref[h] = score(0, h, sa_ref)

    cmax0 = tuple(cm_ref[h] for h in range(heads))
    pb_ref[...] = jnp.zeros_like(pb_ref)
    acc_ref[...] = jnp.zeros_like(acc_ref)
    init = tuple(jnp.full((1, tq), -jnp.inf, F32) for _ in range(heads))
    carry = lax.fori_loop(0, n_loops, body, (init, cmax0))
    last = unroll * n_loops
    step(last, carry, sa_ref, None, pb_ref, pa_ref)
    pvs = [value(last, h, pa_ref) for h in range(heads)]
    for h in range(heads):
        cm_ref[h] = score(0, h, sa_ref, qn_ref)
    outs = []
    for h in range(heads):
        acc = acc_ref[h] + pvs[h]
        outs.append((acc[0:dv] * (1.0 / acc[dv:dv + 1])).astype(BF16))
    out = lax.dot_general(jnp.concatenate(outs, axis=0), eye_ref[...], _TN, preferred_element_type=F32)
    g = gate_ref[0].astype(F32)
    o_ref[0] = (out * _silu(g)).astype(BF16)


def _attention(qt, k, vt, gate_src, gate_block, n_ctx):
    b, heads, dk, lt = qt.shape
    hk, n_chunks, dva = k.shape[1], vt.shape[2], vt.shape[3]
    assert heads * (dva - ONES_ROWS) == GROUP_W
    n_tiles = lt // ATT_TQ
    ctx_chunks, all_chunks = n_ctx // ATT_TK, lt // ATT_TK
    unroll = next(u for u in ATT_UNROLLS if (ctx_chunks - 1) % u == 0 and (all_chunks - 1) % u == 0)
    kern = functools.partial(_attn_kernel, heads=heads, group=heads // hk, nct_q=n_ctx // ATT_TQ,
                             ctx_chunks=ctx_chunks, all_chunks=all_chunks, unroll=unroll)
    return pl.pallas_call(
        kern,
        grid=(b, n_tiles),
        in_specs=[pl.BlockSpec((1, heads, dk, ATT_TQ), lambda bi, i: (bi, 0, 0, i)),
                  pl.BlockSpec((1, heads, dk, ATT_TQ), lambda bi, i: (bi, 0, 0, jnp.minimum(i + 1, n_tiles - 1))),
                  pl.BlockSpec((1, hk, lt, dk), lambda bi, i: (bi, 0, 0, 0)),
                  pl.BlockSpec((1, hk, n_chunks, dva, ATT_TK), lambda bi, i: (bi, 0, 0, 0, 0)),
                  pl.BlockSpec((1, ATT_TQ, GROUP_W), lambda bi, i: (bi, i, gate_block)),
                  pl.BlockSpec((GROUP_W, GROUP_W), lambda bi, i: (0, 0))],
        out_specs=pl.BlockSpec((1, ATT_TQ, GROUP_W), lambda bi, i: (bi, i, 0)),
        out_shape=jax.ShapeDtypeStruct((b, lt, GROUP_W), BF16),
        scratch_shapes=[pltpu.VMEM((heads, ATT_TK, ATT_TQ), F32), pltpu.VMEM((heads, ATT_TK, ATT_TQ), F32),
                        pltpu.VMEM((heads, ATT_TK, ATT_TQ), BF16), pltpu.VMEM((heads, ATT_TK, ATT_TQ), BF16),
                        pltpu.VMEM((heads, 1, ATT_TQ), F32), pltpu.VMEM((heads, dva, ATT_TQ), F32)],
        compiler_params=_cparams("parallel", "arbitrary"),
        name="attention",
    )(qt, qt, k, vt, gate_src, jnp.eye(GROUP_W, dtype=BF16))


def _ssm_kernel(*refs, steps, nb, reverse, finish):
    if finish:
        u_ref, b_ref, a_ref, c_ref, yf_ref, d_ref, gw_ref, gb_ref, o_ref, st, car = refs
    else:
        u_ref, b_ref, a_ref, c_ref, o_ref, st, car = refs

    @pl.when(pl.program_id(0) == 0)
    def _():
        car[...] = jnp.zeros_like(car)

    half = st.shape[0] // 2
    halves = (slice(0, half), slice(half, 2 * half))
    for hs in halves:
        st[hs] = jnp.dot(u_ref[hs, 0:GROUP_W], b_ref[...], preferred_element_type=F32)
    n_state = st.shape[1] // 2
    lw = min(SSM_LANE_SPLIT, n_state)
    for c0 in range(0, n_state, lw):
        lr = slice(c0, c0 + lw)
        li = slice(n_state + c0, n_state + c0 + lw)
        ar = jnp.broadcast_to(a_ref[:, lr], (nb, lw))
        ai = jnp.broadcast_to(a_ref[:, li], (nb, lw))

        def body(tt, carry, lr=lr, li=li, ar=ar, ai=ai):
            sr, si = carry
            t = (steps - 1 - tt) if reverse else tt
            rows = pl.ds(pl.multiple_of(t * nb, nb), nb)
            nr = ar * sr - ai * si + st[rows, lr]
            ni = ar * si + ai * sr + st[rows, li]
            st[rows, lr] = nr
            st[rows, li] = ni
            return nr, ni

        sr, si = lax.fori_loop(0, steps, body, (car[:, lr], car[:, li]), unroll=2)
        car[:, lr] = sr
        car[:, li] = si
    ys = [jnp.dot(st[hs].astype(BF16), c_ref[...], preferred_element_type=F32) for hs in halves]
    if not finish:
        for hs, y in zip(halves, ys):
            o_ref[hs] = y
    else:
        zs = [jax.nn.gelu(y + yf_ref[hs] + d_ref[...] * u_ref[hs, 0:GROUP_W].astype(F32), approximate=True)
              for hs, y in zip(halves, ys)]
        gls = [jnp.dot(z.astype(BF16), gw_ref[...], preferred_element_type=F32) + gb_ref[...] for z in zs]
        for hs, z, gl in zip(halves, zs, gls):
            gate = u_ref[hs, GROUP_W:2 * GROUP_W].astype(F32)
            o_ref[hs] = (z * jax.nn.sigmoid(gl) * _silu(gate)).astype(BF16)


def _ssm_direction(u_tm, mats, nb, n_ctx_chunks, reverse, finish_args=None):
    rows, _ = u_tm.shape
    steps = SSM_T
    blk = steps * nb
    n_chunks = rows // blk
    n_state2 = mats[0].shape[1]
    if reverse:
        cidx = lambda i: (jnp.where(i < n_ctx_chunks, n_ctx_chunks - 1 - i, n_chunks - 1 - (i - n_ctx_chunks)), 0)
    else:
        cidx = lambda i: (i, 0)
    const = lambda i: (0, 0)
    in_specs = [pl.BlockSpec((blk, SSM_PACK), cidx)] + [pl.BlockSpec(m.shape, const) for m in mats]
    args = [u_tm, *mats]
    finish = finish_args is not None
    if finish:
        yf, d_skip, glu_w, glu_b = finish_args
        in_specs += [pl.BlockSpec((blk, GROUP_W), cidx), pl.BlockSpec(d_skip.shape, const),
                     pl.BlockSpec(glu_w.shape, const), pl.BlockSpec(glu_b.shape, const)]
        args += [yf, d_skip, glu_w, glu_b]
    return pl.pallas_call(
        functools.partial(_ssm_kernel, steps=steps, nb=nb, reverse=reverse, finish=finish),
        grid=(n_chunks,),
        in_specs=in_specs,
        out_specs=pl.BlockSpec((blk, GROUP_W), cidx),
        out_shape=jax.ShapeDtypeStruct((rows, GROUP_W), BF16 if finish else F32),
        scratch_shapes=[pltpu.VMEM((blk, n_state2), F32), pltpu.VMEM((nb, n_state2), F32)],
        compiler_params=_cparams("arbitrary"),
        name="ssm_rev" if reverse else "ssm_fwd",
    )(*args)


def _ssm_matrices(lam_re, lam_im, log_step, b_re, b_im, c_re, c_im):
    lr, li = lam_re.astype(F32), lam_im.astype(F32)
    step = jnp.exp(log_step.astype(F32))[:, None]
    mag = jnp.exp(lr * step)
    a_re, a_im = mag * jnp.cos(li * step), mag * jnp.sin(li * step)
    den = lr * lr + li * li
    q_re = ((a_re - 1.0) * lr + a_im * li) / den
    q_im = (a_im * lr - (a_re - 1.0) * li) / den
    bb_re, bb_im = b_re.astype(F32), b_im.astype(F32)
    bbar_re = q_re[..., None] * bb_re - q_im[..., None] * bb_im
    bbar_im = q_re[..., None] * bb_im + q_im[..., None] * bb_re
    eye = jnp.eye(SSM_GROUPS, dtype=F32)
    n_in = SSM_GROUPS * SSM_GROUP
    n_state = SSM_GROUPS * SSM_STATE

    def drive(m):
        return jnp.einsum('gph,gk->ghkp', m, eye).reshape(n_in, n_state).astype(BF16)

    def readout(m):
        return jnp.einsum('ghp,gk->gpkh', m, eye).reshape(n_state, n_in).astype(BF16)

    return (jnp.concatenate([drive(bbar_re), drive(bbar_im)], axis=1),
            jnp.concatenate([a_re.reshape(1, n_state), a_im.reshape(1, n_state)], axis=1),
            jnp.concatenate([readout(c_re.astype(F32)), readout(-c_im.astype(F32))], axis=0))


def _hy_pre_body(p, prev_row, next_row, w_ref, b_ref, v_ref, x1_ref, x2_ref, sg_ref):
    nconv = 3 * HY_W
    x = p[:, 0:nconv]
    tm = x.shape[0]
    rid = lax.broadcasted_iota(jnp.int32, x.shape, 0)
    xm = jnp.where(rid == 0, prev_row, pltpu.roll(x, 1, 0))
    xp = jnp.where(rid == tm - 1, next_row, pltpu.roll(x, tm - 1, 0))
    proj = xm * w_ref[0:1] + x * w_ref[1:2] + xp * w_ref[2:3] + b_ref[...]
    for o_ref, val in ((v_ref, proj[:, 0:HY_W]), (x1_ref, proj[:, HY_W:2 * HY_W]),
                       (x2_ref, proj[:, 2 * HY_W:3 * HY_W]), (sg_ref, _silu(p[:, nconv:nconv + HY_W]))):
        o_ref[0] = val.astype(BF16)


def _bitrev(p, bits):
    r = 0
    for i in range(bits):
        r = (r << 1) | ((p >> i) & 1)
    return r


@functools.lru_cache(maxsize=None)
def _fft_plan(n):
    r = FFT_R
    nb = n // r
    m = 2 * nb
    big = 2 * n
    bits = int(round(math.log2(nb)))
    assert nb >= 2 and (1 << bits) == nb
    jmap = np.zeros(m, np.int64)
    for p in range(nb):
        jmap[p] = 2 * _bitrev(p, bits)
        jmap[nb + p] = 2 * _bitrev(p, bits) + 1
    k2 = np.arange(r)
    n2 = np.arange(r)
    g = np.zeros((m, 2 * r, 2 * r), np.float32)
    for s in range(m):
        ang = -2.0 * np.pi * (np.outer(k2, n2) / r + np.outer(np.ones(r), n2) * jmap[s] / big)
        gr, gi = np.cos(ang), np.sin(ang)
        g[s] = np.block([[gr, -gi], [gi, gr]])
    half0 = nb // 2
    k = np.arange(half0)
    wnb = np.exp(-2j * np.pi * k / nb)
    wm0 = np.exp(-2j * np.pi * k / m)
    wm1 = np.exp(-2j * np.pi * (k + half0) / m)
    tw_edge = np.stack([wnb.real, wnb.imag, wm0.real, wm0.imag, wm1.real, wm1.imag]).astype(np.float32)
    mids = []
    h = half0 // 2
    while h >= 1:
        kk = np.arange(nb // 2) % h
        w = np.exp(-2j * np.pi * kk / (2 * h))
        mids.append(np.stack([w.real, w.imag]))
        h //= 2
    tw_mid = (np.concatenate(mids, 0) if mids else np.zeros((2, max(nb // 2, 1)))).astype(np.float32)
    freq = (jmap[:, None] + m * k2[None, :])
    return dict(nb=nb, m=m, g=g, tw_edge=tw_edge, tw_mid=tw_mid, n_mid=len(mids), freq=freq)


def _cmul(ar, ai, wr, wi):
    return ar * wr - ai * wi, ar * wi + ai * wr


def _slot_pair(bb, lg, nb):
    half0 = nb // 2
    branch = bb // half0
    bf = bb - branch * half0
    i0 = branch * nb + ((bf >> lg) << (lg + 1)) + (bf & ((1 << lg) - 1))
    return bf, i0, i0 + (1 << lg)


def _block_dft_forward(read, w, tw_edge, tw_mid, nb, n_mid):
    r = FFT_R
    half0 = nb // 2
    re, im = slice(0, r), slice(r, 2 * r)

    def first(k, c):
        wr, wi = tw_edge[0, k], tw_edge[1, k]
        ar, ai = read(k)
        br, bi = read(k + half0)
        w[k, re], w[k, im] = ar + br, ai + bi
        dr, di = _cmul(ar - br, ai - bi, wr, wi)
        w[k + half0, re], w[k + half0, im] = dr, di
        a2r, a2i = _cmul(ar, ai, tw_edge[2, k], tw_edge[3, k])
        b2r, b2i = _cmul(br, bi, tw_edge[4, k], tw_edge[5, k])
        w[nb + k, re], w[nb + k, im] = a2r + b2r, a2i + b2i
        dr, di = _cmul(a2r - b2r, a2i - b2i, wr, wi)
        w[nb + k + half0, re], w[nb + k + half0, im] = dr, di
        return c

    lax.fori_loop(0, half0, first, 0)

    for s in range(n_mid):
        lg = int(round(math.log2(half0))) - 1 - s

        def dif(bb, c, s=s, lg=lg):
            bf, i0, i1 = _slot_pair(bb, lg, nb)
            wr, wi = tw_mid[2 * s, bf], tw_mid[2 * s + 1, bf]
            ar, ai, br, bi = w[i0, re], w[i0, im], w[i1, re], w[i1, im]
            w[i0, re], w[i0, im] = ar + br, ai + bi
            dr, di = _cmul(ar - br, ai - bi, wr, wi)
            w[i1, re], w[i1, im] = dr, di
            return c

        lax.fori_loop(0, nb, dif, 0)


def _hy_conv_kernel(*refs, nb, n_mid, n_post):
    tw_edge, tw_mid, a_ref, bias_ref, kh_ref, g_ref, gi_ref = refs[:7]
    post_refs = refs[7:7 + n_post]
    o_ref, w = refs[7 + n_post], refs[8 + n_post]
    r = FFT_R
    m = 2 * nb
    half0 = nb // 2
    re, im = slice(0, r), slice(r, 2 * r)

    def rows(k):
        return pl.ds(pl.multiple_of(k * r, r), r)

    _block_dft_forward(lambda k: (a_ref[0, rows(k), :].astype(F32), a_ref[1, rows(k), :].astype(F32)),
                       w, tw_edge, tw_mid, nb, n_mid)

    def spectral(t, c):
        slots = [FFT_SLOTS * t + u for u in range(FFT_SLOTS)]
        xs = [jnp.dot(g_ref[s], w[s].astype(BF16), preferred_element_type=F32) for s in slots]
        outs = []
        for s, x in zip(slots, xs):
            yr, yi = _cmul(x[re], x[im], kh_ref[0, s, re, :], kh_ref[0, s, im, :])
            y = jnp.concatenate([yr, yi], axis=0).astype(BF16)
            outs.append(jnp.dot(gi_ref[s], y, preferred_element_type=F32))
        for s, o in zip(slots, outs):
            w[s] = o
        return c

    lax.fori_loop(0, m // FFT_SLOTS, spectral, 0)

    for s in reversed(range(n_mid)):
        lg = int(round(math.log2(half0))) - 1 - s

        def dit(bb, c, s=s, lg=lg):
            bf, i0, i1 = _slot_pair(bb, lg, nb)
            wr, wi = tw_mid[2 * s, bf], -tw_mid[2 * s + 1, bf]
            ar, ai = w[i0, re], w[i0, im]
            br, bi = _cmul(w[i1, re], w[i1, im], wr, wi)
            w[i0, re], w[i0, im] = ar + br, ai + bi
            w[i1, re], w[i1, im] = ar - br, ai - bi
            return c

        lax.fori_loop(0, nb, dit, 0)

    bias = bias_ref[0]

    def emit(k, yr, yi):
        for bsel, y in ((0, yr), (1, yi)):
            a = a_ref[bsel, rows(k), :].astype(F32)
            val = y + bias * a
            for p_ref in post_refs:
                val = val * p_ref[bsel, rows(k), :].astype(F32)
            o_ref[bsel, rows(k), :] = val.astype(o_ref.dtype)

    def last(k, c):
        wr, wi = tw_edge[0, k], -tw_edge[1, k]
        ar, ai = w[k, re], w[k, im]
        br, bi = _cmul(w[k + half0, re], w[k + half0, im], wr, wi)
        cr, ci = w[nb + k, re], w[nb + k, im]
        dr, di = _cmul(w[nb + k + half0, re], w[nb + k + half0, im], wr, wi)
        o0r, o0i = _cmul(cr + dr, ci + di, tw_edge[2, k], -tw_edge[3, k])
        o1r, o1i = _cmul(cr - dr, ci - di, tw_edge[4, k], -tw_edge[5, k])
        emit(k, ar + br + o0r, ai + bi + o0i)
        emit(k + half0, ar - br + o1r, ai - bi + o1i)
        return c

    lax.fori_loop(0, half0, last, 0)


def _hy_conv(a, khat, bias, posts, n, out_dtype):
    b, _, width = a[0].shape
    lanes = LANES
    halves = width // lanes
    plan = _fft_plan(n)
    nb, m = plan['nb'], plan['m']
    r = FFT_R

    def window(row0):
        return pl.BlockSpec((pl.Element(2), pl.Element(n), pl.Element(lanes)),
                            lambda hf, pr: (2 * pr, row0, lanes * hf))

    sig = window(a[1])
    smem = pl.BlockSpec(memory_space=pltpu.SMEM)
    once = pl.Buffered(1)
    in_specs = [smem, smem, sig,
                pl.BlockSpec((1, 1, lanes), lambda hf, pr: (hf, 0, 0)),
                pl.BlockSpec((1, m, 2 * r, lanes), lambda hf, pr: (hf, 0, 0, 0), pipeline_mode=once),
                pl.BlockSpec((m, 2 * r, 2 * r), lambda hf, pr: (0, 0, 0), pipeline_mode=once),
                pl.BlockSpec((m, 2 * r, 2 * r), lambda hf, pr: (0, 0, 0), pipeline_mode=once)]
    in_specs += [window(row0) for _, row0 in posts]
    g_fwd = jnp.asarray(plan['g'], BF16)
    g_inv = jnp.asarray(np.swapaxes(plan['g'], 1, 2), BF16)
    return pl.pallas_call(
        functools.partial(_hy_conv_kernel, nb=nb, n_mid=plan['n_mid'], n_post=len(posts)),
        grid=(halves, b // 2),
        in_specs=in_specs,
        out_specs=pl.BlockSpec((2, n, lanes), lambda hf, pr: (pr, 0, hf)),
        out_shape=jax.ShapeDtypeStruct((b, n, width), out_dtype),
        scratch_shapes=[pltpu.VMEM((m, 2 * r, lanes), F32)],
        compiler_params=pltpu.CompilerParams(dimension_semantics=("arbitrary", "arbitrary"),
                                             vmem_limit_bytes=HY_VMEM_LIMIT),
        name="hyena_conv",
    )(jnp.asarray(plan['tw_edge']), jnp.asarray(plan['tw_mid']), a[0], bias, khat, g_fwd, g_inv,
      *[p for p, _ in posts])


_HP = lax.Precision.HIGHEST
HY_TAPS_W = HY_ORDER * 2 * HY_W
HY_FEAT = 128


def _hy_taps_kernel(z_ref, dec_ref, w1_ref, b1_ref, f1_ref, w2_ref, b2_ref, f2_ref, w3_ref, o_ref,
                    h_scr, ss_scr):
    phase, i = pl.program_id(0), pl.program_id(1)
    tm = z_ref.shape[0]
    rows = pl.ds(pl.multiple_of(i * tm, tm), tm)

    @pl.when(phase == 0)
    def _():
        @pl.when(i == 0)
        def _():
            ss_scr[...] = jnp.zeros_like(ss_scr)

        h = jnp.dot(z_ref[...], w1_ref[...], precision=_HP, preferred_element_type=F32)
        h = jnp.sin(f1_ref[...] * (h + b1_ref[...]))
        h = jnp.dot(h, w2_ref[...], precision=_HP, preferred_element_type=F32)
        h = jnp.sin(f2_ref[...] * (h + b2_ref[...]))
        dec = dec_ref[...]
        for c0 in range(0, HY_TAPS_W, HY_W):
            t = jnp.dot(h, w3_ref[:, c0:c0 + HY_W], precision=_HP, preferred_element_type=F32) * dec
            h_scr[rows, c0:c0 + HY_W] = t
            ss_scr[:, c0:c0 + HY_W] += jnp.sum(t * t, axis=0, keepdims=True)

    @pl.when(phase == 1)
    def _():
        first_row = lax.broadcasted_iota(jnp.int32, (tm, HY_W), 0) == jnp.where(i == 0, 0, -1)
        for o in range(HY_ORDER):
            c0 = o * 2 * HY_W
            tot = ss_scr[:, c0:c0 + HY_W] + ss_scr[:, c0 + HY_W:c0 + 2 * HY_W]
            inv = lax.rsqrt(tot + EPS)
            o_ref[:, c0:c0 + HY_W] = h_scr[rows, c0:c0 + HY_W] * inv
            o_ref[:, c0 + HY_W:c0 + 2 * HY_W] = jnp.where(first_row, 0.0, h_scr[rows, c0 + HY_W:c0 + 2 * HY_W] * inv)


@functools.lru_cache(maxsize=None)
def _filter_features(n):
    t = np.linspace(0.0, 1.0, n, dtype=np.float32)[:, None]
    omega = (2.0 * np.pi * np.arange(n, dtype=np.float32)[:, None] / n).astype(np.float32)
    bands = np.linspace(1e-4, HY_BANDS - 1, HY_BANDS, dtype=np.float32)[None, :]
    z = np.zeros((n, HY_FEAT), np.float32)
    z[:, 0:1] = t
    z[:, 1:1 + HY_BANDS] = np.cos(bands * omega)
    z[:, 1 + HY_BANDS:HY_EMB] = -np.sin(bands * omega)
    max_decay = math.log(HY_DECAY_TARGET) / HY_FAST_DECAY
    min_decay = math.log(HY_DECAY_TARGET) / HY_SLOW_DECAY
    deltas = np.linspace(min_decay, max_decay, HY_W, dtype=np.float32)
    dec = np.exp(-t * np.abs(deltas)).astype(np.float32)
    return z, dec


def _hy_taps(n, w1, b1, fr1, w2, b2, fr2, w3):
    z, dec = _filter_features(n)
    tm = min(n, 512)
    pad = HY_FEAT - HY_HIDDEN
    w1p = jnp.pad(w1.astype(F32), ((0, HY_FEAT - HY_EMB), (0, pad)))
    w2p = jnp.pad(w2.astype(F32), ((0, pad), (0, pad)))
    w3p = jnp.pad(w3.astype(F32), ((0, pad), (0, 0)))
    vec = lambda v: jnp.pad(v.astype(F32), (0, pad)).reshape(1, HY_FEAT)
    const = lambda ph, i: (0, 0)
    return pl.pallas_call(
        _hy_taps_kernel,
        grid=(2, n // tm),
        in_specs=[pl.BlockSpec((tm, HY_FEAT), lambda ph, i: (i * (1 - ph), 0)),
                  pl.BlockSpec((tm, HY_W), lambda ph, i: (i * (1 - ph), 0)),
                  pl.BlockSpec((HY_FEAT, HY_FEAT), const), pl.BlockSpec((1, HY_FEAT), const),
                  pl.BlockSpec((1, HY_FEAT), const),
                  pl.BlockSpec((HY_FEAT, HY_FEAT), const), pl.BlockSpec((1, HY_FEAT), const),
                  pl.BlockSpec((1, HY_FEAT), const),
                  pl.BlockSpec((HY_FEAT, HY_TAPS_W), const)],
        out_specs=pl.BlockSpec((tm, HY_TAPS_W), lambda ph, i: (i * ph, 0)),
        out_shape=jax.ShapeDtypeStruct((n, HY_TAPS_W), F32),
        scratch_shapes=[pltpu.VMEM((n, HY_TAPS_W), F32), pltpu.VMEM((1, HY_TAPS_W), F32)],
        compiler_params=_cparams("arbitrary", "arbitrary"),
        name="hyena_taps",
    )(jnp.asarray(z), jnp.asarray(dec), w1p, vec(b1), vec(fr1), w2p, vec(b2), vec(fr2), w3p)


def _hy_spec_kernel(tw_edge, tw_mid, f_ref, b_ref, ghi_ref, glo_ref, o_ref, w, *, nb, n_mid, inv_n):
    r = FFT_R
    m = 2 * nb
    re, im = slice(0, r), slice(r, 2 * r)
    zeros = jnp.zeros((r, LANES), F32)

    def rows(k):
        return pl.ds(pl.multiple_of(k * r, r), r)

    def in_block_dft(s):
        x = w[s]
        xh = x.astype(BF16)
        xl = (x - xh.astype(F32)).astype(BF16)
        gh = ghi_ref[s]
        y = jnp.dot(gh, xh, preferred_element_type=F32)
        y += jnp.dot(gh, xl, preferred_element_type=F32)
        y += jnp.dot(glo_ref[s], xh, preferred_element_type=F32)
        return y * inv_n

    _block_dft_forward(lambda k: (f_ref[rows(k), :], zeros), w, tw_edge, tw_mid, nb, n_mid)

    def fwd_part(s, c):
        o_ref[0, 0, s] = in_block_dft(s)
        return c

    lax.fori_loop(0, m, fwd_part, 0)
    _block_dft_forward(lambda k: (b_ref[rows(k), :], zeros), w, tw_edge, tw_mid, nb, n_mid)

    def bwd_part(s, c):
        y = in_block_dft(s)
        o_ref[0, 0, s, re] += y[re]
        o_ref[0, 0, s, im] -= y[im]
        return c

    lax.fori_loop(0, m, bwd_part, 0)


def _hy_filter_spectrum(taps):
    n = taps.shape[0]
    plan = _fft_plan(n)
    nb, m = plan['nb'], plan['m']
    r = FFT_R
    halves = HY_W // LANES
    g = plan['g']
    g_hi = g.astype(jnp.bfloat16)
    g_lo = (g - np.asarray(g_hi, np.float32)).astype(jnp.bfloat16)
    smem = pl.BlockSpec(memory_space=pltpu.SMEM)
    once = pl.Buffered(1)
    table = lambda o, hf: (0, 0, 0)
    return pl.pallas_call(
        functools.partial(_hy_spec_kernel, nb=nb, n_mid=plan['n_mid'], inv_n=1.0 / (2 * n)),
        grid=(HY_ORDER, halves),
        in_specs=[smem, smem,
                  pl.BlockSpec((n, LANES), lambda o, hf: (0, o * 2 * halves + hf)),
                  pl.BlockSpec((n, LANES), lambda o, hf: (0, o * 2 * halves + halves + hf)),
                  pl.BlockSpec((m, 2 * r, 2 * r), table, pipeline_mode=once),
                  pl.BlockSpec((m, 2 * r, 2 * r), table, pipeline_mode=once)],
        out_specs=pl.BlockSpec((1, 1, m, 2 * r, LANES), lambda o, hf: (o, hf, 0, 0, 0)),
        out_shape=jax.ShapeDtypeStruct((HY_ORDER, halves, m, 2 * r, LANES), F32),
        scratch_shapes=[pltpu.VMEM((m, 2 * r, LANES), F32)],
        compiler_params=pltpu.CompilerParams(dimension_semantics=("arbitrary", "arbitrary"),
                                             vmem_limit_bytes=HY_VMEM_LIMIT),
        name="hyena_spectrum",
    )(jnp.asarray(plan['tw_edge']), jnp.asarray(plan['tw_mid']), taps, taps,
      jnp.asarray(g_hi), jnp.asarray(g_lo))


def _pack_w_in(w_in):
    d = w_in.shape[0]
    z = lambda n: jnp.zeros((d, n), w_in.dtype)
    o = 0
    q_lat = w_in[:, o:o + MLA_Q_RANK]; o += MLA_Q_RANK
    kv_lat = w_in[:, o:o + MLA_KV_RANK]; o += MLA_KV_RANK
    k_rope = w_in[:, o:o + MLA_ROPE]; o += MLA_ROPE
    gate = w_in[:, o:o + GROUP_W]; o += GROUP_W
    mla = jnp.concatenate([q_lat, z(256 - MLA_Q_RANK), kv_lat, z(MLA_NOPE), k_rope,
                           z(MLA_DK - MLA_NOPE - MLA_ROPE), gate], axis=1)
    return jnp.concatenate([mla, w_in[:, o:]], axis=1).astype(BF16)


def _pack_mla_up(w_uq, w_ukv):
    dq = MLA_NOPE + MLA_ROPE
    wq = w_uq.reshape(MLA_Q_RANK, MLA_HEADS, dq).transpose(1, 0, 2)
    wq = jnp.pad(wq, ((0, 0), (0, 256 - MLA_Q_RANK), (0, MLA_DK - dq))).astype(BF16)
    wkv = w_ukv.reshape(MLA_KV_RANK, MLA_HEADS, MLA_NOPE + MLA_V).transpose(1, 0, 2)
    wk = jnp.pad(wkv[:, :, :MLA_NOPE], ((0, 0), (0, 0), (0, MLA_DK - MLA_NOPE))).astype(BF16)
    wvt = jnp.swapaxes(wkv[:, :, MLA_NOPE:], 1, 2).astype(BF16)
    return wq, wk, wvt


def kernel(x, c, ctx, c_ctx, w_mod, b_mod, g_pre, g_post, w_in, w_out, mla_g_cq, mla_w_uq, mla_g_ckv, mla_w_ukv, gqa_g_q, gqa_g_k, ssm_lambda_re, ssm_lambda_im, ssm_log_step, ssm_b_re, ssm_b_im, ssm_c_re, ssm_c_im, ssm_d, ssm_glu_w, ssm_glu_b, hy_conv_w, hy_conv_b, hy_f_w1, hy_f_b1, hy_f_freq1, hy_f_w2, hy_f_b2, hy_f_freq2, hy_f_w3, hy_bias):
    b, n_lat, d = x.shape
    n_ctx = ctx.shape[1]
    depth = w_in.shape[0]
    lt = n_ctx + n_lat
    nct = n_ctx // ROW_TILE
    assert n_ctx % ROW_TILE == 0 and n_lat % ROW_TILE == 0 and b % 2 == 0 and b % 8 == 0

    n_cond = -(-(b + 1) // 8) * 8
    cond = jnp.zeros((n_cond, d), F32).at[:b].set(c).at[b].set(c_ctx)
    mod = _modulation(cond, w_mod, b_mod)[:, :b + 1]
    shift, scale, gate = mod[..., :d], mod[..., d:2 * d], mod[..., 2 * d:]

    mla_cos, mla_sin, gqa_cos, gqa_sin = _rope_tables(n_ctx, n_lat)
    nq = GQA_HEADS * GQA_DIM
    ones_bd = jnp.asarray(np.kron(np.eye(GQA_HEADS), np.ones((GQA_DIM, GQA_DIM))), BF16)

    xa = jnp.concatenate([ctx, x], axis=1)
    for l in range(depth):
        sc = (g_pre[l][None, :] * (1.0 + scale[l]))[:, None, :]
        sh = shift[l][:, None, :]
        wq, wk, wvt = _pack_mla_up(mla_w_uq[l], mla_w_ukv[l])
        g_cq = jnp.pad(mla_g_cq[l], (0, 256 - MLA_Q_RANK)).reshape(1, 256)
        mla_args = (mla_cos, mla_sin, g_cq, mla_g_ckv[l].reshape(1, -1), wq, wk, wvt)
        gqa_args = (gqa_cos, gqa_sin, jnp.tile(gqa_g_q[l], GQA_HEADS).reshape(1, nq),
                    jnp.tile(gqa_g_k[l], GQA_KV_HEADS).reshape(1, -1), ones_bd)
        hy_args = (hy_conv_w[l], hy_conv_b[l].reshape(1, -1))
        gate_m, gate_g, p_ssm, (hv, hx1, hx2, hsg), mla_qkv, gqa_qkv = _inproj(
            xa, sc, sh, _pack_w_in(w_in[l]), nct, mla_args, gqa_args, hy_args)
        a_out = _attention(*mla_qkv, gate_m, 0, n_ctx)
        g_out = _attention(*gqa_qkv, gate_g, 0, n_ctx)

        u_tm = jnp.transpose(p_ssm, (1, 0, 2)).reshape(lt * b, SSM_PACK)
        mats = [_ssm_matrices(ssm_lambda_re[l, di], ssm_lambda_im[l, di], ssm_log_step[l, di],
                              ssm_b_re[l, di], ssm_b_im[l, di], ssm_c_re[l, di], ssm_c_im[l, di])
                for di in range(2)]
        n_ctx_chunks = n_ctx // SSM_T
        y_f = _ssm_direction(u_tm, mats[0], b, n_ctx_chunks, False)
        fin = (y_f, ssm_d[l].reshape(1, -1), ssm_glu_w[l].astype(BF16), ssm_glu_b[l].reshape(1, -1))
        s_tm = _ssm_direction(u_tm, mats[1], b, n_ctx_chunks, True, fin)
        s_out = jnp.transpose(s_tm.reshape(lt, b, GROUP_W), (1, 0, 2))

        filt = (hy_f_w1[l], hy_f_b1[l], hy_f_freq1[l], hy_f_w2[l], hy_f_b2[l], hy_f_freq2[l], hy_f_w3[l])
        bias = hy_bias[l].astype(F32).reshape(HY_ORDER, HY_W // LANES, 1, LANES)
        parts = []
        for row0, n in ((0, n_ctx), (n_ctx, n_lat)):
            khat = _hy_filter_spectrum(_hy_taps(n, *filt))
            z1 = _hy_conv((hv, row0), khat[0], bias[0], ((hx1, row0),), n, BF16)
            parts.append(_hy_conv((z1, 0), khat[1], bias[1], ((hx2, row0), (hsg, row0)), n, BF16))
        y_out = jnp.concatenate(parts, axis=1)

        xa = _outproj(a_out, g_out, s_out, y_out, xa, gate[l][:, None, :], g_post[l].reshape(1, d),
                      w_out[l].astype(BF16), nct, tile0=nct if l == depth - 1 else 0)
    return xa
```

```python
import functools
import math

import numpy as np
import jax
import jax.numpy as jnp
from jax import lax
from jax.experimental import pallas as pl
from jax.experimental.pallas import tpu as pltpu

F32 = jnp.float32
BF16 = jnp.bfloat16

GRID_W = 64
ROPE_BASE = 10000.0
EPS = 1e-6
GROUP_W = 256
MLA_HEADS, MLA_NOPE, MLA_ROPE, MLA_V = 4, 64, 32, 64
MLA_Q_RANK, MLA_KV_RANK = 192, 128
GQA_HEADS, GQA_KV_HEADS, GQA_DIM = 4, 2, 64
SSM_GROUPS, SSM_GROUP, SSM_STATE = 16, 16, 64
HY_W, HY_ORDER, HY_EMB, HY_HIDDEN = 256, 2, 33, 64
HY_BANDS = (HY_EMB - 1) // 2
HY_FAST_DECAY, HY_SLOW_DECAY, HY_DECAY_TARGET = 0.3, 1.5, 1e-2

MLA_PACK = 768
GQA_PACK = 768
SSM_PACK = 512
HY_PACK = 1024
IN_PACK = MLA_PACK + GQA_PACK + SSM_PACK + HY_PACK
MLA_DK = 128

LANES = 128
ROW_TILE = 256
ATT_TQ = 256
ATT_TK = 256
ATT_UNROLLS = (16, 8, 4, 2)
ONES_ROWS = 16
SSM_T = 128
SSM_LANE_SPLIT = 512
FFT_R = 128
FFT_SLOTS = 4
VMEM_LIMIT = 52 * 1024 * 1024
HY_VMEM_LIMIT = 58 * 1024 * 1024


def _cparams(*sem):
    return pltpu.CompilerParams(dimension_semantics=sem, vmem_limit_bytes=VMEM_LIMIT)


def _silu(x):
    return x * jax.nn.sigmoid(x)


def _mod_kernel(c_ref, w_ref, b_ref, o_ref):
    c = c_ref[...]
    s = _silu(c).astype(BF16)
    o_ref[0] = jnp.dot(s, w_ref[0].astype(BF16), preferred_element_type=F32) + b_ref[0]


def _modulation(cond, w_mod, b_mod):
    depth, d, n3 = w_mod.shape
    r = cond.shape[0]
    tn = 512
    return pl.pallas_call(
        _mod_kernel,
        grid=(depth, n3 // tn),
        in_specs=[pl.BlockSpec((r, d), lambda l, j: (0, 0)),
                  pl.BlockSpec((1, d, tn), lambda l, j: (l, 0, j)),
                  pl.BlockSpec((1, 1, tn), lambda l, j: (l, 0, j))],
        out_specs=pl.BlockSpec((1, r, tn), lambda l, j: (l, 0, j)),
        out_shape=jax.ShapeDtypeStruct((depth, r, n3), F32),
        compiler_params=_cparams("parallel", "parallel"),
        name="modulation",
    )(cond, w_mod, b_mod.reshape(depth, 1, n3))


def _outproj_kernel(a_ref, g_ref, s_ref, y_ref, x_ref, gt_ref, gp_ref, w_ref, o_ref):
    half = a_ref.shape[1] // 2
    for hs in (slice(0, half), slice(half, 2 * half)):
        acc = jnp.dot(a_ref[0, hs], w_ref[0:GROUP_W], preferred_element_type=F32)
        acc += jnp.dot(g_ref[0, hs], w_ref[GROUP_W:2 * GROUP_W], preferred_element_type=F32)
        acc += jnp.dot(s_ref[0, hs], w_ref[2 * GROUP_W:3 * GROUP_W], preferred_element_type=F32)
        acc += jnp.dot(y_ref[0, hs], w_ref[3 * GROUP_W:4 * GROUP_W], preferred_element_type=F32)
        ms = jnp.mean(acc * acc, axis=-1, keepdims=True)
        o_ref[0, hs] = x_ref[0, hs] + gt_ref[0] * (acc * lax.rsqrt(ms + EPS) * gp_ref[...])


def _outproj(a, g, s, y, xa, gate, g_post, w, nct, tile0=0):
    b, lt, d = xa.shape
    tm = ROW_TILE

    def mod_idx(bi, i):
        return (jnp.where(i + tile0 < nct, b, bi), 0, 0)

    row = lambda bi, i: (bi, i + tile0, 0)
    return pl.pallas_call(
        _outproj_kernel,
        grid=(b, lt // tm - tile0),
        in_specs=[pl.BlockSpec((1, tm, GROUP_W), row)] * 4 + [
            pl.BlockSpec((1, tm, d), row),
            pl.BlockSpec((1, 1, d), mod_idx),
            pl.BlockSpec((1, d), lambda bi, i: (0, 0)),
            pl.BlockSpec((4 * GROUP_W, d), lambda bi, i: (0, 0))],
        out_specs=pl.BlockSpec((1, tm, d), lambda bi, i: (bi, i, 0)),
        out_shape=jax.ShapeDtypeStruct((b, lt - tile0 * tm, d), F32),
        compiler_params=_cparams("parallel", "parallel"),
        name="outproj",
    )(a, g, s, y, xa, gate, g_post, w)


def _rope(x, cos, sin, shift):
    w = x.shape[-1]
    lane = lax.broadcasted_iota(jnp.int32, x.shape, 1)
    first = (lane & shift) == 0
    swapped = jnp.where(first, -pltpu.roll(x, w - shift, 1), pltpu.roll(x, shift, 1))
    return x * cos + swapped * sin


def _rope_tables(n_ctx, n_lat):
    t = np.arange(n_lat)
    row = (t // GRID_W).astype(np.float64)
    col = (t % GRID_W).astype(np.float64)

    def block(pos, h):
        inv = ROPE_BASE ** (-np.arange(0, h, 2, dtype=np.float64) / h)
        ang = (pos[:, None].astype(np.float32) * inv[None, :].astype(np.float32)).astype(np.float32)
        c, s = np.cos(ang), np.sin(ang)
        return np.concatenate([c, c], -1), np.concatenate([s, s], -1)

    def full(h, lead, width):
        cr, sr = block(row, h)
        cc, sc = block(col, h)
        cos = np.ones((n_ctx + n_lat, width), np.float32)
        sin = np.zeros((n_ctx + n_lat, width), np.float32)
        cos[n_ctx:, lead:lead + 2 * h] = np.concatenate([cr, cc], -1)
        sin[n_ctx:, lead:lead + 2 * h] = np.concatenate([sr, sc], -1)
        return cos, sin

    mc, ms = full(MLA_ROPE // 2, MLA_NOPE, MLA_DK)
    gc, gs = full(GQA_DIM // 2, 0, GQA_DIM)
    reps = GQA_HEADS
    return (jnp.asarray(mc), jnp.asarray(ms),
            jnp.asarray(np.tile(gc, (1, reps))), jnp.asarray(np.tile(gs, (1, reps))))


_NT = (((1,), (1,)), ((), ()))
_TN = (((0,), (0,)), ((), ()))
LOG2E = math.log2(math.e)


def _mla_prep_body(p, cos_ref, sin_ref, gq_ref, gkv_ref, wq_ref, wk_ref, wvt_ref, eye_ref,
                   qt_ref, k_ref, vt_ref):
    cos, sin = cos_ref[...], sin_ref[...]
    ql = p[:, 0:256]
    rq = lax.rsqrt(jnp.sum(ql * ql, axis=-1, keepdims=True) * (1.0 / MLA_Q_RANK) + EPS)
    qn = (ql * rq * gq_ref[...]).astype(BF16)
    kvl = p[:, 256:384]
    rk = lax.rsqrt(jnp.mean(kvl * kvl, axis=-1, keepdims=True) + EPS)
    kvn = (kvl * rk * gkv_ref[...]).astype(BF16)
    k_rope = _rope(p[:, 384:512], cos, sin, MLA_ROPE // 4)
    scale = (MLA_NOPE + MLA_ROPE) ** -0.5 * LOG2E
    heads = range(MLA_HEADS)
    qs = [jnp.dot(qn, wq_ref[h], preferred_element_type=F32) for h in heads]
    ks = [jnp.dot(kvn, wk_ref[h], preferred_element_type=F32) for h in heads]
    vts = [lax.dot_general(wvt_ref[h], kvn, _NT, preferred_element_type=F32) for h in heads]
    for h in heads:
        k_ref[0, h] = (ks[h] + k_rope).astype(BF16)
        vt_ref[0, h, 0, 0:MLA_V] = vts[h].astype(BF16)
        vt_ref[0, h, 0, MLA_V:MLA_V + ONES_ROWS] = jnp.ones((ONES_ROWS, kvn.shape[0]), BF16)
    qr = [(_rope(q, cos, sin, MLA_ROPE // 4) * scale).astype(BF16) for q in qs]
    for h in heads:
        qt_ref[0, h] = lax.dot_general(eye_ref[...], qr[h], _NT, preferred_element_type=F32).astype(BF16)


def _head_mean_sq(x, ones_bd):
    sq = x * x
    hi = sq.astype(BF16)
    lo = (sq - hi.astype(F32)).astype(BF16)
    s = jnp.dot(hi, ones_bd, preferred_element_type=F32) + jnp.dot(lo, ones_bd, preferred_element_type=F32)
    return s * (1.0 / GQA_DIM)


def _gqa_prep_body(p, cos_ref, sin_ref, gq_ref, gk_ref, ones_ref, eye_ref, qt_ref, k_ref, vt_ref):
    cos, sin = cos_ref[...], sin_ref[...]
    nq = GQA_HEADS * GQA_DIM
    nk = GQA_KV_HEADS * GQA_DIM
    q = p[:, 0:nq]
    k = p[:, nq:nq + nk]
    v = p[:, nq + nk:nq + 2 * nk].astype(BF16)
    q_ms = _head_mean_sq(q, ones_ref[...])
    k_ms = _head_mean_sq(k, ones_ref[0:nk, 0:nk])
    for h in range(GQA_KV_HEADS):
        sel = eye_ref[h * GQA_DIM:(h + 1) * GQA_DIM, 0:nk]
        vt_ref[0, h, 0, 0:GQA_DIM] = lax.dot_general(sel, v, _NT, preferred_element_type=F32).astype(BF16)
        vt_ref[0, h, 0, GQA_DIM:GQA_DIM + ONES_ROWS] = jnp.ones((ONES_ROWS, v.shape[0]), BF16)
    qn = q * lax.rsqrt(q_ms + EPS) * gq_ref[...]
    qr = (_rope(qn, cos, sin, GQA_DIM // 4) * (GQA_DIM ** -0.5 * LOG2E)).astype(BF16)
    kn = k * lax.rsqrt(k_ms + EPS) * gk_ref[...]
    kr = _rope(kn, cos[:, 0:nk], sin[:, 0:nk], GQA_DIM // 4)
    for h in range(GQA_HEADS):
        sel = eye_ref[h * GQA_DIM:(h + 1) * GQA_DIM, :]
        qt_ref[0, h] = lax.dot_general(sel, qr, _NT, preferred_element_type=F32).astype(BF16)
    for h in range(GQA_KV_HEADS):
        k_ref[0, h] = kr[:, h * GQA_DIM:(h + 1) * GQA_DIM].astype(BF16)


def _inproj_kernel(x_ref, xp_ref, xn_ref, sc_ref, sh_ref, w_ref,
                   mcos_ref, msin_ref, gcq_ref, gckv_ref, wq_ref, wk_ref, wvt_ref, eye_m_ref,
                   gcos_ref, gsin_ref, gq_ref, gk_ref, ones_ref, eye_g_ref, cw_ref, cb_ref,
                   gate_m_ref, gate_g_ref, ssm_ref, hv_ref, hx1_ref, hx2_ref, hsg_ref,
                   mqt_ref, mk_ref, mvt_ref, gqt_ref, gk_out_ref, gvt_ref, *, nct, n_tiles):
    i = pl.program_id(1)

    tm = x_ref.shape[1]

    def normed(x):
        ms = jnp.mean(x * x, axis=-1, keepdims=True)
        return (x * lax.rsqrt(ms + EPS) * sc_ref[0] + sh_ref[0]).astype(BF16)

    halo = normed(jnp.concatenate([xp_ref[0], xn_ref[0]], axis=0))
    hb = normed(x_ref[0])

    def proj(c0, width, lhs=hb):
        return jnp.dot(lhs, w_ref[:, c0:c0 + width], preferred_element_type=F32)

    qkv = 2 * GROUP_W
    hy0 = MLA_PACK + GQA_PACK + SSM_PACK
    p_mla = proj(0, qkv)
    p_gqa = proj(MLA_PACK, qkv)
    p_hy_ext = proj(hy0, HY_PACK, jnp.concatenate([hb, halo], axis=0))
    p_hy = p_hy_ext[0:tm]
    halo_prev = p_hy_ext[tm + 7:tm + 8, 0:3 * HY_W]
    halo_next = p_hy_ext[tm + 8:tm + 9, 0:3 * HY_W]
    gate_m_ref[0] = proj(qkv, GROUP_W).astype(BF16)
    gate_g_ref[0] = proj(MLA_PACK + qkv, GROUP_W).astype(BF16)
    ssm_ref[0] = proj(MLA_PACK + GQA_PACK, SSM_PACK).astype(BF16)
    seq_start = (i == 0) | (i == nct)
    seq_end = (i == nct - 1) | (i == n_tiles - 1)
    _hy_pre_body(p_hy, jnp.where(seq_start, 0.0, halo_prev), jnp.where(seq_end, 0.0, halo_next),
                 cw_ref, cb_ref, hv_ref, hx1_ref, hx2_ref, hsg_ref)
    _mla_prep_body(p_mla, mcos_ref, msin_ref, gcq_ref, gckv_ref, wq_ref, wk_ref, wvt_ref, eye_m_ref,
                   mqt_ref, mk_ref, mvt_ref)
    _gqa_prep_body(p_gqa, gcos_ref, gsin_ref, gq_ref, gk_ref, ones_ref, eye_g_ref,
                   gqt_ref, gk_out_ref, gvt_ref)


def _inproj(xa, scale, shift, w, nct, mla_args, gqa_args, hy_args):
    b, lt, d = xa.shape
    tm = ROW_TILE
    nq = GQA_HEADS * GQA_DIM
    nk = GQA_KV_HEADS * GQA_DIM
    n_tiles = lt // tm
    sub = tm // 8

    def mod_idx(bi, i):
        return (jnp.where(i < nct, b, bi), 0, 0)

    const2 = lambda bi, i: (0, 0)
    const3 = lambda bi, i: (0, 0, 0)
    row = lambda bi, i: (bi, i, 0)
    pos = lambda bi, i: (i, 0)
    qt_idx = lambda bi, i: (bi, 0, 0, i)
    k_idx = lambda bi, i: (bi, 0, i, 0)
    vt_idx = lambda bi, i: (bi, 0, i, 0, 0)
    in_specs = [pl.BlockSpec((1, tm, d), row),
                pl.BlockSpec((1, 8, d), lambda bi, i: (bi, jnp.maximum(i * sub - 1, 0), 0)),
                pl.BlockSpec((1, 8, d), lambda bi, i: (bi, jnp.minimum((i + 1) * sub, lt // 8 - 1), 0)),
                pl.BlockSpec((1, 1, d), mod_idx), pl.BlockSpec((1, 1, d), mod_idx),
                pl.BlockSpec((d, IN_PACK), const2),
                pl.BlockSpec((tm, MLA_DK), pos), pl.BlockSpec((tm, MLA_DK), pos),
                pl.BlockSpec((1, 256), const2), pl.BlockSpec((1, MLA_KV_RANK), const2),
                pl.BlockSpec((MLA_HEADS, 256, MLA_DK), const3),
                pl.BlockSpec((MLA_HEADS, MLA_KV_RANK, MLA_DK), const3),
                pl.BlockSpec((MLA_HEADS, MLA_V, MLA_KV_RANK), const3),
                pl.BlockSpec((MLA_DK, MLA_DK), const2),
                pl.BlockSpec((tm, nq), pos), pl.BlockSpec((tm, nq), pos),
                pl.BlockSpec((1, nq), const2), pl.BlockSpec((1, nk), const2),
                pl.BlockSpec((nq, nq), const2), pl.BlockSpec((nq, nq), const2),
                pl.BlockSpec((3, 3 * HY_W), const2), pl.BlockSpec((1, 3 * HY_W), const2)]
    out_specs = [pl.BlockSpec((1, tm, GROUP_W), row), pl.BlockSpec((1, tm, GROUP_W), row),
                 pl.BlockSpec((1, tm, SSM_PACK), row)] + [pl.BlockSpec((1, tm, HY_W), row)] * 4 + [
                 pl.BlockSpec((1, MLA_HEADS, MLA_DK, tm), qt_idx),
                 pl.BlockSpec((1, MLA_HEADS, tm, MLA_DK), k_idx),
                 pl.BlockSpec((1, MLA_HEADS, 1, MLA_V + ONES_ROWS, tm), vt_idx),
                 pl.BlockSpec((1, GQA_HEADS, GQA_DIM, tm), qt_idx),
                 pl.BlockSpec((1, GQA_KV_HEADS, tm, GQA_DIM), k_idx),
                 pl.BlockSpec((1, GQA_KV_HEADS, 1, GQA_DIM + ONES_ROWS, tm), vt_idx)]
    out_shape = [jax.ShapeDtypeStruct((b, lt, GROUP_W), BF16), jax.ShapeDtypeStruct((b, lt, GROUP_W), BF16),
                 jax.ShapeDtypeStruct((b, lt, SSM_PACK), BF16)] + [
                 jax.ShapeDtypeStruct((b, lt, HY_W), BF16)] * 4 + [
                 jax.ShapeDtypeStruct((b, MLA_HEADS, MLA_DK, lt), BF16),
                 jax.ShapeDtypeStruct((b, MLA_HEADS, lt, MLA_DK), BF16),
                 jax.ShapeDtypeStruct((b, MLA_HEADS, n_tiles, MLA_V + ONES_ROWS, tm), BF16),
                 jax.ShapeDtypeStruct((b, GQA_HEADS, GQA_DIM, lt), BF16),
                 jax.ShapeDtypeStruct((b, GQA_KV_HEADS, lt, GQA_DIM), BF16),
                 jax.ShapeDtypeStruct((b, GQA_KV_HEADS, n_tiles, GQA_DIM + ONES_ROWS, tm), BF16)]
    outs = pl.pallas_call(
        functools.partial(_inproj_kernel, nct=nct, n_tiles=n_tiles),
        grid=(b, n_tiles),
        in_specs=in_specs,
        out_specs=out_specs,
        out_shape=out_shape,
        compiler_params=_cparams("parallel", "parallel"),
        name="inproj",
    )(xa, xa, xa, scale, shift, w, *mla_args, jnp.eye(MLA_DK, dtype=BF16), *gqa_args,
      jnp.eye(nq, dtype=BF16), *hy_args)
    return outs[0], outs[1], outs[2], tuple(outs[3:7]), tuple(outs[7:10]), tuple(outs[10:13])


def _attn_kernel(qt_ref, qn_ref, k_ref, vt_ref, gate_ref, eye_ref, o_ref, sa_ref, sb_ref, pa_ref, pb_ref,
                 cm_ref, acc_ref, *, heads, group, nct_q, ctx_chunks, all_chunks, unroll):
    i = pl.program_id(1)
    n_loops = jnp.where(i < nct_q, (ctx_chunks - 1) // unroll, (all_chunks - 1) // unroll)
    tq = qt_ref.shape[3]
    dva = vt_ref.shape[3]
    dv = dva - ONES_ROWS

    def score(j, h, dst, q_ref=qt_ref):
        rows = pl.ds(pl.multiple_of(j * ATT_TK, ATT_TK), ATT_TK)
        s = jnp.dot(k_ref[0, h // group, rows, :], q_ref[0, h], preferred_element_type=F32)
        dst[h] = s
        return jnp.max(s, axis=0, keepdims=True)

    def value(j, h, p_ref):
        return jnp.dot(vt_ref[0, h // group, j], p_ref[h], preferred_element_type=F32)

    def step(j, carry, s_cur, s_nxt, p_prev, p_cur):
        state, cmax = carry
        new, nmax = [], []
        for h in range(heads):
            pv = value(jnp.maximum(j - 1, 0), h, p_prev)
            if s_nxt is not None:
                nmax.append(score(j + 1, h, s_nxt))
            m = state[h]
            m_new = jnp.maximum(m, cmax[h])
            alpha = jnp.exp2(m - m_new)
            p_cur[h] = jnp.exp2(s_cur[h] - m_new).astype(BF16)
            acc_ref[h] = alpha * (acc_ref[h] + pv)
            new.append(m_new)
        return tuple(new), tuple(nmax)

    bufs = ((sa_ref, sb_ref, pb_ref, pa_ref), (sb_ref, sa_ref, pa_ref, pb_ref))

    def body(t, carry):
        for u in range(unroll):
            carry = step(unroll * t + u, carry, *bufs[u % 2])
        return carry

    @pl.when(i == 0)
    def _():
        for h in range(heads):
            cm_ref[h] = score(0, h, sa_ref)

    cmax0 = tuple(cm_ref[h] for h in range(heads))
    pb_ref[...] = jnp.zeros_like(pb_ref)
    acc_ref[...] = jnp.zeros_like(acc_ref)
    init = tuple(jnp.full((1, tq), -jnp.inf, F32) for _ in range(heads))
    carry = lax.fori_loop(0, n_loops, body, (init, cmax0))
    last = unroll * n_loops
    step(last, carry, sa_ref, None, pb_ref, pa_ref)
    pvs = [value(last, h, pa_ref) for h in range(heads)]
    for h in range(heads):
        cm_ref[h] = score(0, h, sa_ref, qn_ref)
    outs = []
    for h in range(heads):
        acc = acc_ref[h] + pvs[h]
        outs.append((acc[0:dv] * (1.0 / acc[dv:dv + 1])).astype(BF16))
    out = lax.dot_general(jnp.concatenate(outs, axis=0), eye_ref[...], _TN, preferred_element_type=F32)
    g = gate_ref[0].astype(F32)
    o_ref[0] = (out * _silu(g)).astype(BF16)


def _attention(qt, k, vt, gate_src, gate_block, n_ctx):
    b, heads, dk, lt = qt.shape
    hk, n_chunks, dva = k.shape[1], vt.shape[2], vt.shape[3]
    assert heads * (dva - ONES_ROWS) == GROUP_W
    n_tiles = lt // ATT_TQ
    ctx_chunks, all_chunks = n_ctx // ATT_TK, lt // ATT_TK
    unroll = next(u for u in ATT_UNROLLS if (ctx_chunks - 1) % u == 0 and (all_chunks - 1) % u == 0)
    kern = functools.partial(_attn_kernel, heads=heads, group=heads // hk, nct_q=n_ctx // ATT_TQ,
                             ctx_chunks=ctx_chunks, all_chunks=all_chunks, unroll=unroll)
    return pl.pallas_call(
        kern,
        grid=(b, n_tiles),
        in_specs=[pl.BlockSpec((1, heads, dk, ATT_TQ), lambda bi, i: (bi, 0, 0, i)),
                  pl.BlockSpec((1, heads, dk, ATT_TQ), lambda bi, i: (bi, 0, 0, jnp.minimum(i + 1, n_tiles - 1))),
                  pl.BlockSpec((1, hk, lt, dk), lambda bi, i: (bi, 0, 0, 0)),
                  pl.BlockSpec((1, hk, n_chunks, dva, ATT_TK), lambda bi, i: (bi, 0, 0, 0, 0)),
                  pl.BlockSpec((1, ATT_TQ, GROUP_W), lambda bi, i: (bi, i, gate_block)),
                  pl.BlockSpec((GROUP_W, GROUP_W), lambda bi, i: (0, 0))],
        out_specs=pl.BlockSpec((1, ATT_TQ, GROUP_W), lambda bi, i: (bi, i, 0)),
        out_shape=jax.ShapeDtypeStruct((b, lt, GROUP_W), BF16),
        scratch_shapes=[pltpu.VMEM((heads, ATT_TK, ATT_TQ), F32), pltpu.VMEM((heads, ATT_TK, ATT_TQ), F32),
                        pltpu.VMEM((heads, ATT_TK, ATT_TQ), BF16), pltpu.VMEM((heads, ATT_TK, ATT_TQ), BF16),
                        pltpu.VMEM((heads, 1, ATT_TQ), F32), pltpu.VMEM((heads, dva, ATT_TQ), F32)],
        compiler_params=_cparams("parallel", "arbitrary"),
        name="attention",
    )(qt, qt, k, vt, gate_src, jnp.eye(GROUP_W, dtype=BF16))


def _ssm_kernel(*refs, steps, nb, reverse, finish):
    if finish:
        u_ref, b_ref, a_ref, c_ref, yf_ref, d_ref, gw_ref, gb_ref, o_ref, st, car = refs
    else:
        u_ref, b_ref, a_ref, c_ref, o_ref, st, car = refs

    @pl.when(pl.program_id(0) == 0)
    def _():
        car[...] = jnp.zeros_like(car)

    half = st.shape[0] // 2
    halves = (slice(0, half), slice(half, 2 * half))
    for hs in halves:
        st[hs] = jnp.dot(u_ref[hs, 0:GROUP_W], b_ref[...], preferred_element_type=F32)
    n_state = st.shape[1] // 2
    lw = min(SSM_LANE_SPLIT, n_state)
    for c0 in range(0, n_state, lw):
        lr = slice(c0, c0 + lw)
        li = slice(n_state + c0, n_state + c0 + lw)
        ar = jnp.broadcast_to(a_ref[:, lr], (nb, lw))
        ai = jnp.broadcast_to(a_ref[:, li], (nb, lw))

        def body(tt, carry, lr=lr, li=li, ar=ar, ai=ai):
            sr, si = carry
            t = (steps - 1 - tt) if reverse else tt
            rows = pl.ds(pl.multiple_of(t * nb, nb), nb)
            nr = ar * sr - ai * si + st[rows, lr]
            ni = ar * si + ai * sr + st[rows, li]
            st[rows, lr] = nr
            st[rows, li] = ni
            return nr, ni

        sr, si = lax.fori_loop(0, steps, body, (car[:, lr], car[:, li]), unroll=2)
        car[:, lr] = sr
        car[:, li] = si
    ys = [jnp.dot(st[hs].astype(BF16), c_ref[...], preferred_element_type=F32) for hs in halves]
    if not finish:
        for hs, y in zip(halves, ys):
            o_ref[hs] = y
    else:
        zs = [jax.nn.gelu(y + yf_ref[hs] + d_ref[...] * u_ref[hs, 0:GROUP_W].astype(F32), approximate=True)
              for hs, y in zip(halves, ys)]
        gls = [jnp.dot(z.astype(BF16), gw_ref[...], preferred_element_type=F32) + gb_ref[...] for z in zs]
        for hs, z, gl in zip(halves, zs, gls):
            gate = u_ref[hs, GROUP_W:2 * GROUP_W].astype(F32)
            o_ref[hs] = (z * jax.nn.sigmoid(gl) * _silu(gate)).astype(BF16)


def _ssm_direction(u_tm, mats, nb, n_ctx_chunks, reverse, finish_args=None):
    rows, _ = u_tm.shape
    steps = SSM_T
    blk = steps * nb
    n_chunks = rows // blk
    n_state2 = mats[0].shape[1]
    if reverse:
        cidx = lambda i: (jnp.where(i < n_ctx_chunks, n_ctx_chunks - 1 - i, n_chunks - 1 - (i - n_ctx_chunks)), 0)
    else:
        cidx = lambda i: (i, 0)
    const = lambda i: (0, 0)
    in_specs = [pl.BlockSpec((blk, SSM_PACK), cidx)] + [pl.BlockSpec(m.shape, const) for m in mats]
    args = [u_tm, *mats]
    finish = finish_args is not None
    if finish:
        yf, d_skip, glu_w, glu_b = finish_args
        in_specs += [pl.BlockSpec((blk, GROUP_W), cidx), pl.BlockSpec(d_skip.shape, const),
                     pl.BlockSpec(glu_w.shape, const), pl.BlockSpec(glu_b.shape, const)]
        args += [yf, d_skip, glu_w, glu_b]
    return pl.pallas_call(
        functools.partial(_ssm_kernel, steps=steps, nb=nb, reverse=reverse, finish=finish),
        grid=(n_chunks,),
        in_specs=in_specs,
        out_specs=pl.BlockSpec((blk, GROUP_W), cidx),
        out_shape=jax.ShapeDtypeStruct((rows, GROUP_W), BF16 if finish else F32),
        scratch_shapes=[pltpu.VMEM((blk, n_state2), F32), pltpu.VMEM((nb, n_state2), F32)],
        compiler_params=_cparams("arbitrary"),
        name="ssm_rev" if reverse else "ssm_fwd",
    )(*args)


def _ssm_matrices(lam_re, lam_im, log_step, b_re, b_im, c_re, c_im):
    lr, li = lam_re.astype(F32), lam_im.astype(F32)
    step = jnp.exp(log_step.astype(F32))[:, None]
    mag = jnp.exp(lr * step)
    a_re, a_im = mag * jnp.cos(li * step), mag * jnp.sin(li * step)
    den = lr * lr + li * li
    q_re = ((a_re - 1.0) * lr + a_im * li) / den
    q_im = (a_im * lr - (a_re - 1.0) * li) / den
    bb_re, bb_im = b_re.astype(F32), b_im.astype(F32)
    bbar_re = q_re[..., None] * bb_re - q_im[..., None] * bb_im
    bbar_im = q_re[..., None] * bb_im + q_im[..., None] * bb_re
    eye = jnp.eye(SSM_GROUPS, dtype=F32)
    n_in = SSM_GROUPS * SSM_GROUP
    n_state = SSM_GROUPS * SSM_STATE

    def drive(m):
        return jnp.einsum('gph,gk->ghkp', m, eye).reshape(n_in, n_state).astype(BF16)

    def readout(m):
        return jnp.einsum('ghp,gk->gpkh', m, eye).reshape(n_state, n_in).astype(BF16)

    return (jnp.concatenate([drive(bbar_re), drive(bbar_im)], axis=1),
            jnp.concatenate([a_re.reshape(1, n_state), a_im.reshape(1, n_state)], axis=1),
            jnp.concatenate([readout(c_re.astype(F32)), readout(-c_im.astype(F32))], axis=0))


def _hy_pre_body(p, prev_row, next_row, w_ref, b_ref, v_ref, x1_ref, x2_ref, sg_ref):
    nconv = 3 * HY_W
    x = p[:, 0:nconv]
    tm = x.shape[0]
    rid = lax.broadcasted_iota(jnp.int32, x.shape, 0)
    xm = jnp.where(rid == 0, prev_row, pltpu.roll(x, 1, 0))
    xp = jnp.where(rid == tm - 1, next_row, pltpu.roll(x, tm - 1, 0))
    proj = xm * w_ref[0:1] + x * w_ref[1:2] + xp * w_ref[2:3] + b_ref[...]
    for o_ref, val in ((v_ref, proj[:, 0:HY_W]), (x1_ref, proj[:, HY_W:2 * HY_W]),
                       (x2_ref, proj[:, 2 * HY_W:3 * HY_W]), (sg_ref, _silu(p[:, nconv:nconv + HY_W]))):
        o_ref[0] = val.astype(BF16)


def _bitrev(p, bits):
    r = 0
    for i in range(bits):
        r = (r << 1) | ((p >> i) & 1)
    return r


@functools.lru_cache(maxsize=None)
def _fft_plan(n):
    r = FFT_R
    nb = n // r
    m = 2 * nb
    big = 2 * n
    bits = int(round(math.log2(nb)))
    assert nb >= 2 and (1 << bits) == nb
    jmap = np.zeros(m, np.int64)
    for p in range(nb):
        jmap[p] = 2 * _bitrev(p, bits)
        jmap[nb + p] = 2 * _bitrev(p, bits) + 1
    k2 = np.arange(r)
    n2 = np.arange(r)
    g = np.zeros((m, 2 * r, 2 * r), np.float32)
    for s in range(m):
        ang = -2.0 * np.pi * (np.outer(k2, n2) / r + np.outer(np.ones(r), n2) * jmap[s] / big)
        gr, gi = np.cos(ang), np.sin(ang)
        g[s] = np.block([[gr, -gi], [gi, gr]])
    half0 = nb // 2
    k = np.arange(half0)
    wnb = np.exp(-2j * np.pi * k / nb)
    wm0 = np.exp(-2j * np.pi * k / m)
    wm1 = np.exp(-2j * np.pi * (k + half0) / m)
    tw_edge = np.stack([wnb.real, wnb.imag, wm0.real, wm0.imag, wm1.real, wm1.imag]).astype(np.float32)
    mids = []
    h = half0 // 2
    while h >= 1:
        kk = np.arange(nb // 2) % h
        w = np.exp(-2j * np.pi * kk / (2 * h))
        mids.append(np.stack([w.real, w.imag]))
        h //= 2
    tw_mid = (np.concatenate(mids, 0) if mids else np.zeros((2, max(nb // 2, 1)))).astype(np.float32)
    freq = (jmap[:, None] + m * k2[None, :])
    return dict(nb=nb, m=m, g=g, tw_edge=tw_edge, tw_mid=tw_mid, n_mid=len(mids), freq=freq)


def _cmul(ar, ai, wr, wi):
    return ar * wr - ai * wi, ar * wi + ai * wr


def _slot_pair(bb, lg, nb):
    half0 = nb // 2
    branch = bb // half0
    bf = bb - branch * half0
    i0 = branch * nb + ((bf >> lg) << (lg + 1)) + (bf & ((1 << lg) - 1))
    return bf, i0, i0 + (1 << lg)


def _block_dft_forward(read, w, tw_edge, tw_mid, nb, n_mid):
    r = FFT_R
    half0 = nb // 2
    re, im = slice(0, r), slice(r, 2 * r)

    def first(k, c):
        wr, wi = tw_edge[0, k], tw_edge[1, k]
        ar, ai = read(k)
        br, bi = read(k + half0)
        w[k, re], w[k, im] = ar + br, ai + bi
        dr, di = _cmul(ar - br, ai - bi, wr, wi)
        w[k + half0, re], w[k + half0, im] = dr, di
        a2r, a2i = _cmul(ar, ai, tw_edge[2, k], tw_edge[3, k])
        b2r, b2i = _cmul(br, bi, tw_edge[4, k], tw_edge[5, k])
        w[nb + k, re], w[nb + k, im] = a2r + b2r, a2i + b2i
        dr, di = _cmul(a2r - b2r, a2i - b2i, wr, wi)
        w[nb + k + half0, re], w[nb + k + half0, im] = dr, di
        return c

    lax.fori_loop(0, half0, first, 0)
    _mid_stages(w, tw_mid, nb, n_mid, inverse=False)


def _mid_stages(w, tw_mid, nb, n_mid, inverse):
    r = FFT_R
    half0 = nb // 2
    re, im = slice(0, r), slice(r, 2 * r)
    lg0 = int(round(math.log2(half0))) - 1
    sign = -1.0 if inverse else 1.0

    def tw(s, k):
        return tw_mid[2 * s, k], sign * tw_mid[2 * s + 1, k]

    def bfly(a, b, t):
        (ar, ai), (br, bi) = a, b
        if inverse:
            br, bi = _cmul(br, bi, *t)
            return (ar + br, ai + bi), (ar - br, ai - bi)
        return (ar + br, ai + bi), _cmul(ar - br, ai - bi, *t)

    def single(s):
        lg = lg0 - s

        def body(bb, c):
            bf, i0, i1 = _slot_pair(bb, lg, nb)
            x0, x1 = bfly((w[i0, re], w[i0, im]), (w[i1, re], w[i1, im]), tw(s, bf))
            w[i0, re], w[i0, im] = x0
            w[i1, re], w[i1, im] = x1
            return c

        lax.fori_loop(0, nb, body, 0)

    def double(s):
        lg2 = lg0 - s - 1
        h2 = 1 << lg2
        quarter = nb // 4

        def body(bb, c):
            branch = bb // quarter
            u = bb - branch * quarter
            k = u & (h2 - 1)
            i0 = branch * nb + ((u >> lg2) << (lg2 + 2)) + k
            idx = [i0, i0 + h2, i0 + 2 * h2, i0 + 3 * h2]
            x = [(w[i, re], w[i, im]) for i in idx]
            outer = lambda x: (bfly(x[0], x[2], tw(s, k)), bfly(x[1], x[3], tw(s, k + h2)))
            inner = lambda x: (bfly(x[0], x[1], tw(s + 1, k)), bfly(x[2], x[3], tw(s + 1, k)))
            if inverse:
                (x0, x1), (x2, x3) = inner(x)
                (x0, x2), (x1, x3) = outer([x0, x1, x2, x3])
            else:
                (x0, x2), (x1, x3) = outer(x)
                (x0, x1), (x2, x3) = inner([x0, x1, x2, x3])
            for i, v in zip(idx, (x0, x1, x2, x3)):
                w[i, re], w[i, im] = v
            return c

        lax.fori_loop(0, nb // 2, body, 0)

    lone = [0] if n_mid % 2 else []
    pairs = list(range(len(lone), n_mid, 2))
    if inverse:
        for s in reversed(pairs):
            double(s)
        for s in lone:
            single(s)
    else:
        for s in lone:
            single(s)
        for s in pairs:
            double(s)


def _hy_conv_kernel(*refs, nb, n_mid, n_post):
    tw_edge, tw_mid, a_ref, bias_ref, kh_ref, g_ref, gi_ref = refs[:7]
    post_refs = refs[7:7 + n_post]
    o_ref, w = refs[7 + n_post], refs[8 + n_post]
    r = FFT_R
    m = 2 * nb
    half0 = nb // 2
    re, im = slice(0, r), slice(r, 2 * r)

    def rows(k):
        return pl.ds(pl.multiple_of(k * r, r), r)

    _block_dft_forward(lambda k: (a_ref[0, rows(k), :].astype(F32), a_ref[1, rows(k), :].astype(F32)),
                       w, tw_edge, tw_mid, nb, n_mid)

    def spectral(t, c):
        slots = [FFT_SLOTS * t + u for u in range(FFT_SLOTS)]
        xs = [jnp.dot(g_ref[s], w[s].astype(BF16), preferred_element_type=F32) for s in slots]
        outs = []
        for s, x in zip(slots, xs):
            yr, yi = _cmul(x[re], x[im], kh_ref[0, s, re, :], kh_ref[0, s, im, :])
            y = jnp.concatenate([yr, yi], axis=0).astype(BF16)
            outs.append(jnp.dot(gi_ref[s], y, preferred_element_type=F32))
        for s, o in zip(slots, outs):
            w[s] = o
        return c

    lax.fori_loop(0, m // FFT_SLOTS, spectral, 0)

    _mid_stages(w, tw_mid, nb, n_mid, inverse=True)

    bias = bias_ref[0]

    def emit(k, yr, yi):
        for bsel, y in ((0, yr), (1, yi)):
            a = a_ref[bsel, rows(k), :].astype(F32)
            val = y + bias * a
            for p_ref in post_refs:
                val = val * p_ref[bsel, rows(k), :].astype(F32)
            o_ref[bsel, rows(k), :] = val.astype(o_ref.dtype)

    def last(k, c):
        wr, wi = tw_edge[0, k], -tw_edge[1, k]
        ar, ai = w[k, re], w[k, im]
        br, bi = _cmul(w[k + half0, re], w[k + half0, im], wr, wi)
        cr, ci = w[nb + k, re], w[nb + k, im]
        dr, di = _cmul(w[nb + k + half0, re], w[nb + k + half0, im], wr, wi)
        o0r, o0i = _cmul(cr + dr, ci + di, tw_edge[2, k], -tw_edge[3, k])
        o1r, o1i = _cmul(cr - dr, ci - di, tw_edge[4, k], -tw_edge[5, k])
        emit(k, ar + br + o0r, ai + bi + o0i)
        emit(k + half0, ar - br + o1r, ai - bi + o1i)
        return c

    lax.fori_loop(0, half0, last, 0)


def _hy_conv(a, khat, bias, posts, n, out_dtype):
    b, _, width = a[0].shape
    lanes = LANES
    halves = width // lanes
    plan = _fft_plan(n)
    nb, m = plan['nb'], plan['m']
    r = FFT_R

    def window(row0):
        return pl.BlockSpec((pl.Element(2), pl.Element(n), pl.Element(lanes)),
                            lambda hf, pr: (2 * pr, row0, lanes * hf))

    sig = window(a[1])
    smem = pl.BlockSpec(memory_space=pltpu.SMEM)
    once = pl.Buffered(1)
    in_specs = [smem, smem, sig,
                pl.BlockSpec((1, 1, lanes), lambda hf, pr: (hf, 0, 0)),
                pl.BlockSpec((1, m, 2 * r, lanes), lambda hf, pr: (hf, 0, 0, 0), pipeline_mode=once),
                pl.BlockSpec((m, 2 * r, 2 * r), lambda hf, pr: (0, 0, 0), pipeline_mode=once),
                pl.BlockSpec((m, 2 * r, 2 * r), lambda hf, pr: (0, 0, 0), pipeline_mode=once)]
    in_specs += [window(row0) for _, row0 in posts]
    g_fwd = jnp.asarray(plan['g'], BF16)
    g_inv = jnp.asarray(np.swapaxes(plan['g'], 1, 2), BF16)
    return pl.pallas_call(
        functools.partial(_hy_conv_kernel, nb=nb, n_mid=plan['n_mid'], n_post=len(posts)),
        grid=(halves, b // 2),
        in_specs=in_specs,
        out_specs=pl.BlockSpec((2, n, lanes), lambda hf, pr: (pr, 0, hf)),
        out_shape=jax.ShapeDtypeStruct((b, n, width), out_dtype),
        scratch_shapes=[pltpu.VMEM((m, 2 * r, lanes), F32)],
        compiler_params=pltpu.CompilerParams(dimension_semantics=("arbitrary", "arbitrary"),
                                             vmem_limit_bytes=HY_VMEM_LIMIT),
        name="hyena_conv",
    )(jnp.asarray(plan['tw_edge']), jnp.asarray(plan['tw_mid']), a[0], bias, khat, g_fwd, g_inv,
      *[p for p, _ in posts])


_HP = lax.Precision.HIGHEST
HY_TAPS_W = HY_ORDER * 2 * HY_W
HY_FEAT = 128


def _hy_taps_kernel(z_ref, dec_ref, w1_ref, b1_ref, f1_ref, w2_ref, b2_ref, f2_ref, w3_ref, o_ref,
                    h_scr, ss_scr):
    phase, i = pl.program_id(0), pl.program_id(1)
    tm = z_ref.shape[0]
    rows = pl.ds(pl.multiple_of(i * tm, tm), tm)

    @pl.when(phase == 0)
    def _():
        @pl.when(i == 0)
        def _():
            ss_scr[...] = jnp.zeros_like(ss_scr)

        h = jnp.dot(z_ref[...], w1_ref[...], precision=_HP, preferred_element_type=F32)
        h = jnp.sin(f1_ref[...] * (h + b1_ref[...]))
        h = jnp.dot(h, w2_ref[...], precision=_HP, preferred_element_type=F32)
        h = jnp.sin(f2_ref[...] * (h + b2_ref[...]))
        dec = dec_ref[...]
        for c0 in range(0, HY_TAPS_W, HY_W):
            t = jnp.dot(h, w3_ref[:, c0:c0 + HY_W], precision=_HP, preferred_element_type=F32) * dec
            h_scr[rows, c0:c0 + HY_W] = t
            ss_scr[:, c0:c0 + HY_W] += jnp.sum(t * t, axis=0, keepdims=True)

    @pl.when(phase == 1)
    def _():
        first_row = lax.broadcasted_iota(jnp.int32, (tm, HY_W), 0) == jnp.where(i == 0, 0, -1)
        for o in range(HY_ORDER):
            c0 = o * 2 * HY_W
            tot = ss_scr[:, c0:c0 + HY_W] + ss_scr[:, c0 + HY_W:c0 + 2 * HY_W]
            inv = lax.rsqrt(tot + EPS)
            o_ref[:, c0:c0 + HY_W] = h_scr[rows, c0:c0 + HY_W] * inv
            o_ref[:, c0 + HY_W:c0 + 2 * HY_W] = jnp.where(first_row, 0.0, h_scr[rows, c0 + HY_W:c0 + 2 * HY_W] * inv)


@functools.lru_cache(maxsize=None)
def _filter_features(n):
    t = np.linspace(0.0, 1.0, n, dtype=np.float32)[:, None]
    omega = (2.0 * np.pi * np.arange(n, dtype=np.float32)[:, None] / n).astype(np.float32)
    bands = np.linspace(1e-4, HY_BANDS - 1, HY_BANDS, dtype=np.float32)[None, :]
    z = np.zeros((n, HY_FEAT), np.float32)
    z[:, 0:1] = t
    z[:, 1:1 + HY_BANDS] = np.cos(bands * omega)
    z[:, 1 + HY_BANDS:HY_EMB] = -np.sin(bands * omega)
    max_decay = math.log(HY_DECAY_TARGET) / HY_FAST_DECAY
    min_decay = math.log(HY_DECAY_TARGET) / HY_SLOW_DECAY
    deltas = np.linspace(min_decay, max_decay, HY_W, dtype=np.float32)
    dec = np.exp(-t * np.abs(deltas)).astype(np.float32)
    return z, dec


def _hy_taps(n, w1, b1, fr1, w2, b2, fr2, w3):
    z, dec = _filter_features(n)
    tm = min(n, 512)
    pad = HY_FEAT - HY_HIDDEN
    w1p = jnp.pad(w1.astype(F32), ((0, HY_FEAT - HY_EMB), (0, pad)))
    w2p = jnp.pad(w2.astype(F32), ((0, pad), (0, pad)))
    w3p = jnp.pad(w3.astype(F32), ((0, pad), (0, 0)))
    vec = lambda v: jnp.pad(v.astype(F32), (0, pad)).reshape(1, HY_FEAT)
    const = lambda ph, i: (0, 0)
    return pl.pallas_call(
        _hy_taps_kernel,
        grid=(2, n // tm),
        in_specs=[pl.BlockSpec((tm, HY_FEAT), lambda ph, i: (i * (1 - ph), 0)),
                  pl.BlockSpec((tm, HY_W), lambda ph, i: (i * (1 - ph), 0)),
                  pl.BlockSpec((HY_FEAT, HY_FEAT), const), pl.BlockSpec((1, HY_FEAT), const),
                  pl.BlockSpec((1, HY_FEAT), const),
                  pl.BlockSpec((HY_FEAT, HY_FEAT), const), pl.BlockSpec((1, HY_FEAT), const),
                  pl.BlockSpec((1, HY_FEAT), const),
                  pl.BlockSpec((HY_FEAT, HY_TAPS_W), const)],
        out_specs=pl.BlockSpec((tm, HY_TAPS_W), lambda ph, i: (i * ph, 0)),
        out_shape=jax.ShapeDtypeStruct((n, HY_TAPS_W), F32),
        scratch_shapes=[pltpu.VMEM((n, HY_TAPS_W), F32), pltpu.VMEM((1, HY_TAPS_W), F32)],
        compiler_params=_cparams("arbitrary", "arbitrary"),
        name="hyena_taps",
    )(jnp.asarray(z), jnp.asarray(dec), w1p, vec(b1), vec(fr1), w2p, vec(b2), vec(fr2), w3p)


def _hy_spec_kernel(tw_edge, tw_mid, f_ref, b_ref, ghi_ref, glo_ref, o_ref, w, *, nb, n_mid, inv_n):
    r = FFT_R
    m = 2 * nb
    re, im = slice(0, r), slice(r, 2 * r)
    zeros = jnp.zeros((r, LANES), F32)

    def rows(k):
        return pl.ds(pl.multiple_of(k * r, r), r)

    def in_block_dft(s):
        x = w[s]
        xh = x.astype(BF16)
        xl = (x - xh.astype(F32)).astype(BF16)
        gh = ghi_ref[s]
        y = jnp.dot(gh, xh, preferred_element_type=F32)
        y += jnp.dot(gh, xl, preferred_element_type=F32)
        y += jnp.dot(glo_ref[s], xh, preferred_element_type=F32)
        return y * inv_n

    _block_dft_forward(lambda k: (f_ref[rows(k), :], zeros), w, tw_edge, tw_mid, nb, n_mid)

    def fwd_part(s, c):
        o_ref[0, 0, s] = in_block_dft(s)
        return c

    lax.fori_loop(0, m, fwd_part, 0)
    _block_dft_forward(lambda k: (b_ref[rows(k), :], zeros), w, tw_edge, tw_mid, nb, n_mid)

    def bwd_part(s, c):
        y = in_block_dft(s)
        o_ref[0, 0, s, re] += y[re]
        o_ref[0, 0, s, im] -= y[im]
        return c

    lax.fori_loop(0, m, bwd_part, 0)


def _hy_filter_spectrum(taps):
    n = taps.shape[0]
    plan = _fft_plan(n)
    nb, m = plan['nb'], plan['m']
    r = FFT_R
    halves = HY_W // LANES
    g = plan['g']
    g_hi = g.astype(jnp.bfloat16)
    g_lo = (g - np.asarray(g_hi, np.float32)).astype(jnp.bfloat16)
    smem = pl.BlockSpec(memory_space=pltpu.SMEM)
    once = pl.Buffered(1)
    table = lambda o, hf: (0, 0, 0)
    return pl.pallas_call(
        functools.partial(_hy_spec_kernel, nb=nb, n_mid=plan['n_mid'], inv_n=1.0 / (2 * n)),
        grid=(HY_ORDER, halves),
        in_specs=[smem, smem,
                  pl.BlockSpec((n, LANES), lambda o, hf: (0, o * 2 * halves + hf)),
                  pl.BlockSpec((n, LANES), lambda o, hf: (0, o * 2 * halves + halves + hf)),
                  pl.BlockSpec((m, 2 * r, 2 * r), table, pipeline_mode=once),
                  pl.BlockSpec((m, 2 * r, 2 * r), table, pipeline_mode=once)],
        out_specs=pl.BlockSpec((1, 1, m, 2 * r, LANES), lambda o, hf: (o, hf, 0, 0, 0)),
        out_shape=jax.ShapeDtypeStruct((HY_ORDER, halves, m, 2 * r, LANES), F32),
        scratch_shapes=[pltpu.VMEM((m, 2 * r, LANES), F32)],
        compiler_params=pltpu.CompilerParams(dimension_semantics=("arbitrary", "arbitrary"),
                                             vmem_limit_bytes=HY_VMEM_LIMIT),
        name="hyena_spectrum",
    )(jnp.asarray(plan['tw_edge']), jnp.asarray(plan['tw_mid']), taps, taps,
      jnp.asarray(g_hi), jnp.asarray(g_lo))


def _pack_w_in(w_in):
    d = w_in.shape[0]
    z = lambda n: jnp.zeros((d, n), w_in.dtype)
    o = 0
    q_lat = w_in[:, o:o + MLA_Q_RANK]; o += MLA_Q_RANK
    kv_lat = w_in[:, o:o + MLA_KV_RANK]; o += MLA_KV_RANK
    k_rope = w_in[:, o:o + MLA_ROPE]; o += MLA_ROPE
    gate = w_in[:, o:o + GROUP_W]; o += GROUP_W
    mla = jnp.concatenate([q_lat, z(256 - MLA_Q_RANK), kv_lat, z(MLA_NOPE), k_rope,
                           z(MLA_DK - MLA_NOPE - MLA_ROPE), gate], axis=1)
    return jnp.concatenate([mla, w_in[:, o:]], axis=1).astype(BF16)


def _pack_mla_up(w_uq, w_ukv):
    dq = MLA_NOPE + MLA_ROPE
    wq = w_uq.reshape(MLA_Q_RANK, MLA_HEADS, dq).transpose(1, 0, 2)
    wq = jnp.pad(wq, ((0, 0), (0, 256 - MLA_Q_RANK), (0, MLA_DK - dq))).astype(BF16)
    wkv = w_ukv.reshape(MLA_KV_RANK, MLA_HEADS, MLA_NOPE + MLA_V).transpose(1, 0, 2)
    wk = jnp.pad(wkv[:, :, :MLA_NOPE], ((0, 0), (0, 0), (0, MLA_DK - MLA_NOPE))).astype(BF16)
    wvt = jnp.swapaxes(wkv[:, :, MLA_NOPE:], 1, 2).astype(BF16)
    return wq, wk, wvt


def kernel(x, c, ctx, c_ctx, w_mod, b_mod, g_pre, g_post, w_in, w_out, mla_g_cq, mla_w_uq, mla_g_ckv, mla_w_ukv, gqa_g_q, gqa_g_k, ssm_lambda_re, ssm_lambda_im, ssm_log_step, ssm_b_re, ssm_b_im, ssm_c_re, ssm_c_im, ssm_d, ssm_glu_w, ssm_glu_b, hy_conv_w, hy_conv_b, hy_f_w1, hy_f_b1, hy_f_freq1, hy_f_w2, hy_f_b2, hy_f_freq2, hy_f_w3, hy_bias):
    b, n_lat, d = x.shape
    n_ctx = ctx.shape[1]
    depth = w_in.shape[0]
    lt = n_ctx + n_lat
    nct = n_ctx // ROW_TILE
    assert n_ctx % ROW_TILE == 0 and n_lat % ROW_TILE == 0 and b % 2 == 0 and b % 8 == 0

    n_cond = -(-(b + 1) // 8) * 8
    cond = jnp.zeros((n_cond, d), F32).at[:b].set(c).at[b].set(c_ctx)
    mod = _modulation(cond, w_mod, b_mod)[:, :b + 1]
    shift, scale, gate = mod[..., :d], mod[..., d:2 * d], mod[..., 2 * d:]

    mla_cos, mla_sin, gqa_cos, gqa_sin = _rope_tables(n_ctx, n_lat)
    nq = GQA_HEADS * GQA_DIM
    ones_bd = jnp.asarray(np.kron(np.eye(GQA_HEADS), np.ones((GQA_DIM, GQA_DIM))), BF16)

    xa = jnp.concatenate([ctx, x], axis=1)
    for l in range(depth):
        sc = (g_pre[l][None, :] * (1.0 + scale[l]))[:, None, :]
        sh = shift[l][:, None, :]
        wq, wk, wvt = _pack_mla_up(mla_w_uq[l], mla_w_ukv[l])
        g_cq = jnp.pad(mla_g_cq[l], (0, 256 - MLA_Q_RANK)).reshape(1, 256)
        mla_args = (mla_cos, mla_sin, g_cq, mla_g_ckv[l].reshape(1, -1), wq, wk, wvt)
        gqa_args = (gqa_cos, gqa_sin, jnp.tile(gqa_g_q[l], GQA_HEADS).reshape(1, nq),
                    jnp.tile(gqa_g_k[l], GQA_KV_HEADS).reshape(1, -1), ones_bd)
        hy_args = (hy_conv_w[l], hy_conv_b[l].reshape(1, -1))
        gate_m, gate_g, p_ssm, (hv, hx1, hx2, hsg), mla_qkv, gqa_qkv = _inproj(
            xa, sc, sh, _pack_w_in(w_in[l]), nct, mla_args, gqa_args, hy_args)
        a_out = _attention(*mla_qkv, gate_m, 0, n_ctx)
        g_out = _attention(*gqa_qkv, gate_g, 0, n_ctx)

        u_tm = jnp.transpose(p_ssm, (1, 0, 2)).reshape(lt * b, SSM_PACK)
        mats = [_ssm_matrices(ssm_lambda_re[l, di], ssm_lambda_im[l, di], ssm_log_step[l, di],
                              ssm_b_re[l, di], ssm_b_im[l, di], ssm_c_re[l, di], ssm_c_im[l, di])
                for di in range(2)]
        n_ctx_chunks = n_ctx // SSM_T
        y_f = _ssm_direction(u_tm, mats[0], b, n_ctx_chunks, False)
        fin = (y_f, ssm_d[l].reshape(1, -1), ssm_glu_w[l].astype(BF16), ssm_glu_b[l].reshape(1, -1))
        s_tm = _ssm_direction(u_tm, mats[1], b, n_ctx_chunks, True, fin)
        s_out = jnp.transpose(s_tm.reshape(lt, b, GROUP_W), (1, 0, 2))

        filt = (hy_f_w1[l], hy_f_b1[l], hy_f_freq1[l], hy_f_w2[l], hy_f_b2[l], hy_f_freq2[l], hy_f_w3[l])
        bias = hy_bias[l].astype(F32).reshape(HY_ORDER, HY_W // LANES, 1, LANES)
        parts = []
        for row0, n in ((0, n_ctx), (n_ctx, n_lat)):
            khat = _hy_filter_spectrum(_hy_taps(n, *filt))
            z1 = _hy_conv((hv, row0), khat[0], bias[0], ((hx1, row0),), n, BF16)
            parts.append(_hy_conv((z1, 0), khat[1], bias[1], ((hx2, row0), (hsg, row0)), n, BF16))
        y_out = jnp.concatenate(parts, axis=1)

        xa = _outproj(a_out, g_out, s_out, y_out, xa, gate[l][:, None, :], g_post[l].reshape(1, d),
                      w_out[l].astype(BF16), nct, tile0=nct if l == depth - 1 else 0)
    return xa
```

```python
import functools
import math

import numpy as np
import jax
import jax.numpy as jnp
from jax import lax
from jax.experimental import pallas as pl
from jax.experimental.pallas import tpu as pltpu

F32 = jnp.float32
BF16 = jnp.bfloat16

GRID_W = 64
ROPE_BASE = 10000.0
EPS = 1e-6
GROUP_W = 256
MLA_HEADS, MLA_NOPE, MLA_ROPE, MLA_V = 4, 64, 32, 64
MLA_Q_RANK, MLA_KV_RANK = 192, 128
GQA_HEADS, GQA_KV_HEADS, GQA_DIM = 4, 2, 64
SSM_GROUPS, SSM_GROUP, SSM_STATE = 16, 16, 64
HY_W, HY_ORDER, HY_EMB, HY_HIDDEN = 256, 2, 33, 64
HY_BANDS = (HY_EMB - 1) // 2
HY_FAST_DECAY, HY_SLOW_DECAY, HY_DECAY_TARGET = 0.3, 1.5, 1e-2

MLA_PACK = 768
GQA_PACK = 768
SSM_PACK = 512
HY_PACK = 1024
IN_PACK = MLA_PACK + GQA_PACK + SSM_PACK + HY_PACK
MLA_DK = 128

LANES = 128
ROW_TILE = 256
ATT_TQ = 256
ATT_TK = 256
ATT_UNROLLS = (16, 8, 4, 2)
ONES_ROWS = 16
SSM_T = 128
SSM_LANE_SPLIT = 512
FFT_R = 128
FFT_SLOTS = 16
VMEM_LIMIT = 52 * 1024 * 1024
HY_VMEM_LIMIT = 58 * 1024 * 1024


def _cparams(*sem):
    return pltpu.CompilerParams(dimension_semantics=sem, vmem_limit_bytes=VMEM_LIMIT)


def _silu(x):
    return x * jax.nn.sigmoid(x)


def _mod_kernel(c_ref, w_ref, b_ref, o_ref):
    c = c_ref[...]
    s = _silu(c).astype(BF16)
    o_ref[0] = jnp.dot(s, w_ref[0].astype(BF16), preferred_element_type=F32) + b_ref[0]


def _modulation(cond, w_mod, b_mod):
    depth, d, n3 = w_mod.shape
    r = cond.shape[0]
    tn = 512
    return pl.pallas_call(
        _mod_kernel,
        grid=(depth, n3 // tn),
        in_specs=[pl.BlockSpec((r, d), lambda l, j: (0, 0)),
                  pl.BlockSpec((1, d, tn), lambda l, j: (l, 0, j)),
                  pl.BlockSpec((1, 1, tn), lambda l, j: (l, 0, j))],
        out_specs=pl.BlockSpec((1, r, tn), lambda l, j: (l, 0, j)),
        out_shape=jax.ShapeDtypeStruct((depth, r, n3), F32),
        compiler_params=_cparams("parallel", "parallel"),
        name="modulation",
    )(cond, w_mod, b_mod.reshape(depth, 1, n3))


def _outproj_kernel(a_ref, g_ref, s_ref, y_ref, x_ref, gt_ref, gp_ref, w_ref, o_ref):
    half = a_ref.shape[1] // 2
    for hs in (slice(0, half), slice(half, 2 * half)):
        acc = jnp.dot(a_ref[0, hs], w_ref[0:GROUP_W], preferred_element_type=F32)
        acc += jnp.dot(g_ref[0, hs], w_ref[GROUP_W:2 * GROUP_W], preferred_element_type=F32)
        acc += jnp.dot(s_ref[0, hs], w_ref[2 * GROUP_W:3 * GROUP_W], preferred_element_type=F32)
        acc += jnp.dot(y_ref[0, hs], w_ref[3 * GROUP_W:4 * GROUP_W], preferred_element_type=F32)
        ms = jnp.mean(acc * acc, axis=-1, keepdims=True)
        o_ref[0, hs] = x_ref[0, hs] + gt_ref[0] * (acc * lax.rsqrt(ms + EPS) * gp_ref[...])


def _outproj(a, g, s, y, xa, gate, g_post, w, nct, tile0=0):
    b, lt, d = xa.shape
    tm = ROW_TILE

    def mod_idx(bi, i):
        return (jnp.where(i + tile0 < nct, b, bi), 0, 0)

    row = lambda bi, i: (bi, i + tile0, 0)
    return pl.pallas_call(
        _outproj_kernel,
        grid=(b, lt // tm - tile0),
        in_specs=[pl.BlockSpec((1, tm, GROUP_W), row)] * 4 + [
            pl.BlockSpec((1, tm, d), row),
            pl.BlockSpec((1, 1, d), mod_idx),
            pl.BlockSpec((1, d), lambda bi, i: (0, 0)),
            pl.BlockSpec((4 * GROUP_W, d), lambda bi, i: (0, 0))],
        out_specs=pl.BlockSpec((1, tm, d), lambda bi, i: (bi, i, 0)),
        out_shape=jax.ShapeDtypeStruct((b, lt - tile0 * tm, d), F32),
        compiler_params=_cparams("parallel", "parallel"),
        name="outproj",
    )(a, g, s, y, xa, gate, g_post, w)


def _rope(x, cos, sin, shift):
    w = x.shape[-1]
    lane = lax.broadcasted_iota(jnp.int32, x.shape, 1)
    first = (lane & shift) == 0
    swapped = jnp.where(first, -pltpu.roll(x, w - shift, 1), pltpu.roll(x, shift, 1))
    return x * cos + swapped * sin


def _rope_tables(n_ctx, n_lat):
    t = np.arange(n_lat)
    row = (t // GRID_W).astype(np.float64)
    col = (t % GRID_W).astype(np.float64)

    def block(pos, h):
        inv = ROPE_BASE ** (-np.arange(0, h, 2, dtype=np.float64) / h)
        ang = (pos[:, None].astype(np.float32) * inv[None, :].astype(np.float32)).astype(np.float32)
        c, s = np.cos(ang), np.sin(ang)
        return np.concatenate([c, c], -1), np.concatenate([s, s], -1)

    def full(h, lead, width):
        cr, sr = block(row, h)
        cc, sc = block(col, h)
        cos = np.ones((n_ctx + n_lat, width), np.float32)
        sin = np.zeros((n_ctx + n_lat, width), np.float32)
        cos[n_ctx:, lead:lead + 2 * h] = np.concatenate([cr, cc], -1)
        sin[n_ctx:, lead:lead + 2 * h] = np.concatenate([sr, sc], -1)
        return cos, sin

    mc, ms = full(MLA_ROPE // 2, MLA_NOPE, MLA_DK)
    gc, gs = full(GQA_DIM // 2, 0, GQA_DIM)
    reps = GQA_HEADS
    return (jnp.asarray(mc), jnp.asarray(ms),
            jnp.asarray(np.tile(gc, (1, reps))), jnp.asarray(np.tile(gs, (1, reps))))


_NT = (((1,), (1,)), ((), ()))
_TN = (((0,), (0,)), ((), ()))
LOG2E = math.log2(math.e)


def _mla_prep_body(p, cos_ref, sin_ref, gq_ref, gkv_ref, wq_ref, wk_ref, wvt_ref, eye_ref,
                   qt_ref, k_ref, vt_ref):
    cos, sin = cos_ref[...], sin_ref[...]
    ql = p[:, 0:256]
    rq = lax.rsqrt(jnp.sum(ql * ql, axis=-1, keepdims=True) * (1.0 / MLA_Q_RANK) + EPS)
    qn = (ql * rq * gq_ref[...]).astype(BF16)
    kvl = p[:, 256:384]
    rk = lax.rsqrt(jnp.mean(kvl * kvl, axis=-1, keepdims=True) + EPS)
    kvn = (kvl * rk * gkv_ref[...]).astype(BF16)
    k_rope = _rope(p[:, 384:512], cos, sin, MLA_ROPE // 4)
    scale = (MLA_NOPE + MLA_ROPE) ** -0.5 * LOG2E
    heads = range(MLA_HEADS)
    qs = [jnp.dot(qn, wq_ref[h], preferred_element_type=F32) for h in heads]
    ks = [jnp.dot(kvn, wk_ref[h], preferred_element_type=F32) for h in heads]
    vts = [lax.dot_general(wvt_ref[h], kvn, _NT, preferred_element_type=F32) for h in heads]
    for h in heads:
        k_ref[0, h] = (ks[h] + k_rope).astype(BF16)
        vt_ref[0, h, 0, 0:MLA_V] = vts[h].astype(BF16)
        vt_ref[0, h, 0, MLA_V:MLA_V + ONES_ROWS] = jnp.ones((ONES_ROWS, kvn.shape[0]), BF16)
    qr = [(_rope(q, cos, sin, MLA_ROPE // 4) * scale).astype(BF16) for q in qs]
    for h in heads:
        qt_ref[0, h] = lax.dot_general(eye_ref[...], qr[h], _NT, preferred_element_type=F32).astype(BF16)


def _head_mean_sq(x, ones_bd):
    sq = x * x
    hi = sq.astype(BF16)
    lo = (sq - hi.astype(F32)).astype(BF16)
    s = jnp.dot(hi, ones_bd, preferred_element_type=F32) + jnp.dot(lo, ones_bd, preferred_element_type=F32)
    return s * (1.0 / GQA_DIM)


def _gqa_prep_body(p, cos_ref, sin_ref, gq_ref, gk_ref, ones_ref, eye_ref, qt_ref, k_ref, vt_ref):
    cos, sin = cos_ref[...], sin_ref[...]
    nq = GQA_HEADS * GQA_DIM
    nk = GQA_KV_HEADS * GQA_DIM
    q = p[:, 0:nq]
    k = p[:, nq:nq + nk]
    v = p[:, nq + nk:nq + 2 * nk].astype(BF16)
    q_ms = _head_mean_sq(q, ones_ref[...])
    k_ms = _head_mean_sq(k, ones_ref[0:nk, 0:nk])
    for h in range(GQA_KV_HEADS):
        sel = eye_ref[h * GQA_DIM:(h + 1) * GQA_DIM, 0:nk]
        vt_ref[0, h, 0, 0:GQA_DIM] = lax.dot_general(sel, v, _NT, preferred_element_type=F32).astype(BF16)
        vt_ref[0, h, 0, GQA_DIM:GQA_DIM + ONES_ROWS] = jnp.ones((ONES_ROWS, v.shape[0]), BF16)
    qn = q * lax.rsqrt(q_ms + EPS) * gq_ref[...]
    qr = (_rope(qn, cos, sin, GQA_DIM // 4) * (GQA_DIM ** -0.5 * LOG2E)).astype(BF16)
    kn = k * lax.rsqrt(k_ms + EPS) * gk_ref[...]
    kr = _rope(kn, cos[:, 0:nk], sin[:, 0:nk], GQA_DIM // 4)
    for h in range(GQA_HEADS):
        sel = eye_ref[h * GQA_DIM:(h + 1) * GQA_DIM, :]
        qt_ref[0, h] = lax.dot_general(sel, qr, _NT, preferred_element_type=F32).astype(BF16)
    for h in range(GQA_KV_HEADS):
        k_ref[0, h] = kr[:, h * GQA_DIM:(h + 1) * GQA_DIM].astype(BF16)


def _inproj_kernel(x_ref, xp_ref, xn_ref, sc_ref, sh_ref, w_ref,
                   mcos_ref, msin_ref, gcq_ref, gckv_ref, wq_ref, wk_ref, wvt_ref, eye_m_ref,
                   gcos_ref, gsin_ref, gq_ref, gk_ref, ones_ref, eye_g_ref, cw_ref, cb_ref,
                   gate_m_ref, gate_g_ref, ssm_ref, hv_ref, hx1_ref, hx2_ref, hsg_ref,
                   mqt_ref, mk_ref, mvt_ref, gqt_ref, gk_out_ref, gvt_ref, *, nct, n_tiles):
    i = pl.program_id(1)

    tm = x_ref.shape[1]

    def normed(x):
        ms = jnp.mean(x * x, axis=-1, keepdims=True)
        return (x * lax.rsqrt(ms + EPS) * sc_ref[0] + sh_ref[0]).astype(BF16)

    halo = normed(jnp.concatenate([xp_ref[0], xn_ref[0]], axis=0))
    hb = normed(x_ref[0])

    def proj(c0, width, lhs=hb):
        return jnp.dot(lhs, w_ref[:, c0:c0 + width], preferred_element_type=F32)

    qkv = 2 * GROUP_W
    hy0 = MLA_PACK + GQA_PACK + SSM_PACK
    p_mla = proj(0, qkv)
    p_gqa = proj(MLA_PACK, qkv)
    p_hy_ext = proj(hy0, HY_PACK, jnp.concatenate([hb, halo], axis=0))
    p_hy = p_hy_ext[0:tm]
    halo_prev = p_hy_ext[tm + 7:tm + 8, 0:3 * HY_W]
    halo_next = p_hy_ext[tm + 8:tm + 9, 0:3 * HY_W]
    gate_m_ref[0] = proj(qkv, GROUP_W).astype(BF16)
    gate_g_ref[0] = proj(MLA_PACK + qkv, GROUP_W).astype(BF16)
    ssm_ref[0] = proj(MLA_PACK + GQA_PACK, SSM_PACK).astype(BF16)
    seq_start = (i == 0) | (i == nct)
    seq_end = (i == nct - 1) | (i == n_tiles - 1)
    _hy_pre_body(p_hy, jnp.where(seq_start, 0.0, halo_prev), jnp.where(seq_end, 0.0, halo_next),
                 cw_ref, cb_ref, hv_ref, hx1_ref, hx2_ref, hsg_ref)
    _mla_prep_body(p_mla, mcos_ref, msin_ref, gcq_ref, gckv_ref, wq_ref, wk_ref, wvt_ref, eye_m_ref,
                   mqt_ref, mk_ref, mvt_ref)
    _gqa_prep_body(p_gqa, gcos_ref, gsin_ref, gq_ref, gk_ref, ones_ref, eye_g_ref,
                   gqt_ref, gk_out_ref, gvt_ref)


def _inproj(xa, scale, shift, w, nct, mla_args, gqa_args, hy_args):
    b, lt, d = xa.shape
    tm = ROW_TILE
    nq = GQA_HEADS * GQA_DIM
    nk = GQA_KV_HEADS * GQA_DIM
    n_tiles = lt // tm
    sub = tm // 8

    def mod_idx(bi, i):
        return (jnp.where(i < nct, b, bi), 0, 0)

    const2 = lambda bi, i: (0, 0)
    const3 = lambda bi, i: (0, 0, 0)
    row = lambda bi, i: (bi, i, 0)
    pos = lambda bi, i: (i, 0)
    qt_idx = lambda bi, i: (bi, 0, 0, i)
    k_idx = lambda bi, i: (bi, 0, i, 0)
    vt_idx = lambda bi, i: (bi, 0, i, 0, 0)
    in_specs = [pl.BlockSpec((1, tm, d), row),
                pl.BlockSpec((1, 8, d), lambda bi, i: (bi, jnp.maximum(i * sub - 1, 0), 0)),
                pl.BlockSpec((1, 8, d), lambda bi, i: (bi, jnp.minimum((i + 1) * sub, lt // 8 - 1), 0)),
                pl.BlockSpec((1, 1, d), mod_idx), pl.BlockSpec((1, 1, d), mod_idx),
                pl.BlockSpec((d, IN_PACK), const2),
                pl.BlockSpec((tm, MLA_DK), pos), pl.BlockSpec((tm, MLA_DK), pos),
                pl.BlockSpec((1, 256), const2), pl.BlockSpec((1, MLA_KV_RANK), const2),
                pl.BlockSpec((MLA_HEADS, 256, MLA_DK), const3),
                pl.BlockSpec((MLA_HEADS, MLA_KV_RANK, MLA_DK), const3),
                pl.BlockSpec((MLA_HEADS, MLA_V, MLA_KV_RANK), const3),
                pl.BlockSpec((MLA_DK, MLA_DK), const2),
                pl.BlockSpec((tm, nq), pos), pl.BlockSpec((tm, nq), pos),
                pl.BlockSpec((1, nq), const2), pl.BlockSpec((1, nk), const2),
                pl.BlockSpec((nq, nq), const2), pl.BlockSpec((nq, nq), const2),
                pl.BlockSpec((3, 3 * HY_W), const2), pl.BlockSpec((1, 3 * HY_W), const2)]
    out_specs = [pl.BlockSpec((1, tm, GROUP_W), row), pl.BlockSpec((1, tm, GROUP_W), row),
                 pl.BlockSpec((1, tm, SSM_PACK), row)] + [pl.BlockSpec((1, tm, HY_W), row)] * 4 + [
                 pl.BlockSpec((1, MLA_HEADS, MLA_DK, tm), qt_idx),
                 pl.BlockSpec((1, MLA_HEADS, tm, MLA_DK), k_idx),
                 pl.BlockSpec((1, MLA_HEADS, 1, MLA_V + ONES_ROWS, tm), vt_idx),
                 pl.BlockSpec((1, GQA_HEADS, GQA_DIM, tm), qt_idx),
                 pl.BlockSpec((1, GQA_KV_HEADS, tm, GQA_DIM), k_idx),
                 pl.BlockSpec((1, GQA_KV_HEADS, 1, GQA_DIM + ONES_ROWS, tm), vt_idx)]
    out_shape = [jax.ShapeDtypeStruct((b, lt, GROUP_W), BF16), jax.ShapeDtypeStruct((b, lt, GROUP_W), BF16),
                 jax.ShapeDtypeStruct((b, lt, SSM_PACK), BF16)] + [
                 jax.ShapeDtypeStruct((b, lt, HY_W), BF16)] * 4 + [
                 jax.ShapeDtypeStruct((b, MLA_HEADS, MLA_DK, lt), BF16),
                 jax.ShapeDtypeStruct((b, MLA_HEADS, lt, MLA_DK), BF16),
                 jax.ShapeDtypeStruct((b, MLA_HEADS, n_tiles, MLA_V + ONES_ROWS, tm), BF16),
                 jax.ShapeDtypeStruct((b, GQA_HEADS, GQA_DIM, lt), BF16),
                 jax.ShapeDtypeStruct((b, GQA_KV_HEADS, lt, GQA_DIM), BF16),
                 jax.ShapeDtypeStruct((b, GQA_KV_HEADS, n_tiles, GQA_DIM + ONES_ROWS, tm), BF16)]
    outs = pl.pallas_call(
        functools.partial(_inproj_kernel, nct=nct, n_tiles=n_tiles),
        grid=(b, n_tiles),
        in_specs=in_specs,
        out_specs=out_specs,
        out_shape=out_shape,
        compiler_params=_cparams("parallel", "parallel"),
        name="inproj",
    )(xa, xa, xa, scale, shift, w, *mla_args, jnp.eye(MLA_DK, dtype=BF16), *gqa_args,
      jnp.eye(nq, dtype=BF16), *hy_args)
    return outs[0], outs[1], outs[2], tuple(outs[3:7]), tuple(outs[7:10]), tuple(outs[10:13])


def _attn_kernel(qt_ref, qn_ref, k_ref, vt_ref, gate_ref, eye_ref, o_ref, sa_ref, sb_ref, pa_ref, pb_ref,
                 cm_ref, acc_ref, *, heads, group, nct_q, ctx_chunks, all_chunks, unroll):
    i = pl.program_id(1)
    n_loops = jnp.where(i < nct_q, (ctx_chunks - 1) // unroll, (all_chunks - 1) // unroll)
    tq = qt_ref.shape[3]
    dva = vt_ref.shape[3]
    dv = dva - ONES_ROWS

    def score(j, h, dst, q_ref=qt_ref):
        rows = pl.ds(pl.multiple_of(j * ATT_TK, ATT_TK), ATT_TK)
        s = jnp.dot(k_ref[0, h // group, rows, :], q_ref[0, h], preferred_element_type=F32)
        dst[h] = s
        return jnp.max(s, axis=0, keepdims=True)

    def value(j, h, p_ref):
        return jnp.dot(vt_ref[0, h // group, j], p_ref[h], preferred_element_type=F32)

    def step(j, carry, s_cur, s_nxt, p_prev, p_cur):
        state, cmax = carry
        new, nmax = [], []
        for h in range(heads):
            pv = value(jnp.maximum(j - 1, 0), h, p_prev)
            if s_nxt is not None:
                nmax.append(score(j + 1, h, s_nxt))
            m = state[h]
            m_new = jnp.maximum(m, cmax[h])
            alpha = jnp.exp2(m - m_new)
            p_cur[h] = jnp.exp2(s_cur[h] - m_new).astype(BF16)
            acc_ref[h] = alpha * (acc_ref[h] + pv)
            new.append(m_new)
        return tuple(new), tuple(nmax)

    bufs = ((sa_ref, sb_ref, pb_ref, pa_ref), (sb_ref, sa_ref, pa_ref, pb_ref))

    def body(t, carry):
        for u in range(unroll):
            carry = step(unroll * t + u, carry, *bufs[u % 2])
        return carry

    @pl.when(i == 0)
    def _():
        for h in range(heads):
            cm_ref[h] = score(0, h, sa_ref)

    cmax0 = tuple(cm_ref[h] for h in range(heads))
    pb_ref[...] = jnp.zeros_like(pb_ref)
    acc_ref[...] = jnp.zeros_like(acc_ref)
    init = tuple(jnp.full((1, tq), -jnp.inf, F32) for _ in range(heads))
    carry = lax.fori_loop(0, n_loops, body, (init, cmax0))
    last = unroll * n_loops
    step(last, carry, sa_ref, None, pb_ref, pa_ref)
    pvs = [value(last, h, pa_ref) for h in range(heads)]
    for h in range(heads):
        cm_ref[h] = score(0, h, sa_ref, qn_ref)
    outs = []
    for h in range(heads):
        acc = acc_ref[h] + pvs[h]
        outs.append((acc[0:dv] * (1.0 / acc[dv:dv + 1])).astype(BF16))
    out = lax.dot_general(jnp.concatenate(outs, axis=0), eye_ref[...], _TN, preferred_element_type=F32)
    g = gate_ref[0].astype(F32)
    o_ref[0] = (out * _silu(g)).astype(BF16)


def _attention(qt, k, vt, gate_src, gate_block, n_ctx):
    b, heads, dk, lt = qt.shape
    hk, n_chunks, dva = k.shape[1], vt.shape[2], vt.shape[3]
    assert heads * (dva - ONES_ROWS) == GROUP_W
    n_tiles = lt // ATT_TQ
    ctx_chunks, all_chunks = n_ctx // ATT_TK, lt // ATT_TK
    unroll = next(u for u in ATT_UNROLLS if (ctx_chunks - 1) % u == 0 and (all_chunks - 1) % u == 0)
    kern = functools.partial(_attn_kernel, heads=heads, group=heads // hk, nct_q=n_ctx // ATT_TQ,
                             ctx_chunks=ctx_chunks, all_chunks=all_chunks, unroll=unroll)
    return pl.pallas_call(
        kern,
        grid=(b, n_tiles),
        in_specs=[pl.BlockSpec((1, heads, dk, ATT_TQ), lambda bi, i: (bi, 0, 0, i)),
                  pl.BlockSpec((1, heads, dk, ATT_TQ), lambda bi, i: (bi, 0, 0, jnp.minimum(i + 1, n_tiles - 1))),
                  pl.BlockSpec((1, hk, lt, dk), lambda bi, i: (bi, 0, 0, 0)),
                  pl.BlockSpec((1, hk, n_chunks, dva, ATT_TK), lambda bi, i: (bi, 0, 0, 0, 0)),
                  pl.BlockSpec((1, ATT_TQ, GROUP_W), lambda bi, i: (bi, i, gate_block)),
                  pl.BlockSpec((GROUP_W, GROUP_W), lambda bi, i: (0, 0))],
        out_specs=pl.BlockSpec((1, ATT_TQ, GROUP_W), lambda bi, i: (bi, i, 0)),
        out_shape=jax.ShapeDtypeStruct((b, lt, GROUP_W), BF16),
        scratch_shapes=[pltpu.VMEM((heads, ATT_TK, ATT_TQ), F32), pltpu.VMEM((heads, ATT_TK, ATT_TQ), F32),
                        pltpu.VMEM((heads, ATT_TK, ATT_TQ), BF16), pltpu.VMEM((heads, ATT_TK, ATT_TQ), BF16),
                        pltpu.VMEM((heads, 1, ATT_TQ), F32), pltpu.VMEM((heads, dva, ATT_TQ), F32)],
        compiler_params=_cparams("parallel", "arbitrary"),
        name="attention",
    )(qt, qt, k, vt, gate_src, jnp.eye(GROUP_W, dtype=BF16))


def _ssm_kernel(*refs, steps, nb, reverse, finish):
    if finish:
        u_ref, b_ref, a_ref, c_ref, yf_ref, d_ref, gw_ref, gb_ref, o_ref, st, car = refs
    else:
        u_ref, b_ref, a_ref, c_ref, o_ref, st, car = refs

    @pl.when(pl.program_id(0) == 0)
    def _():
        car[...] = jnp.zeros_like(car)

    half = st.shape[0] // 2
    halves = (slice(0, half), slice(half, 2 * half))
    for hs in halves:
        st[hs] = jnp.dot(u_ref[hs, 0:GROUP_W], b_ref[...], preferred_element_type=F32)
    n_state = st.shape[1] // 2
    lw = min(SSM_LANE_SPLIT, n_state)
    for c0 in range(0, n_state, lw):
        lr = slice(c0, c0 + lw)
        li = slice(n_state + c0, n_state + c0 + lw)
        ar = jnp.broadcast_to(a_ref[:, lr], (nb, lw))
        ai = jnp.broadcast_to(a_ref[:, li], (nb, lw))

        def body(tt, carry, lr=lr, li=li, ar=ar, ai=ai):
            sr, si = carry
            t = (steps - 1 - tt) if reverse else tt
            rows = pl.ds(pl.multiple_of(t * nb, nb), nb)
            nr = ar * sr - ai * si + st[rows, lr]
            ni = ar * si + ai * sr + st[rows, li]
            st[rows, lr] = nr
            st[rows, li] = ni
            return nr, ni

        sr, si = lax.fori_loop(0, steps, body, (car[:, lr], car[:, li]), unroll=2)
        car[:, lr] = sr
        car[:, li] = si
    ys = [jnp.dot(st[hs].astype(BF16), c_ref[...], preferred_element_type=F32) for hs in halves]
    if not finish:
        for hs, y in zip(halves, ys):
            o_ref[hs] = y
    else:
        zs = [jax.nn.gelu(y + yf_ref[hs] + d_ref[...] * u_ref[hs, 0:GROUP_W].astype(F32), approximate=True)
              for hs, y in zip(halves, ys)]
        gls = [jnp.dot(z.astype(BF16), gw_ref[...], preferred_element_type=F32) + gb_ref[...] for z in zs]
        for hs, z, gl in zip(halves, zs, gls):
            gate = u_ref[hs, GROUP_W:2 * GROUP_W].astype(F32)
            o_ref[hs] = (z * jax.nn.sigmoid(gl) * _silu(gate)).astype(BF16)


def _ssm_direction(u_tm, mats, nb, n_ctx_chunks, reverse, finish_args=None):
    rows, _ = u_tm.shape
    steps = SSM_T
    blk = steps * nb
    n_chunks = rows // blk
    n_state2 = mats[0].shape[1]
    if reverse:
        cidx = lambda i: (jnp.where(i < n_ctx_chunks, n_ctx_chunks - 1 - i, n_chunks - 1 - (i - n_ctx_chunks)), 0)
    else:
        cidx = lambda i: (i, 0)
    const = lambda i: (0, 0)
    in_specs = [pl.BlockSpec((blk, SSM_PACK), cidx)] + [pl.BlockSpec(m.shape, const) for m in mats]
    args = [u_tm, *mats]
    finish = finish_args is not None
    if finish:
        yf, d_skip, glu_w, glu_b = finish_args
        in_specs += [pl.BlockSpec((blk, GROUP_W), cidx), pl.BlockSpec(d_skip.shape, const),
                     pl.BlockSpec(glu_w.shape, const), pl.BlockSpec(glu_b.shape, const)]
        args += [yf, d_skip, glu_w, glu_b]
    return pl.pallas_call(
        functools.partial(_ssm_kernel, steps=steps, nb=nb, reverse=reverse, finish=finish),
        grid=(n_chunks,),
        in_specs=in_specs,
        out_specs=pl.BlockSpec((blk, GROUP_W), cidx),
        out_shape=jax.ShapeDtypeStruct((rows, GROUP_W), BF16 if finish else F32),
        scratch_shapes=[pltpu.VMEM((blk, n_state2), F32), pltpu.VMEM((nb, n_state2), F32)],
        compiler_params=_cparams("arbitrary"),
        name="ssm_rev" if reverse else "ssm_fwd",
    )(*args)


def _ssm_matrices(lam_re, lam_im, log_step, b_re, b_im, c_re, c_im):
    lr, li = lam_re.astype(F32), lam_im.astype(F32)
    step = jnp.exp(log_step.astype(F32))[:, None]
    mag = jnp.exp(lr * step)
    a_re, a_im = mag * jnp.cos(li * step), mag * jnp.sin(li * step)
    den = lr * lr + li * li
    q_re = ((a_re - 1.0) * lr + a_im * li) / den
    q_im = (a_im * lr - (a_re - 1.0) * li) / den
    bb_re, bb_im = b_re.astype(F32), b_im.astype(F32)
    bbar_re = q_re[..., None] * bb_re - q_im[..., None] * bb_im
    bbar_im = q_re[..., None] * bb_im + q_im[..., None] * bb_re
    eye = jnp.eye(SSM_GROUPS, dtype=F32)
    n_in = SSM_GROUPS * SSM_GROUP
    n_state = SSM_GROUPS * SSM_STATE

    def drive(m):
        return jnp.einsum('gph,gk->ghkp', m, eye).reshape(n_in, n_state).astype(BF16)

    def readout(m):
        return jnp.einsum('ghp,gk->gpkh', m, eye).reshape(n_state, n_in).astype(BF16)

    return (jnp.concatenate([drive(bbar_re), drive(bbar_im)], axis=1),
            jnp.concatenate([a_re.reshape(1, n_state), a_im.reshape(1, n_state)], axis=1),
            jnp.concatenate([readout(c_re.astype(F32)), readout(-c_im.astype(F32))], axis=0))


def _hy_pre_body(p, prev_row, next_row, w_ref, b_ref, v_ref, x1_ref, x2_ref, sg_ref):
    nconv = 3 * HY_W
    x = p[:, 0:nconv]
    tm = x.shape[0]
    rid = lax.broadcasted_iota(jnp.int32, x.shape, 0)
    xm = jnp.where(rid == 0, prev_row, pltpu.roll(x, 1, 0))
    xp = jnp.where(rid == tm - 1, next_row, pltpu.roll(x, tm - 1, 0))
    proj = xm * w_ref[0:1] + x * w_ref[1:2] + xp * w_ref[2:3] + b_ref[...]
    for o_ref, val in ((v_ref, proj[:, 0:HY_W]), (x1_ref, proj[:, HY_W:2 * HY_W]),
                       (x2_ref, proj[:, 2 * HY_W:3 * HY_W]), (sg_ref, _silu(p[:, nconv:nconv + HY_W]))):
        o_ref[0] = val.astype(BF16)


def _bitrev(p, bits):
    r = 0
    for i in range(bits):
        r = (r << 1) | ((p >> i) & 1)
    return r


@functools.lru_cache(maxsize=None)
def _fft_plan(n):
    r = FFT_R
    nb = n // r
    m = 2 * nb
    big = 2 * n
    bits = int(round(math.log2(nb)))
    assert nb >= 2 and (1 << bits) == nb
    jmap = np.zeros(m, np.int64)
    for p in range(nb):
        jmap[p] = 2 * _bitrev(p, bits)
        jmap[nb + p] = 2 * _bitrev(p, bits) + 1
    k2 = np.arange(r)
    n2 = np.arange(r)
    g = np.zeros((m, 2 * r, 2 * r), np.float32)
    for s in range(m):
        ang = -2.0 * np.pi * (np.outer(k2, n2) / r + np.outer(np.ones(r), n2) * jmap[s] / big)
        gr, gi = np.cos(ang), np.sin(ang)
        g[s] = np.block([[gr, -gi], [gi, gr]])
    half0 = nb // 2
    k = np.arange(half0)
    wnb = np.exp(-2j * np.pi * k / nb)
    wm0 = np.exp(-2j * np.pi * k / m)
    wm1 = np.exp(-2j * np.pi * (k + half0) / m)
    tw_edge = np.stack([wnb.real, wnb.imag, wm0.real, wm0.imag, wm1.real, wm1.imag]).astype(np.float32)
    mids = []
    h = half0 // 2
    while h >= 1:
        kk = np.arange(nb // 2) % h
        w = np.exp(-2j * np.pi * kk / (2 * h))
        mids.append(np.stack([w.real, w.imag]))
        h //= 2
    tw_mid = (np.concatenate(mids, 0) if mids else np.zeros((2, max(nb // 2, 1)))).astype(np.float32)
    freq = (jmap[:, None] + m * k2[None, :])
    return dict(nb=nb, m=m, g=g, tw_edge=tw_edge, tw_mid=tw_mid, n_mid=len(mids), freq=freq)


def _cmul(ar, ai, wr, wi):
    return ar * wr - ai * wi, ar * wi + ai * wr


def _slot_pair(bb, lg, nb):
    half0 = nb // 2
    branch = bb // half0
    bf = bb - branch * half0
    i0 = branch * nb + ((bf >> lg) << (lg + 1)) + (bf & ((1 << lg) - 1))
    return bf, i0, i0 + (1 << lg)


def _block_dft_forward(read, w, tw_edge, tw_mid, nb, n_mid):
    r = FFT_R
    half0 = nb // 2
    re, im = slice(0, r), slice(r, 2 * r)

    def first(k, c):
        wr, wi = tw_edge[0, k], tw_edge[1, k]
        ar, ai = read(k)
        br, bi = read(k + half0)
        w[k, re], w[k, im] = ar + br, ai + bi
        dr, di = _cmul(ar - br, ai - bi, wr, wi)
        w[k + half0, re], w[k + half0, im] = dr, di
        a2r, a2i = _cmul(ar, ai, tw_edge[2, k], tw_edge[3, k])
        b2r, b2i = _cmul(br, bi, tw_edge[4, k], tw_edge[5, k])
        w[nb + k, re], w[nb + k, im] = a2r + b2r, a2i + b2i
        dr, di = _cmul(a2r - b2r, a2i - b2i, wr, wi)
        w[nb + k + half0, re], w[nb + k + half0, im] = dr, di
        return c

    lax.fori_loop(0, half0, first, 0)
    _mid_stages(w, tw_mid, nb, n_mid, inverse=False)


def _mid_stages(w, tw_mid, nb, n_mid, inverse):
    r = FFT_R
    half0 = nb // 2
    re, im = slice(0, r), slice(r, 2 * r)
    lg0 = int(round(math.log2(half0))) - 1
    sign = -1.0 if inverse else 1.0

    def tw(s, k):
        return tw_mid[2 * s, k], sign * tw_mid[2 * s + 1, k]

    def bfly(a, b, t):
        (ar, ai), (br, bi) = a, b
        if inverse:
            br, bi = _cmul(br, bi, *t)
            return (ar + br, ai + bi), (ar - br, ai - bi)
        return (ar + br, ai + bi), _cmul(ar - br, ai - bi, *t)

    def single(s):
        lg = lg0 - s

        def body(bb, c):
            bf, i0, i1 = _slot_pair(bb, lg, nb)
            x0, x1 = bfly((w[i0, re], w[i0, im]), (w[i1, re], w[i1, im]), tw(s, bf))
            w[i0, re], w[i0, im] = x0
            w[i1, re], w[i1, im] = x1
            return c

        lax.fori_loop(0, nb, body, 0)

    def double(s):
        lg2 = lg0 - s - 1
        h2 = 1 << lg2
        quarter = nb // 4

        def body(bb, c):
            branch = bb // quarter
            u = bb - branch * quarter
            k = u & (h2 - 1)
            i0 = branch * nb + ((u >> lg2) << (lg2 + 2)) + k
            idx = [i0, i0 + h2, i0 + 2 * h2, i0 + 3 * h2]
            x = [(w[i, re], w[i, im]) for i in idx]
            outer = lambda x: (bfly(x[0], x[2], tw(s, k)), bfly(x[1], x[3], tw(s, k + h2)))
            inner = lambda x: (bfly(x[0], x[1], tw(s + 1, k)), bfly(x[2], x[3], tw(s + 1, k)))
            if inverse:
                (x0, x1), (x2, x3) = inner(x)
                (x0, x2), (x1, x3) = outer([x0, x1, x2, x3])
            else:
                (x0, x2), (x1, x3) = outer(x)
                (x0, x1), (x2, x3) = inner([x0, x1, x2, x3])
            for i, v in zip(idx, (x0, x1, x2, x3)):
                w[i, re], w[i, im] = v
            return c

        lax.fori_loop(0, nb // 2, body, 0)

    lone = [0] if n_mid % 2 else []
    pairs = list(range(len(lone), n_mid, 2))
    if inverse:
        for s in reversed(pairs):
            double(s)
        for s in lone:
            single(s)
    else:
        for s in lone:
            single(s)
        for s in pairs:
            double(s)


def _hy_conv_kernel(*refs, nb, n_mid, n_post):
    tw_edge, tw_mid, a_ref, bias_ref, kh_ref, g_ref, gi_ref = refs[:7]
    post_refs = refs[7:7 + n_post]
    o_ref, w = refs[7 + n_post], refs[8 + n_post]
    r = FFT_R
    m = 2 * nb
    half0 = nb // 2
    re, im = slice(0, r), slice(r, 2 * r)

    def rows(k):
        return pl.ds(pl.multiple_of(k * r, r), r)

    _block_dft_forward(lambda k: (a_ref[0, rows(k), :].astype(F32), a_ref[1, rows(k), :].astype(F32)),
                       w, tw_edge, tw_mid, nb, n_mid)

    per_trip = min(FFT_SLOTS, m)

    def spectral(t, c):
        slots = [per_trip * t + u for u in range(per_trip)]
        xs = [jnp.dot(g_ref[s], w[s].astype(BF16), preferred_element_type=F32) for s in slots]
        outs = []
        for s, x in zip(slots, xs):
            yr, yi = _cmul(x[re], x[im], kh_ref[0, s, re, :], kh_ref[0, s, im, :])
            y = jnp.concatenate([yr, yi], axis=0).astype(BF16)
            outs.append(jnp.dot(gi_ref[s], y, preferred_element_type=F32))
        for s, o in zip(slots, outs):
            w[s] = o
        return c

    lax.fori_loop(0, m // per_trip, spectral, 0)

    _mid_stages(w, tw_mid, nb, n_mid, inverse=True)

    bias = bias_ref[0]

    def emit(k, yr, yi):
        for bsel, y in ((0, yr), (1, yi)):
            a = a_ref[bsel, rows(k), :].astype(F32)
            val = y + bias * a
            for p_ref in post_refs:
                val = val * p_ref[bsel, rows(k), :].astype(F32)
            o_ref[bsel, rows(k), :] = val.astype(o_ref.dtype)

    def last(k, c):
        wr, wi = tw_edge[0, k], -tw_edge[1, k]
        ar, ai = w[k, re], w[k, im]
        br, bi = _cmul(w[k + half0, re], w[k + half0, im], wr, wi)
        cr, ci = w[nb + k, re], w[nb + k, im]
        dr, di = _cmul(w[nb + k + half0, re], w[nb + k + half0, im], wr, wi)
        o0r, o0i = _cmul(cr + dr, ci + di, tw_edge[2, k], -tw_edge[3, k])
        o1r, o1i = _cmul(cr - dr, ci - di, tw_edge[4, k], -tw_edge[5, k])
        emit(k, ar + br + o0r, ai + bi + o0i)
        emit(k + half0, ar - br + o1r, ai - bi + o1i)
        return c

    lax.fori_loop(0, half0, last, 0)


def _hy_conv(a, khat, bias, posts, n, out_dtype):
    b, _, width = a[0].shape
    lanes = LANES
    halves = width // lanes
    plan = _fft_plan(n)
    nb, m = plan['nb'], plan['m']
    r = FFT_R

    def window(row0):
        return pl.BlockSpec((pl.Element(2), pl.Element(n), pl.Element(lanes)),
                            lambda hf, pr: (2 * pr, row0, lanes * hf))

    sig = window(a[1])
    smem = pl.BlockSpec(memory_space=pltpu.SMEM)
    once = pl.Buffered(1)
    in_specs = [smem, smem, sig,
                pl.BlockSpec((1, 1, lanes), lambda hf, pr: (hf, 0, 0)),
                pl.BlockSpec((1, m, 2 * r, lanes), lambda hf, pr: (hf, 0, 0, 0), pipeline_mode=once),
                pl.BlockSpec((m, 2 * r, 2 * r), lambda hf, pr: (0, 0, 0), pipeline_mode=once),
                pl.BlockSpec((m, 2 * r, 2 * r), lambda hf, pr: (0, 0, 0), pipeline_mode=once)]
    in_specs += [window(row0) for _, row0 in posts]
    g_fwd = jnp.asarray(plan['g'], BF16)
    g_inv = jnp.asarray(np.swapaxes(plan['g'], 1, 2), BF16)
    return pl.pallas_call(
        functools.partial(_hy_conv_kernel, nb=nb, n_mid=plan['n_mid'], n_post=len(posts)),
        grid=(halves, b // 2),
        in_specs=in_specs,
        out_specs=pl.BlockSpec((2, n, lanes), lambda hf, pr: (pr, 0, hf)),
        out_shape=jax.ShapeDtypeStruct((b, n, width), out_dtype),
        scratch_shapes=[pltpu.VMEM((m, 2 * r, lanes), F32)],
        compiler_params=pltpu.CompilerParams(dimension_semantics=("arbitrary", "arbitrary"),
                                             vmem_limit_bytes=HY_VMEM_LIMIT),
        name="hyena_conv",
    )(jnp.asarray(plan['tw_edge']), jnp.asarray(plan['tw_mid']), a[0], bias, khat, g_fwd, g_inv,
      *[p for p, _ in posts])


_HP = lax.Precision.HIGHEST
HY_TAPS_W = HY_ORDER * 2 * HY_W
HY_FEAT = 128


def _hy_taps_kernel(z_ref, dec_ref, w1_ref, b1_ref, f1_ref, w2_ref, b2_ref, f2_ref, w3_ref, o_ref,
                    h_scr, ss_scr):
    phase, i = pl.program_id(0), pl.program_id(1)
    tm = z_ref.shape[0]
    rows = pl.ds(pl.multiple_of(i * tm, tm), tm)

    @pl.when(phase == 0)
    def _():
        @pl.when(i == 0)
        def _():
            ss_scr[...] = jnp.zeros_like(ss_scr)

        h = jnp.dot(z_ref[...], w1_ref[...], precision=_HP, preferred_element_type=F32)
        h = jnp.sin(f1_ref[...] * (h + b1_ref[...]))
        h = jnp.dot(h, w2_ref[...], precision=_HP, preferred_element_type=F32)
        h = jnp.sin(f2_ref[...] * (h + b2_ref[...]))
        dec = dec_ref[...]
        for c0 in range(0, HY_TAPS_W, HY_W):
            t = jnp.dot(h, w3_ref[:, c0:c0 + HY_W], precision=_HP, preferred_element_type=F32) * dec
            h_scr[rows, c0:c0 + HY_W] = t
            ss_scr[:, c0:c0 + HY_W] += jnp.sum(t * t, axis=0, keepdims=True)

    @pl.when(phase == 1)
    def _():
        first_row = lax.broadcasted_iota(jnp.int32, (tm, HY_W), 0) == jnp.where(i == 0, 0, -1)
        for o in range(HY_ORDER):
            c0 = o * 2 * HY_W
            tot = ss_scr[:, c0:c0 + HY_W] + ss_scr[:, c0 + HY_W:c0 + 2 * HY_W]
            inv = lax.rsqrt(tot + EPS)
            o_ref[:, c0:c0 + HY_W] = h_scr[rows, c0:c0 + HY_W] * inv
            o_ref[:, c0 + HY_W:c0 + 2 * HY_W] = jnp.where(first_row, 0.0, h_scr[rows, c0 + HY_W:c0 + 2 * HY_W] * inv)


@functools.lru_cache(maxsize=None)
def _filter_features(n):
    t = np.linspace(0.0, 1.0, n, dtype=np.float32)[:, None]
    omega = (2.0 * np.pi * np.arange(n, dtype=np.float32)[:, None] / n).astype(np.float32)
    bands = np.linspace(1e-4, HY_BANDS - 1, HY_BANDS, dtype=np.float32)[None, :]
    z = np.zeros((n, HY_FEAT), np.float32)
    z[:, 0:1] = t
    z[:, 1:1 + HY_BANDS] = np.cos(bands * omega)
    z[:, 1 + HY_BANDS:HY_EMB] = -np.sin(bands * omega)
    max_decay = math.log(HY_DECAY_TARGET) / HY_FAST_DECAY
    min_decay = math.log(HY_DECAY_TARGET) / HY_SLOW_DECAY
    deltas = np.linspace(min_decay, max_decay, HY_W, dtype=np.float32)
    dec = np.exp(-t * np.abs(deltas)).astype(np.float32)
    return z, dec


def _hy_taps(n, w1, b1, fr1, w2, b2, fr2, w3):
    z, dec = _filter_features(n)
    tm = min(n, 512)
    pad = HY_FEAT - HY_HIDDEN
    w1p = jnp.pad(w1.astype(F32), ((0, HY_FEAT - HY_EMB), (0, pad)))
    w2p = jnp.pad(w2.astype(F32), ((0, pad), (0, pad)))
    w3p = jnp.pad(w3.astype(F32), ((0, pad), (0, 0)))
    vec = lambda v: jnp.pad(v.astype(F32), (0, pad)).reshape(1, HY_FEAT)
    const = lambda ph, i: (0, 0)
    return pl.pallas_call(
        _hy_taps_kernel,
        grid=(2, n // tm),
        in_specs=[pl.BlockSpec((tm, HY_FEAT), lambda ph, i: (i * (1 - ph), 0)),
                  pl.BlockSpec((tm, HY_W), lambda ph, i: (i * (1 - ph), 0)),
                  pl.BlockSpec((HY_FEAT, HY_FEAT), const), pl.BlockSpec((1, HY_FEAT), const),
                  pl.BlockSpec((1, HY_FEAT), const),
                  pl.BlockSpec((HY_FEAT, HY_FEAT), const), pl.BlockSpec((1, HY_FEAT), const),
                  pl.BlockSpec((1, HY_FEAT), const),
                  pl.BlockSpec((HY_FEAT, HY_TAPS_W), const)],
        out_specs=pl.BlockSpec((tm, HY_TAPS_W), lambda ph, i: (i * ph, 0)),
        out_shape=jax.ShapeDtypeStruct((n, HY_TAPS_W), F32),
        scratch_shapes=[pltpu.VMEM((n, HY_TAPS_W), F32), pltpu.VMEM((1, HY_TAPS_W), F32)],
        compiler_params=_cparams("arbitrary", "arbitrary"),
        name="hyena_taps",
    )(jnp.asarray(z), jnp.asarray(dec), w1p, vec(b1), vec(fr1), w2p, vec(b2), vec(fr2), w3p)


def _hy_spec_kernel(tw_edge, tw_mid, f_ref, b_ref, ghi_ref, glo_ref, o_ref, w, *, nb, n_mid, inv_n):
    r = FFT_R
    m = 2 * nb
    re, im = slice(0, r), slice(r, 2 * r)
    zeros = jnp.zeros((r, LANES), F32)

    def rows(k):
        return pl.ds(pl.multiple_of(k * r, r), r)

    per_trip = min(4, m)

    def in_block_dfts(t):
        slots = [per_trip * t + u for u in range(per_trip)]
        ys = []
        for s in slots:
            x = w[s]
            xh = x.astype(BF16)
            xl = (x - xh.astype(F32)).astype(BF16)
            gh = ghi_ref[s]
            y = jnp.dot(gh, xh, preferred_element_type=F32)
            y += jnp.dot(gh, xl, preferred_element_type=F32)
            y += jnp.dot(glo_ref[s], xh, preferred_element_type=F32)
            ys.append(y * inv_n)
        return slots, ys

    _block_dft_forward(lambda k: (f_ref[rows(k), :], zeros), w, tw_edge, tw_mid, nb, n_mid)

    def fwd_part(t, c):
        for s, y in zip(*in_block_dfts(t)):
            o_ref[0, 0, s] = y
        return c

    lax.fori_loop(0, m // per_trip, fwd_part, 0)
    _block_dft_forward(lambda k: (b_ref[rows(k), :], zeros), w, tw_edge, tw_mid, nb, n_mid)

    def bwd_part(t, c):
        for s, y in zip(*in_block_dfts(t)):
            o_ref[0, 0, s, re] += y[re]
            o_ref[0, 0, s, im] -= y[im]
        return c

    lax.fori_loop(0, m // per_trip, bwd_part, 0)


def _hy_filter_spectrum(taps):
    n = taps.shape[0]
    plan = _fft_plan(n)
    nb, m = plan['nb'], plan['m']
    r = FFT_R
    halves = HY_W // LANES
    g = plan['g']
    g_hi = g.astype(jnp.bfloat16)
    g_lo = (g - np.asarray(g_hi, np.float32)).astype(jnp.bfloat16)
    smem = pl.BlockSpec(memory_space=pltpu.SMEM)
    once = pl.Buffered(1)
    table = lambda o, hf: (0, 0, 0)
    return pl.pallas_call(
        functools.partial(_hy_spec_kernel, nb=nb, n_mid=plan['n_mid'], inv_n=1.0 / (2 * n)),
        grid=(HY_ORDER, halves),
        in_specs=[smem, smem,
                  pl.BlockSpec((n, LANES), lambda o, hf: (0, o * 2 * halves + hf)),
                  pl.BlockSpec((n, LANES), lambda o, hf: (0, o * 2 * halves + halves + hf)),
                  pl.BlockSpec((m, 2 * r, 2 * r), table, pipeline_mode=once),
                  pl.BlockSpec((m, 2 * r, 2 * r), table, pipeline_mode=once)],
        out_specs=pl.BlockSpec((1, 1, m, 2 * r, LANES), lambda o, hf: (o, hf, 0, 0, 0)),
        out_shape=jax.ShapeDtypeStruct((HY_ORDER, halves, m, 2 * r, LANES), F32),
        scratch_shapes=[pltpu.VMEM((m, 2 * r, LANES), F32)],
        compiler_params=pltpu.CompilerParams(dimension_semantics=("arbitrary", "arbitrary"),
                                             vmem_limit_bytes=HY_VMEM_LIMIT),
        name="hyena_spectrum",
    )(jnp.asarray(plan['tw_edge']), jnp.asarray(plan['tw_mid']), taps, taps,
      jnp.asarray(g_hi), jnp.asarray(g_lo))


def _pack_w_in(w_in):
    d = w_in.shape[0]
    z = lambda n: jnp.zeros((d, n), w_in.dtype)
    o = 0
    q_lat = w_in[:, o:o + MLA_Q_RANK]; o += MLA_Q_RANK
    kv_lat = w_in[:, o:o + MLA_KV_RANK]; o += MLA_KV_RANK
    k_rope = w_in[:, o:o + MLA_ROPE]; o += MLA_ROPE
    gate = w_in[:, o:o + GROUP_W]; o += GROUP_W
    mla = jnp.concatenate([q_lat, z(256 - MLA_Q_RANK), kv_lat, z(MLA_NOPE), k_rope,
                           z(MLA_DK - MLA_NOPE - MLA_ROPE), gate], axis=1)
    return jnp.concatenate([mla, w_in[:, o:]], axis=1).astype(BF16)


def _pack_mla_up(w_uq, w_ukv):
    dq = MLA_NOPE + MLA_ROPE
    wq = w_uq.reshape(MLA_Q_RANK, MLA_HEADS, dq).transpose(1, 0, 2)
    wq = jnp.pad(wq, ((0, 0), (0, 256 - MLA_Q_RANK), (0, MLA_DK - dq))).astype(BF16)
    wkv = w_ukv.reshape(MLA_KV_RANK, MLA_HEADS, MLA_NOPE + MLA_V).transpose(1, 0, 2)
    wk = jnp.pad(wkv[:, :, :MLA_NOPE], ((0, 0), (0, 0), (0, MLA_DK - MLA_NOPE))).astype(BF16)
    wvt = jnp.swapaxes(wkv[:, :, MLA_NOPE:], 1, 2).astype(BF16)
    return wq, wk, wvt


def kernel(x, c, ctx, c_ctx, w_mod, b_mod, g_pre, g_post, w_in, w_out, mla_g_cq, mla_w_uq, mla_g_ckv, mla_w_ukv, gqa_g_q, gqa_g_k, ssm_lambda_re, ssm_lambda_im, ssm_log_step, ssm_b_re, ssm_b_im, ssm_c_re, ssm_c_im, ssm_d, ssm_glu_w, ssm_glu_b, hy_conv_w, hy_conv_b, hy_f_w1, hy_f_b1, hy_f_freq1, hy_f_w2, hy_f_b2, hy_f_freq2, hy_f_w3, hy_bias):
    b, n_lat, d = x.shape
    n_ctx = ctx.shape[1]
    depth = w_in.shape[0]
    lt = n_ctx + n_lat
    nct = n_ctx // ROW_TILE
    assert n_ctx % ROW_TILE == 0 and n_lat % ROW_TILE == 0 and b % 2 == 0 and b % 8 == 0

    n_cond = -(-(b + 1) // 8) * 8
    cond = jnp.zeros((n_cond, d), F32).at[:b].set(c).at[b].set(c_ctx)
    mod = _modulation(cond, w_mod, b_mod)[:, :b + 1]
    shift, scale, gate = mod[..., :d], mod[..., d:2 * d], mod[..., 2 * d:]

    mla_cos, mla_sin, gqa_cos, gqa_sin = _rope_tables(n_ctx, n_lat)
    nq = GQA_HEADS * GQA_DIM
    ones_bd = jnp.asarray(np.kron(np.eye(GQA_HEADS), np.ones((GQA_DIM, GQA_DIM))), BF16)

    xa = jnp.concatenate([ctx, x], axis=1)
    for l in range(depth):
        sc = (g_pre[l][None, :] * (1.0 + scale[l]))[:, None, :]
        sh = shift[l][:, None, :]
        wq, wk, wvt = _pack_mla_up(mla_w_uq[l], mla_w_ukv[l])
        g_cq = jnp.pad(mla_g_cq[l], (0, 256 - MLA_Q_RANK)).reshape(1, 256)
        mla_args = (mla_cos, mla_sin, g_cq, mla_g_ckv[l].reshape(1, -1), wq, wk, wvt)
        gqa_args = (gqa_cos, gqa_sin, jnp.tile(gqa_g_q[l], GQA_HEADS).reshape(1, nq),
                    jnp.tile(gqa_g_k[l], GQA_KV_HEADS).reshape(1, -1), ones_bd)
        hy_args = (hy_conv_w[l], hy_conv_b[l].reshape(1, -1))
        gate_m, gate_g, p_ssm, (hv, hx1, hx2, hsg), mla_qkv, gqa_qkv = _inproj(
            xa, sc, sh, _pack_w_in(w_in[l]), nct, mla_args, gqa_args, hy_args)
        a_out = _attention(*mla_qkv, gate_m, 0, n_ctx)
        g_out = _attention(*gqa_qkv, gate_g, 0, n_ctx)

        u_tm = jnp.transpose(p_ssm, (1, 0, 2)).reshape(lt * b, SSM_PACK)
        mats = [_ssm_matrices(ssm_lambda_re[l, di], ssm_lambda_im[l, di], ssm_log_step[l, di],
                              ssm_b_re[l, di], ssm_b_im[l, di], ssm_c_re[l, di], ssm_c_im[l, di])
                for di in range(2)]
        n_ctx_chunks = n_ctx // SSM_T
        y_f = _ssm_direction(u_tm, mats[0], b, n_ctx_chunks, False)
        fin = (y_f, ssm_d[l].reshape(1, -1), ssm_glu_w[l].astype(BF16), ssm_glu_b[l].reshape(1, -1))
        s_tm = _ssm_direction(u_tm, mats[1], b, n_ctx_chunks, True, fin)
        s_out = jnp.transpose(s_tm.reshape(lt, b, GROUP_W), (1, 0, 2))

        filt = (hy_f_w1[l], hy_f_b1[l], hy_f_freq1[l], hy_f_w2[l], hy_f_b2[l], hy_f_freq2[l], hy_f_w3[l])
        bias = hy_bias[l].astype(F32).reshape(HY_ORDER, HY_W // LANES, 1, LANES)
        parts = []
        for row0, n in ((0, n_ctx), (n_ctx, n_lat)):
            khat = _hy_filter_spectrum(_hy_taps(n, *filt))
            z1 = _hy_conv((hv, row0), khat[0], bias[0], ((hx1, row0),), n, BF16)
            parts.append(_hy_conv((z1, 0), khat[1], bias[1], ((hx2, row0), (hsg, row0)), n, BF16))
        y_out = jnp.concatenate(parts, axis=1)

        xa = _outproj(a_out, g_out, s_out, y_out, xa, gate[l][:, None, :], g_post[l].reshape(1, d),
                      w_out[l].astype(BF16), nct, tile0=nct if l == depth - 1 else 0)
    return xa
```

```python
import functools
import math

import numpy as np
import jax
import jax.numpy as jnp
from jax import lax
from jax.experimental import pallas as pl
from jax.experimental.pallas import tpu as pltpu

F32 = jnp.float32
BF16 = jnp.bfloat16

GRID_W = 64
ROPE_BASE = 10000.0
EPS = 1e-6
GROUP_W = 256
MLA_HEADS, MLA_NOPE, MLA_ROPE, MLA_V = 4, 64, 32, 64
MLA_Q_RANK, MLA_KV_RANK = 192, 128
GQA_HEADS, GQA_KV_HEADS, GQA_DIM = 4, 2, 64
SSM_GROUPS, SSM_GROUP, SSM_STATE = 16, 16, 64
HY_W, HY_ORDER, HY_EMB, HY_HIDDEN = 256, 2, 33, 64
HY_BANDS = (HY_EMB - 1) // 2
HY_FAST_DECAY, HY_SLOW_DECAY, HY_DECAY_TARGET = 0.3, 1.5, 1e-2

MLA_PACK = 768
GQA_PACK = 768
SSM_PACK = 512
HY_PACK = 1024
IN_PACK = MLA_PACK + GQA_PACK + SSM_PACK + HY_PACK
MLA_DK = 128

LANES = 128
ROW_TILE = 256
ATT_TQ = 256
ATT_TK = 256
ATT_UNROLLS = (16, 8, 4, 2)
ONES_ROWS = 16
SSM_T = 128
SSM_LANE_SPLIT = 512
FFT_R = 128
FFT_SLOTS = 16
VMEM_LIMIT = 52 * 1024 * 1024
HY_VMEM_LIMIT = 58 * 1024 * 1024


def _cparams(*sem):
    return pltpu.CompilerParams(dimension_semantics=sem, vmem_limit_bytes=VMEM_LIMIT)


def _silu(x):
    return x * jax.nn.sigmoid(x)


def _mod_kernel(c_ref, w_ref, b_ref, o_ref):
    c = c_ref[...]
    s = _silu(c).astype(BF16)
    o_ref[0] = jnp.dot(s, w_ref[0].astype(BF16), preferred_element_type=F32) + b_ref[0]


def _modulation(cond, w_mod, b_mod):
    depth, d, n3 = w_mod.shape
    r = cond.shape[0]
    tn = 512
    return pl.pallas_call(
        _mod_kernel,
        grid=(depth, n3 // tn),
        in_specs=[pl.BlockSpec((r, d), lambda l, j: (0, 0)),
                  pl.BlockSpec((1, d, tn), lambda l, j: (l, 0, j)),
                  pl.BlockSpec((1, 1, tn), lambda l, j: (l, 0, j))],
        out_specs=pl.BlockSpec((1, r, tn), lambda l, j: (l, 0, j)),
        out_shape=jax.ShapeDtypeStruct((depth, r, n3), F32),
        compiler_params=_cparams("parallel", "parallel"),
        name="modulation",
    )(cond, w_mod, b_mod.reshape(depth, 1, n3))


def _outproj_kernel(a_ref, g_ref, s_ref, y_ref, x_ref, gt_ref, gp_ref, w_ref, o_ref):
    half = a_ref.shape[1] // 2
    for hs in (slice(0, half), slice(half, 2 * half)):
        acc = jnp.dot(a_ref[0, hs], w_ref[0:GROUP_W], preferred_element_type=F32)
        acc += jnp.dot(g_ref[0, hs], w_ref[GROUP_W:2 * GROUP_W], preferred_element_type=F32)
        acc += jnp.dot(s_ref[hs], w_ref[2 * GROUP_W:3 * GROUP_W], preferred_element_type=F32)
        acc += jnp.dot(y_ref[0, hs], w_ref[3 * GROUP_W:4 * GROUP_W], preferred_element_type=F32)
        ms = jnp.mean(acc * acc, axis=-1, keepdims=True)
        o_ref[0, hs] = x_ref[0, hs] + gt_ref[0] * (acc * lax.rsqrt(ms + EPS) * gp_ref[...])


def _outproj(a, g, s, y, xa, gate, g_post, w, nct, tile0=0):
    b, lt, d = xa.shape
    tm = ROW_TILE

    def mod_idx(bi, i):
        return (jnp.where(i + tile0 < nct, b, bi), 0, 0)

    row = lambda bi, i: (bi, i + tile0, 0)
    act = pl.BlockSpec((1, tm, GROUP_W), row)
    return pl.pallas_call(
        _outproj_kernel,
        grid=(b, lt // tm - tile0),
        in_specs=[act, act, pl.BlockSpec((tm, GROUP_W), lambda bi, i: (i + tile0, bi)), act] + [
            pl.BlockSpec((1, tm, d), row),
            pl.BlockSpec((1, 1, d), mod_idx),
            pl.BlockSpec((1, d), lambda bi, i: (0, 0)),
            pl.BlockSpec((4 * GROUP_W, d), lambda bi, i: (0, 0))],
        out_specs=pl.BlockSpec((1, tm, d), lambda bi, i: (bi, i, 0)),
        out_shape=jax.ShapeDtypeStruct((b, lt - tile0 * tm, d), F32),
        compiler_params=_cparams("parallel", "parallel"),
        name="outproj",
    )(a, g, s, y, xa, gate, g_post, w)


def _rope(x, cos, sin, shift):
    w = x.shape[-1]
    lane = lax.broadcasted_iota(jnp.int32, x.shape, 1)
    first = (lane & shift) == 0
    swapped = jnp.where(first, -pltpu.roll(x, w - shift, 1), pltpu.roll(x, shift, 1))
    return x * cos + swapped * sin


def _rope_tables(n_ctx, n_lat):
    t = np.arange(n_lat)
    row = (t // GRID_W).astype(np.float64)
    col = (t % GRID_W).astype(np.float64)

    def block(pos, h):
        inv = ROPE_BASE ** (-np.arange(0, h, 2, dtype=np.float64) / h)
        ang = (pos[:, None].astype(np.float32) * inv[None, :].astype(np.float32)).astype(np.float32)
        c, s = np.cos(ang), np.sin(ang)
        return np.concatenate([c, c], -1), np.concatenate([s, s], -1)

    def full(h, lead, width):
        cr, sr = block(row, h)
        cc, sc = block(col, h)
        cos = np.ones((n_ctx + n_lat, width), np.float32)
        sin = np.zeros((n_ctx + n_lat, width), np.float32)
        cos[n_ctx:, lead:lead + 2 * h] = np.concatenate([cr, cc], -1)
        sin[n_ctx:, lead:lead + 2 * h] = np.concatenate([sr, sc], -1)
        return cos, sin

    mc, ms = full(MLA_ROPE // 2, MLA_NOPE, MLA_DK)
    gc, gs = full(GQA_DIM // 2, 0, GQA_DIM)
    reps = GQA_HEADS
    return (jnp.asarray(mc), jnp.asarray(ms),
            jnp.asarray(np.tile(gc, (1, reps))), jnp.asarray(np.tile(gs, (1, reps))))


_NT = (((1,), (1,)), ((), ()))
_TN = (((0,), (0,)), ((), ()))
LOG2E = math.log2(math.e)


def _mla_prep_body(p, cos_ref, sin_ref, gq_ref, gkv_ref, wq_ref, wk_ref, wvt_ref, eye_ref,
                   qt_ref, k_ref, vt_ref):
    cos, sin = cos_ref[...], sin_ref[...]
    ql = p[:, 0:256]
    rq = lax.rsqrt(jnp.sum(ql * ql, axis=-1, keepdims=True) * (1.0 / MLA_Q_RANK) + EPS)
    qn = (ql * rq * gq_ref[...]).astype(BF16)
    kvl = p[:, 256:384]
    rk = lax.rsqrt(jnp.mean(kvl * kvl, axis=-1, keepdims=True) + EPS)
    kvn = (kvl * rk * gkv_ref[...]).astype(BF16)
    k_rope = _rope(p[:, 384:512], cos, sin, MLA_ROPE // 4)
    scale = (MLA_NOPE + MLA_ROPE) ** -0.5 * LOG2E
    heads = range(MLA_HEADS)
    qs = [jnp.dot(qn, wq_ref[h], preferred_element_type=F32) for h in heads]
    ks = [jnp.dot(kvn, wk_ref[h], preferred_element_type=F32) for h in heads]
    vts = [lax.dot_general(wvt_ref[h], kvn, _NT, preferred_element_type=F32) for h in heads]
    for h in heads:
        k_ref[0, h] = (ks[h] + k_rope).astype(BF16)
        vt_ref[0, h, 0, 0:MLA_V] = vts[h].astype(BF16)
        vt_ref[0, h, 0, MLA_V:MLA_V + ONES_ROWS] = jnp.ones((ONES_ROWS, kvn.shape[0]), BF16)
    qr = [(_rope(q, cos, sin, MLA_ROPE // 4) * scale).astype(BF16) for q in qs]
    for h in heads:
        qt_ref[0, h] = lax.dot_general(eye_ref[...], qr[h], _NT, preferred_element_type=F32).astype(BF16)


def _head_mean_sq(x, ones_bd):
    sq = x * x
    hi = sq.astype(BF16)
    lo = (sq - hi.astype(F32)).astype(BF16)
    s = jnp.dot(hi, ones_bd, preferred_element_type=F32) + jnp.dot(lo, ones_bd, preferred_element_type=F32)
    return s * (1.0 / GQA_DIM)


def _gqa_prep_body(p, cos_ref, sin_ref, gq_ref, gk_ref, ones_ref, eye_ref, qt_ref, k_ref, vt_ref):
    cos, sin = cos_ref[...], sin_ref[...]
    nq = GQA_HEADS * GQA_DIM
    nk = GQA_KV_HEADS * GQA_DIM
    q = p[:, 0:nq]
    k = p[:, nq:nq + nk]
    v = p[:, nq + nk:nq + 2 * nk].astype(BF16)
    q_ms = _head_mean_sq(q, ones_ref[...])
    k_ms = _head_mean_sq(k, ones_ref[0:nk, 0:nk])
    for h in range(GQA_KV_HEADS):
        sel = eye_ref[h * GQA_DIM:(h + 1) * GQA_DIM, 0:nk]
        vt_ref[0, h, 0, 0:GQA_DIM] = lax.dot_general(sel, v, _NT, preferred_element_type=F32).astype(BF16)
        vt_ref[0, h, 0, GQA_DIM:GQA_DIM + ONES_ROWS] = jnp.ones((ONES_ROWS, v.shape[0]), BF16)
    qn = q * lax.rsqrt(q_ms + EPS) * gq_ref[...]
    qr = (_rope(qn, cos, sin, GQA_DIM // 4) * (GQA_DIM ** -0.5 * LOG2E)).astype(BF16)
    kn = k * lax.rsqrt(k_ms + EPS) * gk_ref[...]
    kr = _rope(kn, cos[:, 0:nk], sin[:, 0:nk], GQA_DIM // 4)
    for h in range(GQA_HEADS):
        sel = eye_ref[h * GQA_DIM:(h + 1) * GQA_DIM, :]
        qt_ref[0, h] = lax.dot_general(sel, qr, _NT, preferred_element_type=F32).astype(BF16)
    for h in range(GQA_KV_HEADS):
        k_ref[0, h] = kr[:, h * GQA_DIM:(h + 1) * GQA_DIM].astype(BF16)


def _inproj_kernel(x_ref, xp_ref, xn_ref, sc_ref, sh_ref, w_ref,
                   mcos_ref, msin_ref, gcq_ref, gckv_ref, wq_ref, wk_ref, wvt_ref, eye_m_ref,
                   gcos_ref, gsin_ref, gq_ref, gk_ref, ones_ref, eye_g_ref, cw_ref, cb_ref,
                   gate_m_ref, gate_g_ref, ssm_ref, hv_ref, hx1_ref, hx2_ref, hsg_ref,
                   mqt_ref, mk_ref, mvt_ref, gqt_ref, gk_out_ref, gvt_ref, *, nct, n_tiles):
    i = pl.program_id(1)

    tm = x_ref.shape[1]

    def normed(x):
        ms = jnp.mean(x * x, axis=-1, keepdims=True)
        return (x * lax.rsqrt(ms + EPS) * sc_ref[0] + sh_ref[0]).astype(BF16)

    halo = normed(jnp.concatenate([xp_ref[0], xn_ref[0]], axis=0))
    hb = normed(x_ref[0])

    def proj(c0, width, lhs=hb):
        return jnp.dot(lhs, w_ref[:, c0:c0 + width], preferred_element_type=F32)

    qkv = 2 * GROUP_W
    hy0 = MLA_PACK + GQA_PACK + SSM_PACK
    p_mla = proj(0, qkv)
    p_gqa = proj(MLA_PACK, qkv)
    p_hy_ext = proj(hy0, HY_PACK, jnp.concatenate([hb, halo], axis=0))
    p_hy = p_hy_ext[0:tm]
    halo_prev = p_hy_ext[tm + 7:tm + 8, 0:3 * HY_W]
    halo_next = p_hy_ext[tm + 8:tm + 9, 0:3 * HY_W]
    gate_m_ref[0] = proj(qkv, GROUP_W).astype(BF16)
    gate_g_ref[0] = proj(MLA_PACK + qkv, GROUP_W).astype(BF16)
    ssm_ref[...] = proj(MLA_PACK + GQA_PACK, SSM_PACK).astype(BF16)
    seq_start = (i == 0) | (i == nct)
    seq_end = (i == nct - 1) | (i == n_tiles - 1)
    _hy_pre_body(p_hy, jnp.where(seq_start, 0.0, halo_prev), jnp.where(seq_end, 0.0, halo_next),
                 cw_ref, cb_ref, hv_ref, hx1_ref, hx2_ref, hsg_ref)
    _mla_prep_body(p_mla, mcos_ref, msin_ref, gcq_ref, gckv_ref, wq_ref, wk_ref, wvt_ref, eye_m_ref,
                   mqt_ref, mk_ref, mvt_ref)
    _gqa_prep_body(p_gqa, gcos_ref, gsin_ref, gq_ref, gk_ref, ones_ref, eye_g_ref,
                   gqt_ref, gk_out_ref, gvt_ref)


def _inproj(xa, scale, shift, w, nct, mla_args, gqa_args, hy_args):
    b, lt, d = xa.shape
    tm = ROW_TILE
    nq = GQA_HEADS * GQA_DIM
    nk = GQA_KV_HEADS * GQA_DIM
    n_tiles = lt // tm
    sub = tm // 8

    def mod_idx(bi, i):
        return (jnp.where(i < nct, b, bi), 0, 0)

    const2 = lambda bi, i: (0, 0)
    const3 = lambda bi, i: (0, 0, 0)
    row = lambda bi, i: (bi, i, 0)
    pos = lambda bi, i: (i, 0)
    qt_idx = lambda bi, i: (bi, 0, 0, i)
    k_idx = lambda bi, i: (bi, 0, i, 0)
    vt_idx = lambda bi, i: (bi, 0, i, 0, 0)
    in_specs = [pl.BlockSpec((1, tm, d), row),
                pl.BlockSpec((1, 8, d), lambda bi, i: (bi, jnp.maximum(i * sub - 1, 0), 0)),
                pl.BlockSpec((1, 8, d), lambda bi, i: (bi, jnp.minimum((i + 1) * sub, lt // 8 - 1), 0)),
                pl.BlockSpec((1, 1, d), mod_idx), pl.BlockSpec((1, 1, d), mod_idx),
                pl.BlockSpec((d, IN_PACK), const2),
                pl.BlockSpec((tm, MLA_DK), pos), pl.BlockSpec((tm, MLA_DK), pos),
                pl.BlockSpec((1, 256), const2), pl.BlockSpec((1, MLA_KV_RANK), const2),
                pl.BlockSpec((MLA_HEADS, 256, MLA_DK), const3),
                pl.BlockSpec((MLA_HEADS, MLA_KV_RANK, MLA_DK), const3),
                pl.BlockSpec((MLA_HEADS, MLA_V, MLA_KV_RANK), const3),
                pl.BlockSpec((MLA_DK, MLA_DK), const2),
                pl.BlockSpec((tm, nq), pos), pl.BlockSpec((tm, nq), pos),
                pl.BlockSpec((1, nq), const2), pl.BlockSpec((1, nk), const2),
                pl.BlockSpec((nq, nq), const2), pl.BlockSpec((nq, nq), const2),
                pl.BlockSpec((3, 3 * HY_W), const2), pl.BlockSpec((1, 3 * HY_W), const2)]
    out_specs = [pl.BlockSpec((1, tm, GROUP_W), row), pl.BlockSpec((1, tm, GROUP_W), row),
                 pl.BlockSpec((tm, SSM_PACK), lambda bi, i: (i, bi))] + [pl.BlockSpec((1, tm, HY_W), row)] * 4 + [
                 pl.BlockSpec((1, MLA_HEADS, MLA_DK, tm), qt_idx),
                 pl.BlockSpec((1, MLA_HEADS, tm, MLA_DK), k_idx),
                 pl.BlockSpec((1, MLA_HEADS, 1, MLA_V + ONES_ROWS, tm), vt_idx),
                 pl.BlockSpec((1, GQA_HEADS, GQA_DIM, tm), qt_idx),
                 pl.BlockSpec((1, GQA_KV_HEADS, tm, GQA_DIM), k_idx),
                 pl.BlockSpec((1, GQA_KV_HEADS, 1, GQA_DIM + ONES_ROWS, tm), vt_idx)]
    out_shape = [jax.ShapeDtypeStruct((b, lt, GROUP_W), BF16), jax.ShapeDtypeStruct((b, lt, GROUP_W), BF16),
                 jax.ShapeDtypeStruct((lt, b * SSM_PACK), BF16)] + [
                 jax.ShapeDtypeStruct((b, lt, HY_W), BF16)] * 4 + [
                 jax.ShapeDtypeStruct((b, MLA_HEADS, MLA_DK, lt), BF16),
                 jax.ShapeDtypeStruct((b, MLA_HEADS, lt, MLA_DK), BF16),
                 jax.ShapeDtypeStruct((b, MLA_HEADS, n_tiles, MLA_V + ONES_ROWS, tm), BF16),
                 jax.ShapeDtypeStruct((b, GQA_HEADS, GQA_DIM, lt), BF16),
                 jax.ShapeDtypeStruct((b, GQA_KV_HEADS, lt, GQA_DIM), BF16),
                 jax.ShapeDtypeStruct((b, GQA_KV_HEADS, n_tiles, GQA_DIM + ONES_ROWS, tm), BF16)]
    outs = pl.pallas_call(
        functools.partial(_inproj_kernel, nct=nct, n_tiles=n_tiles),
        grid=(b, n_tiles),
        in_specs=in_specs,
        out_specs=out_specs,
        out_shape=out_shape,
        compiler_params=_cparams("parallel", "parallel"),
        name="inproj",
    )(xa, xa, xa, scale, shift, w, *mla_args, jnp.eye(MLA_DK, dtype=BF16), *gqa_args,
      jnp.eye(nq, dtype=BF16), *hy_args)
    return outs[0], outs[1], outs[2], tuple(outs[3:7]), tuple(outs[7:10]), tuple(outs[10:13])


def _attn_kernel(qt_ref, qn_ref, k_ref, vt_ref, gate_ref, eye_ref, o_ref, sa_ref, sb_ref, pa_ref, pb_ref,
                 cm_ref, acc_ref, *, heads, group, nct_q, ctx_chunks, all_chunks, unroll):
    i = pl.program_id(1)
    n_loops = jnp.where(i < nct_q, (ctx_chunks - 1) // unroll, (all_chunks - 1) // unroll)
    tq = qt_ref.shape[3]
    dva = vt_ref.shape[3]
    dv = dva - ONES_ROWS

    def score(j, h, dst, q_ref=qt_ref):
        rows = pl.ds(pl.multiple_of(j * ATT_TK, ATT_TK), ATT_TK)
        s = jnp.dot(k_ref[0, h // group, rows, :], q_ref[0, h], preferred_element_type=F32)
        dst[h] = s
        return jnp.max(s, axis=0, keepdims=True)

    def value(j, h, p_ref):
        return jnp.dot(vt_ref[0, h // group, j], p_ref[h], preferred_element_type=F32)

    def step(j, carry, s_cur, s_nxt, p_prev, p_cur):
        state, cmax = carry
        new, nmax = [], []
        for h in range(heads):
            pv = value(jnp.maximum(j - 1, 0), h, p_prev)
            if s_nxt is not None:
                nmax.append(score(j + 1, h, s_nxt))
            m = state[h]
            m_new = jnp.maximum(m, cmax[h])
            alpha = jnp.exp2(m - m_new)
            p_cur[h] = jnp.exp2(s_cur[h] - m_new).astype(BF16)
            acc_ref[h] = alpha * (acc_ref[h] + pv)
            new.append(m_new)
        return tuple(new), tuple(nmax)

    bufs = ((sa_ref, sb_ref, pb_ref, pa_ref), (sb_ref, sa_ref, pa_ref, pb_ref))

    def body(t, carry):
        for u in range(unroll):
            carry = step(unroll * t + u, carry, *bufs[u % 2])
        return carry

    @pl.when(i == 0)
    def _():
        for h in range(heads):
            cm_ref[h] = score(0, h, sa_ref)

    cmax0 = tuple(cm_ref[h] for h in range(heads))
    pb_ref[...] = jnp.zeros_like(pb_ref)
    acc_ref[...] = jnp.zeros_like(acc_ref)
    init = tuple(jnp.full((1, tq), -jnp.inf, F32) for _ in range(heads))
    carry = lax.fori_loop(0, n_loops, body, (init, cmax0))
    last = unroll * n_loops
    step(last, carry, sa_ref, None, pb_ref, pa_ref)
    pvs = [value(last, h, pa_ref) for h in range(heads)]
    for h in range(heads):
        cm_ref[h] = score(0, h, sa_ref, qn_ref)
    outs = []
    for h in range(heads):
        acc = acc_ref[h] + pvs[h]
        outs.append((acc[0:dv] * (1.0 / acc[dv:dv + 1])).astype(BF16))
    out = lax.dot_general(jnp.concatenate(outs, axis=0), eye_ref[...], _TN, preferred_element_type=F32)
    g = gate_ref[0].astype(F32)
    o_ref[0] = (out * _silu(g)).astype(BF16)


def _attention(qt, k, vt, gate_src, gate_block, n_ctx):
    b, heads, dk, lt = qt.shape
    hk, n_chunks, dva = k.shape[1], vt.shape[2], vt.shape[3]
    assert heads * (dva - ONES_ROWS) == GROUP_W
    n_tiles = lt // ATT_TQ
    ctx_chunks, all_chunks = n_ctx // ATT_TK, lt // ATT_TK
    unroll = next(u for u in ATT_UNROLLS if (ctx_chunks - 1) % u == 0 and (all_chunks - 1) % u == 0)
    kern = functools.partial(_attn_kernel, heads=heads, group=heads // hk, nct_q=n_ctx // ATT_TQ,
                             ctx_chunks=ctx_chunks, all_chunks=all_chunks, unroll=unroll)
    return pl.pallas_call(
        kern,
        grid=(b, n_tiles),
        in_specs=[pl.BlockSpec((1, heads, dk, ATT_TQ), lambda bi, i: (bi, 0, 0, i)),
                  pl.BlockSpec((1, heads, dk, ATT_TQ), lambda bi, i: (bi, 0, 0, jnp.minimum(i + 1, n_tiles - 1))),
                  pl.BlockSpec((1, hk, lt, dk), lambda bi, i: (bi, 0, 0, 0)),
                  pl.BlockSpec((1, hk, n_chunks, dva, ATT_TK), lambda bi, i: (bi, 0, 0, 0, 0)),
                  pl.BlockSpec((1, ATT_TQ, GROUP_W), lambda bi, i: (bi, i, gate_block)),
                  pl.BlockSpec((GROUP_W, GROUP_W), lambda bi, i: (0, 0))],
        out_specs=pl.BlockSpec((1, ATT_TQ, GROUP_W), lambda bi, i: (bi, i, 0)),
        out_shape=jax.ShapeDtypeStruct((b, lt, GROUP_W), BF16),
        scratch_shapes=[pltpu.VMEM((heads, ATT_TK, ATT_TQ), F32), pltpu.VMEM((heads, ATT_TK, ATT_TQ), F32),
                        pltpu.VMEM((heads, ATT_TK, ATT_TQ), BF16), pltpu.VMEM((heads, ATT_TK, ATT_TQ), BF16),
                        pltpu.VMEM((heads, 1, ATT_TQ), F32), pltpu.VMEM((heads, dva, ATT_TQ), F32)],
        compiler_params=_cparams("parallel", "arbitrary"),
        name="attention",
    )(qt, qt, k, vt, gate_src, jnp.eye(GROUP_W, dtype=BF16))


def _ssm_kernel(*refs, steps, nb, reverse, finish):
    if finish:
        u_ref, b_ref, a_ref, c_ref, yf_ref, d_ref, gw_ref, gb_ref, o_ref, st, car = refs
    else:
        u_ref, b_ref, a_ref, c_ref, o_ref, st, car = refs

    @pl.when(pl.program_id(0) == 0)
    def _():
        car[...] = jnp.zeros_like(car)

    half = st.shape[0] // 2
    halves = (slice(0, half), slice(half, 2 * half))
    for hs in halves:
        st[hs] = jnp.dot(u_ref[hs, 0:GROUP_W], b_ref[...], preferred_element_type=F32)
    n_state = st.shape[1] // 2
    lw = min(SSM_LANE_SPLIT, n_state)
    for c0 in range(0, n_state, lw):
        lr = slice(c0, c0 + lw)
        li = slice(n_state + c0, n_state + c0 + lw)
        ar = jnp.broadcast_to(a_ref[:, lr], (nb, lw))
        ai = jnp.broadcast_to(a_ref[:, li], (nb, lw))

        def body(tt, carry, lr=lr, li=li, ar=ar, ai=ai):
            sr, si = carry
            t = (steps - 1 - tt) if reverse else tt
            rows = pl.ds(pl.multiple_of(t * nb, nb), nb)
            nr = ar * sr - ai * si + st[rows, lr]
            ni = ar * si + ai * sr + st[rows, li]
            st[rows, lr] = nr
            st[rows, li] = ni
            return nr, ni

        sr, si = lax.fori_loop(0, steps, body, (car[:, lr], car[:, li]), unroll=2)
        car[:, lr] = sr
        car[:, li] = si
    ys = [jnp.dot(st[hs].astype(BF16), c_ref[...], preferred_element_type=F32) for hs in halves]
    if not finish:
        for hs, y in zip(halves, ys):
            o_ref[hs] = y
    else:
        zs = [jax.nn.gelu(y + yf_ref[hs] + d_ref[...] * u_ref[hs, 0:GROUP_W].astype(F32), approximate=True)
              for hs, y in zip(halves, ys)]
        gls = [jnp.dot(z.astype(BF16), gw_ref[...], preferred_element_type=F32) + gb_ref[...] for z in zs]
        for hs, z, gl in zip(halves, zs, gls):
            gate = u_ref[hs, GROUP_W:2 * GROUP_W].astype(F32)
            o_ref[hs] = (z * jax.nn.sigmoid(gl) * _silu(gate)).astype(BF16)


def _ssm_direction(u_tm, mats, nb, n_ctx_chunks, reverse, finish_args=None):
    rows, _ = u_tm.shape
    steps = SSM_T
    blk = steps * nb
    n_chunks = rows // blk
    n_state2 = mats[0].shape[1]
    if reverse:
        cidx = lambda i: (jnp.where(i < n_ctx_chunks, n_ctx_chunks - 1 - i, n_chunks - 1 - (i - n_ctx_chunks)), 0)
    else:
        cidx = lambda i: (i, 0)
    const = lambda i: (0, 0)
    in_specs = [pl.BlockSpec((blk, SSM_PACK), cidx)] + [pl.BlockSpec(m.shape, const) for m in mats]
    args = [u_tm, *mats]
    finish = finish_args is not None
    if finish:
        yf, d_skip, glu_w, glu_b = finish_args
        in_specs += [pl.BlockSpec((blk, GROUP_W), cidx), pl.BlockSpec(d_skip.shape, const),
                     pl.BlockSpec(glu_w.shape, const), pl.BlockSpec(glu_b.shape, const)]
        args += [yf, d_skip, glu_w, glu_b]
    return pl.pallas_call(
        functools.partial(_ssm_kernel, steps=steps, nb=nb, reverse=reverse, finish=finish),
        grid=(n_chunks,),
        in_specs=in_specs,
        out_specs=pl.BlockSpec((blk, GROUP_W), cidx),
        out_shape=jax.ShapeDtypeStruct((rows, GROUP_W), BF16 if finish else F32),
        scratch_shapes=[pltpu.VMEM((blk, n_state2), F32), pltpu.VMEM((nb, n_state2), F32)],
        compiler_params=_cparams("arbitrary"),
        name="ssm_rev" if reverse else "ssm_fwd",
    )(*args)


def _ssm_matrices(lam_re, lam_im, log_step, b_re, b_im, c_re, c_im):
    lr, li = lam_re.astype(F32), lam_im.astype(F32)
    step = jnp.exp(log_step.astype(F32))[:, None]
    mag = jnp.exp(lr * step)
    a_re, a_im = mag * jnp.cos(li * step), mag * jnp.sin(li * step)
    den = lr * lr + li * li
    q_re = ((a_re - 1.0) * lr + a_im * li) / den
    q_im = (a_im * lr - (a_re - 1.0) * li) / den
    bb_re, bb_im = b_re.astype(F32), b_im.astype(F32)
    bbar_re = q_re[..., None] * bb_re - q_im[..., None] * bb_im
    bbar_im = q_re[..., None] * bb_im + q_im[..., None] * bb_re
    eye = jnp.eye(SSM_GROUPS, dtype=F32)
    n_in = SSM_GROUPS * SSM_GROUP
    n_state = SSM_GROUPS * SSM_STATE

    def drive(m):
        return jnp.einsum('gph,gk->ghkp', m, eye).reshape(n_in, n_state).astype(BF16)

    def readout(m):
        return jnp.einsum('ghp,gk->gpkh', m, eye).reshape(n_state, n_in).astype(BF16)

    return (jnp.concatenate([drive(bbar_re), drive(bbar_im)], axis=1),
            jnp.concatenate([a_re.reshape(1, n_state), a_im.reshape(1, n_state)], axis=1),
            jnp.concatenate([readout(c_re.astype(F32)), readout(-c_im.astype(F32))], axis=0))


def _hy_pre_body(p, prev_row, next_row, w_ref, b_ref, v_ref, x1_ref, x2_ref, sg_ref):
    nconv = 3 * HY_W
    x = p[:, 0:nconv]
    tm = x.shape[0]
    rid = lax.broadcasted_iota(jnp.int32, x.shape, 0)
    xm = jnp.where(rid == 0, prev_row, pltpu.roll(x, 1, 0))
    xp = jnp.where(rid == tm - 1, next_row, pltpu.roll(x, tm - 1, 0))
    proj = xm * w_ref[0:1] + x * w_ref[1:2] + xp * w_ref[2:3] + b_ref[...]
    for o_ref, val in ((v_ref, proj[:, 0:HY_W]), (x1_ref, proj[:, HY_W:2 * HY_W]),
                       (x2_ref, proj[:, 2 * HY_W:3 * HY_W]), (sg_ref, _silu(p[:, nconv:nconv + HY_W]))):
        o_ref[0] = val.astype(BF16)


def _bitrev(p, bits):
    r = 0
    for i in range(bits):
        r = (r << 1) | ((p >> i) & 1)
    return r


@functools.lru_cache(maxsize=None)
def _fft_plan(n):
    r = FFT_R
    nb = n // r
    m = 2 * nb
    big = 2 * n
    bits = int(round(math.log2(nb)))
    assert nb >= 2 and (1 << bits) == nb
    jmap = np.zeros(m, np.int64)
    for p in range(nb):
        jmap[p] = 2 * _bitrev(p, bits)
        jmap[nb + p] = 2 * _bitrev(p, bits) + 1
    k2 = np.arange(r)
    n2 = np.arange(r)
    g = np.zeros((m, 2 * r, 2 * r), np.float32)
    for s in range(m):
        ang = -2.0 * np.pi * (np.outer(k2, n2) / r + np.outer(np.ones(r), n2) * jmap[s] / big)
        gr, gi = np.cos(ang), np.sin(ang)
        g[s] = np.block([[gr, -gi], [gi, gr]])
    half0 = nb // 2
    k = np.arange(half0)
    wnb = np.exp(-2j * np.pi * k / nb)
    wm0 = np.exp(-2j * np.pi * k / m)
    wm1 = np.exp(-2j * np.pi * (k + half0) / m)
    tw_edge = np.stack([wnb.real, wnb.imag, wm0.real, wm0.imag, wm1.real, wm1.imag]).astype(np.float32)
    mids = []
    h = half0 // 2
    while h >= 1:
        kk = np.arange(nb // 2) % h
        w = np.exp(-2j * np.pi * kk / (2 * h))
        mids.append(np.stack([w.real, w.imag]))
        h //= 2
    tw_mid = (np.concatenate(mids, 0) if mids else np.zeros((2, max(nb // 2, 1)))).astype(np.float32)
    freq = (jmap[:, None] + m * k2[None, :])
    return dict(nb=nb, m=m, g=g, tw_edge=tw_edge, tw_mid=tw_mid, n_mid=len(mids), freq=freq)


def _cmul(ar, ai, wr, wi):
    return ar * wr - ai * wi, ar * wi + ai * wr


def _slot_pair(bb, lg, nb):
    half0 = nb // 2
    branch = bb // half0
    bf = bb - branch * half0
    i0 = branch * nb + ((bf >> lg) << (lg + 1)) + (bf & ((1 << lg) - 1))
    return bf, i0, i0 + (1 << lg)


def _block_dft_forward(read, w, tw_edge, tw_mid, nb, n_mid):
    r = FFT_R
    half0 = nb // 2
    re, im = slice(0, r), slice(r, 2 * r)

    def first(k, c):
        wr, wi = tw_edge[0, k], tw_edge[1, k]
        ar, ai = read(k)
        br, bi = read(k + half0)
        w[k, re], w[k, im] = ar + br, ai + bi
        dr, di = _cmul(ar - br, ai - bi, wr, wi)
        w[k + half0, re], w[k + half0, im] = dr, di
        a2r, a2i = _cmul(ar, ai, tw_edge[2, k], tw_edge[3, k])
        b2r, b2i = _cmul(br, bi, tw_edge[4, k], tw_edge[5, k])
        w[nb + k, re], w[nb + k, im] = a2r + b2r, a2i + b2i
        dr, di = _cmul(a2r - b2r, a2i - b2i, wr, wi)
        w[nb + k + half0, re], w[nb + k + half0, im] = dr, di
        return c

    lax.fori_loop(0, half0, first, 0)
    _mid_stages(w, tw_mid, nb, n_mid, inverse=False)


def _mid_stages(w, tw_mid, nb, n_mid, inverse):
    r = FFT_R
    half0 = nb // 2
    re, im = slice(0, r), slice(r, 2 * r)
    lg0 = int(round(math.log2(half0))) - 1
    sign = -1.0 if inverse else 1.0

    def tw(s, k):
        return tw_mid[2 * s, k], sign * tw_mid[2 * s + 1, k]

    def bfly(a, b, t):
        (ar, ai), (br, bi) = a, b
        if inverse:
            br, bi = _cmul(br, bi, *t)
            return (ar + br, ai + bi), (ar - br, ai - bi)
        return (ar + br, ai + bi), _cmul(ar - br, ai - bi, *t)

    def single(s):
        lg = lg0 - s

        def body(bb, c):
            bf, i0, i1 = _slot_pair(bb, lg, nb)
            x0, x1 = bfly((w[i0, re], w[i0, im]), (w[i1, re], w[i1, im]), tw(s, bf))
            w[i0, re], w[i0, im] = x0
            w[i1, re], w[i1, im] = x1
            return c

        lax.fori_loop(0, nb, body, 0)

    def double(s):
        lg2 = lg0 - s - 1
        h2 = 1 << lg2
        quarter = nb // 4

        def body(bb, c):
            branch = bb // quarter
            u = bb - branch * quarter
            k = u & (h2 - 1)
            i0 = branch * nb + ((u >> lg2) << (lg2 + 2)) + k
            idx = [i0, i0 + h2, i0 + 2 * h2, i0 + 3 * h2]
            x = [(w[i, re], w[i, im]) for i in idx]
            outer = lambda x: (bfly(x[0], x[2], tw(s, k)), bfly(x[1], x[3], tw(s, k + h2)))
            inner = lambda x: (bfly(x[0], x[1], tw(s + 1, k)), bfly(x[2], x[3], tw(s + 1, k)))
            if inverse:
                (x0, x1), (x2, x3) = inner(x)
                (x0, x2), (x1, x3) = outer([x0, x1, x2, x3])
            else:
                (x0, x2), (x1, x3) = outer(x)
                (x0, x1), (x2, x3) = inner([x0, x1, x2, x3])
            for i, v in zip(idx, (x0, x1, x2, x3)):
                w[i, re], w[i, im] = v
            return c

        lax.fori_loop(0, nb // 2, body, 0)

    lone = [0] if n_mid % 2 else []
    pairs = list(range(len(lone), n_mid, 2))
    if inverse:
        for s in reversed(pairs):
            double(s)
        for s in lone:
            single(s)
    else:
        for s in lone:
            single(s)
        for s in pairs:
            double(s)


def _hy_conv_kernel(*refs, nb, n_mid, n_post):
    tw_edge, tw_mid, a_ref, bias_ref, kh_ref, g_ref, gi_ref = refs[:7]
    post_refs = refs[7:7 + n_post]
    o_ref, w = refs[7 + n_post], refs[8 + n_post]
    r = FFT_R
    m = 2 * nb
    half0 = nb // 2
    re, im = slice(0, r), slice(r, 2 * r)

    def rows(k):
        return pl.ds(pl.multiple_of(k * r, r), r)

    _block_dft_forward(lambda k: (a_ref[0, rows(k), :].astype(F32), a_ref[1, rows(k), :].astype(F32)),
                       w, tw_edge, tw_mid, nb, n_mid)

    per_trip = min(FFT_SLOTS, m)

    def spectral(t, c):
        slots = [per_trip * t + u for u in range(per_trip)]
        xs = [jnp.dot(g_ref[s], w[s].astype(BF16), preferred_element_type=F32) for s in slots]
        outs = []
        for s, x in zip(slots, xs):
            yr, yi = _cmul(x[re], x[im], kh_ref[0, s, re, :], kh_ref[0, s, im, :])
            y = jnp.concatenate([yr, yi], axis=0).astype(BF16)
            outs.append(jnp.dot(gi_ref[s], y, preferred_element_type=F32))
        for s, o in zip(slots, outs):
            w[s] = o
        return c

    lax.fori_loop(0, m // per_trip, spectral, 0)

    _mid_stages(w, tw_mid, nb, n_mid, inverse=True)

    bias = bias_ref[0]

    def emit(k, yr, yi):
        for bsel, y in ((0, yr), (1, yi)):
            a = a_ref[bsel, rows(k), :].astype(F32)
            val = y + bias * a
            for p_ref in post_refs:
                val = val * p_ref[bsel, rows(k), :].astype(F32)
            o_ref[bsel, rows(k), :] = val.astype(o_ref.dtype)

    def last(k, c):
        wr, wi = tw_edge[0, k], -tw_edge[1, k]
        ar, ai = w[k, re], w[k, im]
        br, bi = _cmul(w[k + half0, re], w[k + half0, im], wr, wi)
        cr, ci = w[nb + k, re], w[nb + k, im]
        dr, di = _cmul(w[nb + k + half0, re], w[nb + k + half0, im], wr, wi)
        o0r, o0i = _cmul(cr + dr, ci + di, tw_edge[2, k], -tw_edge[3, k])
        o1r, o1i = _cmul(cr - dr, ci - di, tw_edge[4, k], -tw_edge[5, k])
        emit(k, ar + br + o0r, ai + bi + o0i)
        emit(k + half0, ar - br + o1r, ai - bi + o1i)
        return c

    lax.fori_loop(0, half0, last, 0)


def _hy_conv(a, khat, bias, posts, n, out_dtype):
    b, _, width = a[0].shape
    lanes = LANES
    halves = width // lanes
    plan = _fft_plan(n)
    nb, m = plan['nb'], plan['m']
    r = FFT_R

    def window(row0):
        return pl.BlockSpec((pl.Element(2), pl.Element(n), pl.Element(lanes)),
                            lambda hf, pr: (2 * pr, row0, lanes * hf))

    sig = window(a[1])
    smem = pl.BlockSpec(memory_space=pltpu.SMEM)
    once = pl.Buffered(1)
    in_specs = [smem, smem, sig,
                pl.BlockSpec((1, 1, lanes), lambda hf, pr: (hf, 0, 0)),
                pl.BlockSpec((1, m, 2 * r, lanes), lambda hf, pr: (hf, 0, 0, 0), pipeline_mode=once),
                pl.BlockSpec((m, 2 * r, 2 * r), lambda hf, pr: (0, 0, 0), pipeline_mode=once),
                pl.BlockSpec((m, 2 * r, 2 * r), lambda hf, pr: (0, 0, 0), pipeline_mode=once)]
    in_specs += [window(row0) for _, row0 in posts]
    g_fwd = jnp.asarray(plan['g'], BF16)
    g_inv = jnp.asarray(np.swapaxes(plan['g'], 1, 2), BF16)
    return pl.pallas_call(
        functools.partial(_hy_conv_kernel, nb=nb, n_mid=plan['n_mid'], n_post=len(posts)),
        grid=(halves, b // 2),
        in_specs=in_specs,
        out_specs=pl.BlockSpec((2, n, lanes), lambda hf, pr: (pr, 0, hf)),
        out_shape=jax.ShapeDtypeStruct((b, n, width), out_dtype),
        scratch_shapes=[pltpu.VMEM((m, 2 * r, lanes), F32)],
        compiler_params=pltpu.CompilerParams(dimension_semantics=("arbitrary", "arbitrary"),
                                             vmem_limit_bytes=HY_VMEM_LIMIT),
        name="hyena_conv",
    )(jnp.asarray(plan['tw_edge']), jnp.asarray(plan['tw_mid']), a[0], bias, khat, g_fwd, g_inv,
      *[p for p, _ in posts])


_HP = lax.Precision.HIGHEST
HY_TAPS_W = HY_ORDER * 2 * HY_W
HY_FEAT = 128


def _hy_taps_kernel(z_ref, dec_ref, w1_ref, b1_ref, f1_ref, w2_ref, b2_ref, f2_ref, w3_ref, o_ref,
                    h_scr, ss_scr):
    phase, i = pl.program_id(0), pl.program_id(1)
    tm = z_ref.shape[0]
    rows = pl.ds(pl.multiple_of(i * tm, tm), tm)

    @pl.when(phase == 0)
    def _():
        @pl.when(i == 0)
        def _():
            ss_scr[...] = jnp.zeros_like(ss_scr)

        h = jnp.dot(z_ref[...], w1_ref[...], precision=_HP, preferred_element_type=F32)
        h = jnp.sin(f1_ref[...] * (h + b1_ref[...]))
        h = jnp.dot(h, w2_ref[...], precision=_HP, preferred_element_type=F32)
        h = jnp.sin(f2_ref[...] * (h + b2_ref[...]))
        dec = dec_ref[...]
        for c0 in range(0, HY_TAPS_W, HY_W):
            t = jnp.dot(h, w3_ref[:, c0:c0 + HY_W], precision=_HP, preferred_element_type=F32) * dec
            h_scr[rows, c0:c0 + HY_W] = t
            ss_scr[:, c0:c0 + HY_W] += jnp.sum(t * t, axis=0, keepdims=True)

    @pl.when(phase == 1)
    def _():
        first_row = lax.broadcasted_iota(jnp.int32, (tm, HY_W), 0) == jnp.where(i == 0, 0, -1)
        for o in range(HY_ORDER):
            c0 = o * 2 * HY_W
            tot = ss_scr[:, c0:c0 + HY_W] + ss_scr[:, c0 + HY_W:c0 + 2 * HY_W]
            inv = lax.rsqrt(tot + EPS)
            o_ref[:, c0:c0 + HY_W] = h_scr[rows, c0:c0 + HY_W] * inv
            o_ref[:, c0 + HY_W:c0 + 2 * HY_W] = jnp.where(first_row, 0.0, h_scr[rows, c0 + HY_W:c0 + 2 * HY_W] * inv)


@functools.lru_cache(maxsize=None)
def _filter_features(n):
    t = np.linspace(0.0, 1.0, n, dtype=np.float32)[:, None]
    omega = (2.0 * np.pi * np.arange(n, dtype=np.float32)[:, None] / n).astype(np.float32)
    bands = np.linspace(1e-4, HY_BANDS - 1, HY_BANDS, dtype=np.float32)[None, :]
    z = np.zeros((n, HY_FEAT), np.float32)
    z[:, 0:1] = t
    z[:, 1:1 + HY_BANDS] = np.cos(bands * omega)
    z[:, 1 + HY_BANDS:HY_EMB] = -np.sin(bands * omega)
    max_decay = math.log(HY_DECAY_TARGET) / HY_FAST_DECAY
    min_decay = math.log(HY_DECAY_TARGET) / HY_SLOW_DECAY
    deltas = np.linspace(min_decay, max_decay, HY_W, dtype=np.float32)
    dec = np.exp(-t * np.abs(deltas)).astype(np.float32)
    return z, dec


def _hy_taps(n, w1, b1, fr1, w2, b2, fr2, w3):
    z, dec = _filter_features(n)
    tm = min(n, 512)
    pad = HY_FEAT - HY_HIDDEN
    w1p = jnp.pad(w1.astype(F32), ((0, HY_FEAT - HY_EMB), (0, pad)))
    w2p = jnp.pad(w2.astype(F32), ((0, pad), (0, pad)))
    w3p = jnp.pad(w3.astype(F32), ((0, pad), (0, 0)))
    vec = lambda v: jnp.pad(v.astype(F32), (0, pad)).reshape(1, HY_FEAT)
    const = lambda ph, i: (0, 0)
    return pl.pallas_call(
        _hy_taps_kernel,
        grid=(2, n // tm),
        in_specs=[pl.BlockSpec((tm, HY_FEAT), lambda ph, i: (i * (1 - ph), 0)),
                  pl.BlockSpec((tm, HY_W), lambda ph, i: (i * (1 - ph), 0)),
                  pl.BlockSpec((HY_FEAT, HY_FEAT), const), pl.BlockSpec((1, HY_FEAT), const),
                  pl.BlockSpec((1, HY_FEAT), const),
                  pl.BlockSpec((HY_FEAT, HY_FEAT), const), pl.BlockSpec((1, HY_FEAT), const),
                  pl.BlockSpec((1, HY_FEAT), const),
                  pl.BlockSpec((HY_FEAT, HY_TAPS_W), const)],
        out_specs=pl.BlockSpec((tm, HY_TAPS_W), lambda ph, i: (i * ph, 0)),
        out_shape=jax.ShapeDtypeStruct((n, HY_TAPS_W), F32),
        scratch_shapes=[pltpu.VMEM((n, HY_TAPS_W), F32), pltpu.VMEM((1, HY_TAPS_W), F32)],
        compiler_params=_cparams("arbitrary", "arbitrary"),
        name="hyena_taps",
    )(jnp.asarray(z), jnp.asarray(dec), w1p, vec(b1), vec(fr1), w2p, vec(b2), vec(fr2), w3p)


def _hy_spec_kernel(tw_edge, tw_mid, f_ref, b_ref, ghi_ref, glo_ref, o_ref, w, *, nb, n_mid, inv_n):
    r = FFT_R
    m = 2 * nb
    re, im = slice(0, r), slice(r, 2 * r)
    zeros = jnp.zeros((r, LANES), F32)

    def rows(k):
        return pl.ds(pl.multiple_of(k * r, r), r)

    per_trip = min(4, m)

    def in_block_dfts(t):
        slots = [per_trip * t + u for u in range(per_trip)]
        ys = []
        for s in slots:
            x = w[s]
            xh = x.astype(BF16)
            xl = (x - xh.astype(F32)).astype(BF16)
            gh = ghi_ref[s]
            y = jnp.dot(gh, xh, preferred_element_type=F32)
            y += jnp.dot(gh, xl, preferred_element_type=F32)
            y += jnp.dot(glo_ref[s], xh, preferred_element_type=F32)
            ys.append(y * inv_n)
        return slots, ys

    _block_dft_forward(lambda k: (f_ref[rows(k), :], zeros), w, tw_edge, tw_mid, nb, n_mid)

    def fwd_part(t, c):
        for s, y in zip(*in_block_dfts(t)):
            o_ref[0, 0, s] = y
        return c

    lax.fori_loop(0, m // per_trip, fwd_part, 0)
    _block_dft_forward(lambda k: (b_ref[rows(k), :], zeros), w, tw_edge, tw_mid, nb, n_mid)

    def bwd_part(t, c):
        for s, y in zip(*in_block_dfts(t)):
            o_ref[0, 0, s, re] += y[re]
            o_ref[0, 0, s, im] -= y[im]
        return c

    lax.fori_loop(0, m // per_trip, bwd_part, 0)


def _hy_filter_spectrum(taps):
    n = taps.shape[0]
    plan = _fft_plan(n)
    nb, m = plan['nb'], plan['m']
    r = FFT_R
    halves = HY_W // LANES
    g = plan['g']
    g_hi = g.astype(jnp.bfloat16)
    g_lo = (g - np.asarray(g_hi, np.float32)).astype(jnp.bfloat16)
    smem = pl.BlockSpec(memory_space=pltpu.SMEM)
    once = pl.Buffered(1)
    table = lambda o, hf: (0, 0, 0)
    return pl.pallas_call(
        functools.partial(_hy_spec_kernel, nb=nb, n_mid=plan['n_mid'], inv_n=1.0 / (2 * n)),
        grid=(HY_ORDER, halves),
        in_specs=[smem, smem,
                  pl.BlockSpec((n, LANES), lambda o, hf: (0, o * 2 * halves + hf)),
                  pl.BlockSpec((n, LANES), lambda o, hf: (0, o * 2 * halves + halves + hf)),
                  pl.BlockSpec((m, 2 * r, 2 * r), table, pipeline_mode=once),
                  pl.BlockSpec((m, 2 * r, 2 * r), table, pipeline_mode=once)],
        out_specs=pl.BlockSpec((1, 1, m, 2 * r, LANES), lambda o, hf: (o, hf, 0, 0, 0)),
        out_shape=jax.ShapeDtypeStruct((HY_ORDER, halves, m, 2 * r, LANES), F32),
        scratch_shapes=[pltpu.VMEM((m, 2 * r, LANES), F32)],
        compiler_params=pltpu.CompilerParams(dimension_semantics=("arbitrary", "arbitrary"),
                                             vmem_limit_bytes=HY_VMEM_LIMIT),
        name="hyena_spectrum",
    )(jnp.asarray(plan['tw_edge']), jnp.asarray(plan['tw_mid']), taps, taps,
      jnp.asarray(g_hi), jnp.asarray(g_lo))


def _pack_w_in(w_in):
    d = w_in.shape[0]
    z = lambda n: jnp.zeros((d, n), w_in.dtype)
    o = 0
    q_lat = w_in[:, o:o + MLA_Q_RANK]; o += MLA_Q_RANK
    kv_lat = w_in[:, o:o + MLA_KV_RANK]; o += MLA_KV_RANK
    k_rope = w_in[:, o:o + MLA_ROPE]; o += MLA_ROPE
    gate = w_in[:, o:o + GROUP_W]; o += GROUP_W
    mla = jnp.concatenate([q_lat, z(256 - MLA_Q_RANK), kv_lat, z(MLA_NOPE), k_rope,
                           z(MLA_DK - MLA_NOPE - MLA_ROPE), gate], axis=1)
    return jnp.concatenate([mla, w_in[:, o:]], axis=1).astype(BF16)


def _pack_mla_up(w_uq, w_ukv):
    dq = MLA_NOPE + MLA_ROPE
    wq = w_uq.reshape(MLA_Q_RANK, MLA_HEADS, dq).transpose(1, 0, 2)
    wq = jnp.pad(wq, ((0, 0), (0, 256 - MLA_Q_RANK), (0, MLA_DK - dq))).astype(BF16)
    wkv = w_ukv.reshape(MLA_KV_RANK, MLA_HEADS, MLA_NOPE + MLA_V).transpose(1, 0, 2)
    wk = jnp.pad(wkv[:, :, :MLA_NOPE], ((0, 0), (0, 0), (0, MLA_DK - MLA_NOPE))).astype(BF16)
    wvt = jnp.swapaxes(wkv[:, :, MLA_NOPE:], 1, 2).astype(BF16)
    return wq, wk, wvt


def kernel(x, c, ctx, c_ctx, w_mod, b_mod, g_pre, g_post, w_in, w_out, mla_g_cq, mla_w_uq, mla_g_ckv, mla_w_ukv, gqa_g_q, gqa_g_k, ssm_lambda_re, ssm_lambda_im, ssm_log_step, ssm_b_re, ssm_b_im, ssm_c_re, ssm_c_im, ssm_d, ssm_glu_w, ssm_glu_b, hy_conv_w, hy_conv_b, hy_f_w1, hy_f_b1, hy_f_freq1, hy_f_w2, hy_f_b2, hy_f_freq2, hy_f_w3, hy_bias):
    b, n_lat, d = x.shape
    n_ctx = ctx.shape[1]
    depth = w_in.shape[0]
    lt = n_ctx + n_lat
    nct = n_ctx // ROW_TILE
    assert n_ctx % ROW_TILE == 0 and n_lat % ROW_TILE == 0 and b % 2 == 0 and b % 8 == 0

    n_cond = -(-(b + 1) // 8) * 8
    cond = jnp.zeros((n_cond, d), F32).at[:b].set(c).at[b].set(c_ctx)
    mod = _modulation(cond, w_mod, b_mod)[:, :b + 1]
    shift, scale, gate = mod[..., :d], mod[..., d:2 * d], mod[..., 2 * d:]

    mla_cos, mla_sin, gqa_cos, gqa_sin = _rope_tables(n_ctx, n_lat)
    nq = GQA_HEADS * GQA_DIM
    ones_bd = jnp.asarray(np.kron(np.eye(GQA_HEADS), np.ones((GQA_DIM, GQA_DIM))), BF16)

    xa = jnp.concatenate([ctx, x], axis=1)
    for l in range(depth):
        sc = (g_pre[l][None, :] * (1.0 + scale[l]))[:, None, :]
        sh = shift[l][:, None, :]
        wq, wk, wvt = _pack_mla_up(mla_w_uq[l], mla_w_ukv[l])
        g_cq = jnp.pad(mla_g_cq[l], (0, 256 - MLA_Q_RANK)).reshape(1, 256)
        mla_args = (mla_cos, mla_sin, g_cq, mla_g_ckv[l].reshape(1, -1), wq, wk, wvt)
        gqa_args = (gqa_cos, gqa_sin, jnp.tile(gqa_g_q[l], GQA_HEADS).reshape(1, nq),
                    jnp.tile(gqa_g_k[l], GQA_KV_HEADS).reshape(1, -1), ones_bd)
        hy_args = (hy_conv_w[l], hy_conv_b[l].reshape(1, -1))
        gate_m, gate_g, p_ssm, (hv, hx1, hx2, hsg), mla_qkv, gqa_qkv = _inproj(
            xa, sc, sh, _pack_w_in(w_in[l]), nct, mla_args, gqa_args, hy_args)
        a_out = _attention(*mla_qkv, gate_m, 0, n_ctx)
        g_out = _attention(*gqa_qkv, gate_g, 0, n_ctx)

        u_tm = p_ssm.reshape(lt * b, SSM_PACK)
        mats = [_ssm_matrices(ssm_lambda_re[l, di], ssm_lambda_im[l, di], ssm_log_step[l, di],
                              ssm_b_re[l, di], ssm_b_im[l, di], ssm_c_re[l, di], ssm_c_im[l, di])
                for di in range(2)]
        n_ctx_chunks = n_ctx // SSM_T
        y_f = _ssm_direction(u_tm, mats[0], b, n_ctx_chunks, False)
        fin = (y_f, ssm_d[l].reshape(1, -1), ssm_glu_w[l].astype(BF16), ssm_glu_b[l].reshape(1, -1))
        s_tm = _ssm_direction(u_tm, mats[1], b, n_ctx_chunks, True, fin)
        s_out = s_tm.reshape(lt, b * GROUP_W)

        filt = (hy_f_w1[l], hy_f_b1[l], hy_f_freq1[l], hy_f_w2[l], hy_f_b2[l], hy_f_freq2[l], hy_f_w3[l])
        bias = hy_bias[l].astype(F32).reshape(HY_ORDER, HY_W // LANES, 1, LANES)
        parts = []
        for row0, n in ((0, n_ctx), (n_ctx, n_lat)):
            khat = _hy_filter_spectrum(_hy_taps(n, *filt))
            z1 = _hy_conv((hv, row0), khat[0], bias[0], ((hx1, row0),), n, BF16)
            parts.append(_hy_conv((z1, 0), khat[1], bias[1], ((hx2, row0), (hsg, row0)), n, BF16))
        y_out = jnp.concatenate(parts, axis=1)

        xa = _outproj(a_out, g_out, s_out, y_out, xa, gate[l][:, None, :], g_post[l].reshape(1, d),
                      w_out[l].astype(BF16), nct, tile0=nct if l == depth - 1 else 0)
    return xa
```
